```python
import math
import jax, jax.numpy as jnp
from jax import lax
import numpy as np

D_MODEL = 2048
BATCH = 1
SEQ = 8192
DEPTH = 4

N_MIXERS = 3
S5_GROUP = 16
S5_GROUPS = D_MODEL // S5_GROUP
S5_STATE = 64
S5_DT_MIN = 1e-3
S5_DT_MAX = 1e-1
CONV_WIDTH = 31
POOL_WINDOWS = (2, 4, 8, 16)
POOL_GROUPS = len(POOL_WINDOWS)
POOL_CH = D_MODEL // POOL_GROUPS
N_EXPERTS = 16
N_EXPERT_GROUPS = 4
EXPERTS_PER_GROUP = N_EXPERTS // N_EXPERT_GROUPS
TOP_K = 2
D_EXPERT = D_MODEL // 2
MOE_BLOCK = 128
DN_ALPHA = (2 * DEPTH) ** 0.25
DN_BETA = (8 * DEPTH) ** -0.25
LN_EPS = 1e-5
N_S5_LAYERS = (DEPTH + N_MIXERS - 1) // N_MIXERS
N_CONV_LAYERS = (DEPTH + N_MIXERS - 2) // N_MIXERS
N_POOL_LAYERS = (DEPTH + N_MIXERS - 3) // N_MIXERS

kernel_name = "hybrid_s5_conformer_pool_moe_deepnorm"


def layer_norm(x, g, b):
    xf = x.astype(jnp.float32)
    mu = jnp.mean(xf, axis=-1, keepdims=True)
    xc = xf - mu
    var = jnp.mean(xc * xc, axis=-1, keepdims=True)
    return (xc * lax.rsqrt(var + LN_EPS) * g + b).astype(x.dtype)


def s5_mixer(h, w_in, b_re, b_im, c_re, c_im, a_re, a_im, log_step, d_skip, w_glu, w_out):
    bsz, seq, dm = h.shape
    u = h @ w_in
    ug = u.astype(jnp.float32).reshape(bsz, seq, S5_GROUPS, S5_GROUP)
    a_re = a_re.astype(jnp.float32)
    a_im = a_im.astype(jnp.float32)
    dt = jnp.exp(log_step.astype(jnp.float32))[:, None]
    mag = jnp.exp(a_re * dt)
    lb_re = mag * jnp.cos(a_im * dt)
    lb_im = mag * jnp.sin(a_im * dt)
    den = a_re * a_re + a_im * a_im
    n_re = lb_re - 1.0
    n_im = lb_im
    f_re = (n_re * a_re + n_im * a_im) / den
    f_im = (n_im * a_re - n_re * a_im) / den
    bb_re = f_re[..., None] * b_re - f_im[..., None] * b_im
    bb_im = f_re[..., None] * b_im + f_im[..., None] * b_re
    bu_re = jnp.einsum('bsgc,gpc->bsgp', ug, bb_re)
    bu_im = jnp.einsum('bsgc,gpc->bsgp', ug, bb_im)
    al_re = jnp.broadcast_to(lb_re, bu_re.shape)
    al_im = jnp.broadcast_to(lb_im, bu_im.shape)

    def combine(e1, e2):
        a1r, a1i, b1r, b1i = e1
        a2r, a2i, b2r, b2i = e2
        return (a2r * a1r - a2i * a1i,
                a2r * a1i + a2i * a1r,
                a2r * b1r - a2i * b1i + b2r,
                a2r * b1i + a2i * b1r + b2i)

    _, _, st_re, st_im = lax.associative_scan(combine, (al_re, al_im, bu_re, bu_im), axis=1)
    y = (jnp.einsum('bsgp,gcp->bsgc', st_re, c_re)
         - jnp.einsum('bsgp,gcp->bsgc', st_im, c_im)).reshape(bsz, seq, dm)
    y = y + d_skip * u
    z = jax.nn.gelu(y, approximate=False)
    za, zg = jnp.split(z @ w_glu, 2, axis=-1)
    return (za * jax.nn.sigmoid(zg)) @ w_out


def conv_module(h, w_pw1, b_pw1, w_dw, b_dw, ln_g, ln_b, w_pw2, b_pw2):
    dm = h.shape[-1]
    va, vg = jnp.split(h @ w_pw1 + b_pw1, 2, axis=-1)
    v = va * jax.nn.sigmoid(vg)
    v = lax.conv_general_dilated(
        v, w_dw[:, None, :].astype(v.dtype), window_strides=(1,),
        padding=[(CONV_WIDTH - 1, 0)],
        dimension_numbers=('NWC', 'WIO', 'NWC'),
        feature_group_count=dm) + b_dw
    v = jax.nn.silu(layer_norm(v, ln_g, ln_b))
    return v @ w_pw2 + b_pw2


def pool_mixer(h, w_grp, scale):
    bsz, seq, dm = h.shape
    xf = h.astype(jnp.float32)
    csp = jnp.pad(jnp.cumsum(xf, axis=1), ((0, 0), (1, 0), (0, 0)))
    t = jnp.arange(seq)
    outs = []
    for k, w in enumerate(POOL_WINDOWS):
        c = csp[:, :, k * POOL_CH:(k + 1) * POOL_CH]
        upper = c[:, 1:]
        lower = c[:, jnp.maximum(t + 1 - w, 0)]
        cnt = jnp.minimum(t + 1, w).astype(jnp.float32)[:, None]
        outs.append((upper - lower) / cnt - xf[:, :, k * POOL_CH:(k + 1) * POOL_CH])
    pooled = jnp.stack(outs, axis=2)
    y = jnp.einsum('bsgc,gcd->bsgd', pooled, w_grp).reshape(bsz, seq, dm)
    return y * scale


def grouped_moe(h, router_w, router_b, w_gate, w_up, w_down):
    bsz, seq, dm = h.shape
    n_tok = bsz * seq
    xt = h.reshape(n_tok, dm)
    logits = (xt @ router_w + router_b).astype(jnp.float32)
    probs = jax.nn.softmax(logits, axis=-1)
    grp = probs.reshape(n_tok, N_EXPERT_GROUPS, EXPERTS_PER_GROUP)
    grp_score = lax.top_k(grp, TOP_K)[0].sum(-1)
    g_sel = jnp.argmax(grp_score, axis=-1)
    in_grp = (jnp.arange(N_EXPERTS) // EXPERTS_PER_GROUP)[None, :] == g_sel[:, None]
    top_p, top_e = lax.top_k(jnp.where(in_grp, probs, -1.0), TOP_K)
    gates = top_p / jnp.sum(top_p, axis=-1, keepdims=True)

    n_asg = n_tok * TOP_K
    flat_e = top_e.reshape(-1).astype(jnp.int32)
    flat_t = jnp.repeat(jnp.arange(n_tok, dtype=jnp.int32), TOP_K)
    flat_g = gates.reshape(-1)
    order = jnp.argsort(flat_e, stable=True)
    se, st, sg = flat_e[order], flat_t[order], flat_g[order]
    counts = jnp.bincount(flat_e, length=N_EXPERTS)
    pcounts = ((counts + MOE_BLOCK - 1) // MOE_BLOCK) * MOE_BLOCK
    starts = jnp.cumsum(counts) - counts
    pends = jnp.cumsum(pcounts)
    pstarts = pends - pcounts
    dest = pstarts[se] + (jnp.arange(n_asg, dtype=jnp.int32) - starts[se])
    cap = ((n_asg + MOE_BLOCK - 1) // MOE_BLOCK) * MOE_BLOCK + N_EXPERTS * MOE_BLOCK
    n_blk = cap // MOE_BLOCK
    row_tok = jnp.full((cap,), n_tok, jnp.int32).at[dest].set(st)
    row_gate = jnp.zeros((cap,), jnp.float32).at[dest].set(sg)
    blk_e = jnp.minimum(jnp.searchsorted(pends, jnp.arange(n_blk) * MOE_BLOCK, side='right'),
                        N_EXPERTS - 1).astype(jnp.int32)
    x_pad = jnp.concatenate([xt, jnp.zeros((1, dm), xt.dtype)], axis=0)
    xb = x_pad[row_tok].reshape(n_blk, MOE_BLOCK, dm)

    def expert_block(args):
        xblk, e = args
        hid = jax.nn.silu(xblk @ w_gate[e]) * (xblk @ w_up[e])
        return hid @ w_down[e]

    yb = lax.map(expert_block, (xb, blk_e))
    y_rows = yb.reshape(cap, dm) * row_gate[:, None]
    out = jnp.zeros((n_tok + 1, dm), y_rows.dtype).at[row_tok].add(y_rows)[:n_tok]
    return out.reshape(bsz, seq, dm)


def setup_inputs(seed: int = 0) -> dict:
    key = jax.random.key(seed)
    ks = iter(jax.random.split(key, 40))
    f32 = jnp.float32
    D = D_MODEL
    G, P, C = S5_GROUPS, S5_STATE, S5_GROUP

    def nrm(shape, scale):
        return jax.random.normal(next(ks), shape, f32) * scale

    n_idx = jnp.arange(P, dtype=f32)
    inp = {}
    inp['x'] = nrm((BATCH, SEQ, D), 1.0)
    inp['s5_w_in'] = nrm((N_S5_LAYERS, D, D), D ** -0.5)
    inp['s5_b_re'] = nrm((N_S5_LAYERS, G, P, C), (2 * C) ** -0.5)
    inp['s5_b_im'] = nrm((N_S5_LAYERS, G, P, C), (2 * C) ** -0.5)
    inp['s5_c_re'] = nrm((N_S5_LAYERS, G, C, P), (2 * P) ** -0.5)
    inp['s5_c_im'] = nrm((N_S5_LAYERS, G, C, P), (2 * P) ** -0.5)
    inp['s5_a_re'] = -0.5 + nrm((N_S5_LAYERS, G, P), 0.01)
    inp['s5_a_im'] = math.pi * n_idx + nrm((N_S5_LAYERS, G, P), 0.01)
    inp['s5_log_step'] = jax.random.uniform(next(ks), (N_S5_LAYERS, G), f32,
                                            math.log(S5_DT_MIN), math.log(S5_DT_MAX))
    inp['s5_d'] = nrm((N_S5_LAYERS, D), 1.0)
    inp['s5_w_glu'] = nrm((N_S5_LAYERS, D, 2 * D), D ** -0.5)
    inp['s5_w_out'] = nrm((N_S5_LAYERS, D, D), DN_BETA * D ** -0.5)
    inp['cv_w_pw1'] = nrm((N_CONV_LAYERS, D, 2 * D), D ** -0.5)
    inp['cv_b_pw1'] = nrm((N_CONV_LAYERS, 2 * D), 0.02)
    inp['cv_w_dw'] = nrm((N_CONV_LAYERS, CONV_WIDTH, D), CONV_WIDTH ** -0.5)
    inp['cv_b_dw'] = nrm((N_CONV_LAYERS, D), 0.02)
    inp['cv_ln_g'] = 1.0 + nrm((N_CONV_LAYERS, D), 0.02)
    inp['cv_ln_b'] = nrm((N_CONV_LAYERS, D), 0.02)
    inp['cv_w_pw2'] = nrm((N_CONV_LAYERS, D, D), DN_BETA * D ** -0.5)
    inp['cv_b_pw2'] = nrm((N_CONV_LAYERS, D), 0.02)
    inp['pl_w'] = nrm((N_POOL_LAYERS, POOL_GROUPS, POOL_CH, POOL_CH), DN_BETA * POOL_CH ** -0.5)
    inp['pl_scale'] = 1.0 + nrm((N_POOL_LAYERS, D), 0.1)
    inp['router_w'] = nrm((D, N_EXPERTS), D ** -0.5)
    inp['router_b'] = nrm((N_EXPERTS,), 0.01)
    inp['moe_w_gate'] = nrm((DEPTH, N_EXPERTS, D, D_EXPERT), D ** -0.5)
    inp['moe_w_up'] = nrm((DEPTH, N_EXPERTS, D, D_EXPERT), D ** -0.5)
    inp['moe_w_down'] = nrm((DEPTH, N_EXPERTS, D_EXPERT, D), DN_BETA * D_EXPERT ** -0.5)
    inp['ln_mix_g'] = 1.0 + nrm((DEPTH, D), 0.02)
    inp['ln_mix_b'] = nrm((DEPTH, D), 0.02)
    inp['ln_ffn_g'] = 1.0 + nrm((DEPTH, D), 0.02)
    inp['ln_ffn_b'] = nrm((DEPTH, D), 0.02)
    return inp


def reference(x, s5_w_in, s5_b_re, s5_b_im, s5_c_re, s5_c_im, s5_a_re, s5_a_im,
              s5_log_step, s5_d, s5_w_glu, s5_w_out,
              cv_w_pw1, cv_b_pw1, cv_w_dw, cv_b_dw, cv_ln_g, cv_ln_b, cv_w_pw2, cv_b_pw2,
              pl_w, pl_scale, router_w, router_b, moe_w_gate, moe_w_up, moe_w_down,
              ln_mix_g, ln_mix_b, ln_ffn_g, ln_ffn_b):
    h = x
    for i in range(DEPTH):
        m = i % N_MIXERS
        j = i // N_MIXERS
        if m == 0:
            y = s5_mixer(h, s5_w_in[j], s5_b_re[j], s5_b_im[j], s5_c_re[j], s5_c_im[j],
                         s5_a_re[j], s5_a_im[j], s5_log_step[j], s5_d[j], s5_w_glu[j], s5_w_out[j])
        elif m == 1:
            y = conv_module(h, cv_w_pw1[j], cv_b_pw1[j], cv_w_dw[j], cv_b_dw[j],
                            cv_ln_g[j], cv_ln_b[j], cv_w_pw2[j], cv_b_pw2[j])
        else:
            y = pool_mixer(h, pl_w[j], pl_scale[j])
        h = layer_norm(DN_ALPHA * h + y.astype(h.dtype), ln_mix_g[i], ln_mix_b[i])
        f = grouped_moe(h, router_w, router_b, moe_w_gate[i], moe_w_up[i], moe_w_down[i])
        h = layer_norm(DN_ALPHA * h + f.astype(h.dtype), ln_ffn_g[i], ln_ffn_b[i])
    return h
```

```python
import functools
import math

import jax
import jax.numpy as jnp
from jax import lax
from jax.experimental import pallas as pl
from jax.experimental.pallas import tpu as pltpu

f32 = jnp.float32
bf16 = jnp.bfloat16

D_MODEL = 2048
DEPTH = 4
N_MIXERS = 3
S5_GROUP = 16
S5_STATE = 64
CONV_WIDTH = 31
POOL_WINDOWS = (2, 4, 8, 16)
POOL_CH = D_MODEL // len(POOL_WINDOWS)
N_EXPERTS = 16
EXPERTS_PER_GROUP = 4
D_EXPERT = D_MODEL // 2
DN_ALPHA = (2 * DEPTH) ** 0.25
LN_EPS = 1e-5

LANES = 128
SUBLANES = 8
VMEM_LIMIT = 56 * 1024 * 1024
S5_BLOCK = 8
S5_CHUNK_GROUPS = LANES // S5_GROUP
MOE_ROWS = 256
GATHER_ROWS = 128


def _params(*sem):
    return pltpu.CompilerParams(dimension_semantics=sem, vmem_limit_bytes=VMEM_LIMIT)


def _layer_norm(r, g, b):
    mu = jnp.mean(r, axis=-1, keepdims=True)
    xc = r - mu
    var = jnp.mean(xc * xc, axis=-1, keepdims=True)
    return xc * lax.rsqrt(var + LN_EPS) * g + b


def _dot(a, b):
    return jnp.dot(a.astype(bf16), b.astype(bf16), preferred_element_type=f32)


def _mm_kernel(x_ref, w_ref, o_ref):
    o_ref[...] = _dot(x_ref[...], w_ref[...]).astype(o_ref.dtype)


def _mm(x, w, tm=512, tn=1024):
    m, k = x.shape
    n = w.shape[1]
    return pl.pallas_call(
        _mm_kernel,
        grid=(n // tn, m // tm),
        in_specs=[pl.BlockSpec((tm, k), lambda j, i: (i, 0)),
                  pl.BlockSpec((k, tn), lambda j, i: (0, j))],
        out_specs=pl.BlockSpec((tm, tn), lambda j, i: (i, j)),
        out_shape=jax.ShapeDtypeStruct((m, n), f32),
        compiler_params=_params("parallel", "parallel"),
        name="mm",
    )(x, w)


def _mm_glu_kernel(x_ref, wa_ref, wg_ref, ba_ref, bg_ref, o_ref):
    x = x_ref[...].astype(bf16)
    a = _dot(x, wa_ref[...]) + ba_ref[...]
    g = _dot(x, wg_ref[...]) + bg_ref[...]
    o_ref[...] = (a * jax.nn.sigmoid(g)).astype(o_ref.dtype)


def _mm_glu(x, w, b, out_dtype, tm=512, tn=512):
    m, k = x.shape
    n = w.shape[1] // 2
    nb = n // tn
    return pl.pallas_call(
        _mm_glu_kernel,
        grid=(nb, m // tm),
        in_specs=[pl.BlockSpec((tm, k), lambda j, i: (i, 0)),
                  pl.BlockSpec((k, tn), lambda j, i: (0, j)),
                  pl.BlockSpec((k, tn), lambda j, i: (0, j + nb)),
                  pl.BlockSpec((1, tn), lambda j, i: (0, j)),
                  pl.BlockSpec((1, tn), lambda j, i: (0, j + nb))],
        out_specs=pl.BlockSpec((tm, tn), lambda j, i: (i, j)),
        out_shape=jax.ShapeDtypeStruct((m, n), out_dtype),
        compiler_params=_params("parallel", "parallel"),
        name="mm_glu",
    )(x, w, w, b, b)


def _mm_res_ln_kernel(x_ref, w_ref, b_ref, h_ref, g_ref, beta_ref, of_ref, ob_ref):
    y = _dot(x_ref[...], w_ref[...]) + b_ref[...]
    out = _layer_norm(DN_ALPHA * h_ref[...] + y, g_ref[...], beta_ref[...])
    of_ref[...] = out
    ob_ref[...] = out.astype(bf16)


def _mm_res_ln(x, w, b, h, g, beta, tm=256):
    m, k = x.shape
    d = w.shape[1]
    row = lambda i: (i, 0)
    fixed = lambda i: (0, 0)
    return pl.pallas_call(
        _mm_res_ln_kernel,
        grid=(m // tm,),
        in_specs=[pl.BlockSpec((tm, k), row), pl.BlockSpec((k, d), fixed), pl.BlockSpec((1, d), fixed),
                  pl.BlockSpec((tm, d), row), pl.BlockSpec((1, d), fixed), pl.BlockSpec((1, d), fixed)],
        out_specs=[pl.BlockSpec((tm, d), row), pl.BlockSpec((tm, d), row)],
        out_shape=[jax.ShapeDtypeStruct((m, d), f32), jax.ShapeDtypeStruct((m, d), bf16)],
        compiler_params=_params("parallel"),
        name="mm_res_ln",
    )(x, w, b, h, g, beta)


def _cmul_add(xr, xi, ar, ai, sr, si):
    return xr + ar * sr - ai * si, xi + ar * si + ai * sr


def _s5_kernel(u_ref, wagg_ref, t_ref, wout_ref, dtab_ref, d_ref, z_ref, xb_scr, v_scr):
    m_rows = v_scr.shape[0]
    half = v_scr.shape[1] // 2
    ncol = half // LANES
    for i in range(S5_BLOCK):
        xb_scr[:, i * LANES:(i + 1) * LANES] = u_ref[pl.ds(i, m_rows, stride=S5_BLOCK), :].astype(bf16)
    xb = xb_scr[...]
    v_scr[...] = jnp.dot(xb, wagg_ref[0], preferred_element_type=f32)

    tab = dtab_ref[0]
    sub = lax.broadcasted_iota(jnp.int32, (SUBLANES, LANES), 0)

    def col(j, part):
        lo = part * half + j * LANES
        return slice(lo, lo + LANES)

    def bcast(row):
        return jnp.broadcast_to(row, (SUBLANES, LANES))

    def body(r, carry):
        r0 = pl.multiple_of(r * SUBLANES, SUBLANES)
        new = []
        for j in range(ncol):
            cr, ci = carry[2 * j], carry[2 * j + 1]
            xr = v_scr[pl.ds(r0, SUBLANES), col(j, 0)]
            xi = v_scr[pl.ds(r0, SUBLANES), col(j, 1)]
            for shift in (1, 2, 4):
                ar = bcast(tab[shift - 1:shift, col(j, 0)])
                ai = bcast(tab[shift - 1:shift, col(j, 1)])
                sr = jnp.where(sub >= shift, pltpu.roll(xr, shift, 0), 0.0)
                si = jnp.where(sub >= shift, pltpu.roll(xi, shift, 0), 0.0)
                xr, xi = _cmul_add(xr, xi, ar, ai, sr, si)
            xr, xi = _cmul_add(xr, xi, tab[:, col(j, 0)], tab[:, col(j, 1)], cr, ci)
            v_scr[pl.ds(r0, SUBLANES), col(j, 0)] = jnp.where(sub >= 1, pltpu.roll(xr, 1, 0), cr)
            v_scr[pl.ds(r0, SUBLANES), col(j, 1)] = jnp.where(sub >= 1, pltpu.roll(xi, 1, 0), ci)
            new.append(bcast(xr[SUBLANES - 1:SUBLANES, :]))
            new.append(bcast(xi[SUBLANES - 1:SUBLANES, :]))
        return tuple(new)

    zero = jnp.zeros((SUBLANES, LANES), f32)
    lax.fori_loop(0, m_rows // SUBLANES, body, (zero,) * (2 * ncol))

    sp = v_scr[...].astype(bf16)
    for i0 in range(0, S5_BLOCK, 2):
        cs = slice(i0 * LANES, (i0 + 2) * LANES)
        y = (jnp.dot(xb, t_ref[0, :, cs], preferred_element_type=f32)
             + jnp.dot(sp, wout_ref[0, :, cs], preferred_element_type=f32))
        for i in (i0, i0 + 1):
            yi = y[:, (i - i0) * LANES:(i - i0 + 1) * LANES] + d_ref[...] * u_ref[pl.ds(i, m_rows, stride=S5_BLOCK), :]
            z_ref[pl.ds(i, m_rows, stride=S5_BLOCK), :] = 0.5 * yi * (1.0 + lax.erf(yi * (1.0 / math.sqrt(2.0))))


def _s5_core(u, wagg, tmat, wout, dtab, d_skip):
    n, d = u.shape
    nq = d // LANES
    m_rows = n // S5_BLOCK
    width = wagg.shape[1]
    wspec = pl.BlockSpec((1, width, width), lambda q: (q, 0, 0))
    return pl.pallas_call(
        _s5_kernel,
        grid=(nq,),
        in_specs=[pl.BlockSpec((n, LANES), lambda q: (0, q)), wspec, wspec, wspec,
                  pl.BlockSpec((1, SUBLANES, width), lambda q: (q, 0, 0)),
                  pl.BlockSpec((1, LANES), lambda q: (0, q))],
        out_specs=pl.BlockSpec((n, LANES), lambda q: (0, q)),
        out_shape=jax.ShapeDtypeStruct((n, d), f32),
        scratch_shapes=[pltpu.VMEM((m_rows, width), bf16), pltpu.VMEM((m_rows, width), f32)],
        compiler_params=_params("parallel"),
        name="s5_core",
    )(u, wagg, tmat, wout, dtab, d_skip)


def _s5_derive(b_re, b_im, c_re, c_im, a_re, a_im, log_step):
    hp = lax.Precision.HIGHEST
    g, p, c = b_re.shape
    nq = g // S5_CHUNK_GROUPS
    dt = jnp.exp(log_step)[:, None]
    mag = jnp.exp(a_re * dt)
    lb_re = mag * jnp.cos(a_im * dt)
    lb_im = mag * jnp.sin(a_im * dt)
    den = a_re * a_re + a_im * a_im
    n_re = lb_re - 1.0
    n_im = lb_im
    f_re = (n_re * a_re + n_im * a_im) / den
    f_im = (n_im * a_re - n_re * a_im) / den
    bb_re = f_re[..., None] * b_re - f_im[..., None] * b_im
    bb_im = f_re[..., None] * b_im + f_im[..., None] * b_re

    def powers(br, bi, count):
        rs, is_ = [br], [bi]
        for _ in range(count - 1):
            rs.append(rs[-1] * br - is_[-1] * bi)
            is_.append(rs[-2] * bi + is_[-1] * br)
        return rs, is_

    pr, pi = powers(lb_re, lb_im, S5_BLOCK)
    lam_r = [jnp.ones_like(lb_re)] + pr
    lam_i = [jnp.zeros_like(lb_im)] + pi
    eye = jnp.eye(S5_CHUNK_GROUPS, dtype=f32)[None, None, :, None, None, :, None]
    width = S5_BLOCK * LANES

    def block_diag(w, perm):
        w = w.transpose(perm)
        return (w[..., None, :] * eye).reshape(nq, width, width)

    ar = jnp.stack([lam_r[S5_BLOCK - 1 - i] for i in range(S5_BLOCK)])[..., None]
    ai = jnp.stack([lam_i[S5_BLOCK - 1 - i] for i in range(S5_BLOCK)])[..., None]
    ab = jnp.stack([ar * bb_re - ai * bb_im, ar * bb_im + ai * bb_re], axis=-1)
    ab = ab.reshape(S5_BLOCK, nq, S5_CHUNK_GROUPS, p, c, 2)
    wagg = block_diag(ab, (1, 0, 2, 4, 5, 3))

    ar = jnp.stack(lam_r[1:S5_BLOCK + 1])[:, :, None, :]
    ai = jnp.stack(lam_i[1:S5_BLOCK + 1])[:, :, None, :]
    cl = jnp.stack([c_re * ar - c_im * ai, -(c_re * ai + c_im * ar)], axis=-1)
    cl = cl.reshape(S5_BLOCK, nq, S5_CHUNK_GROUPS, c, p, 2)
    wout = block_diag(cl, (1, 5, 2, 4, 0, 3))

    ar = jnp.stack(lam_r[:S5_BLOCK])[:, :, None, :]
    ai = jnp.stack(lam_i[:S5_BLOCK])[:, :, None, :]
    kj = (jnp.einsum('jgcp,gpd->jgdc', c_re * ar - c_im * ai, bb_re, precision=hp)
          - jnp.einsum('jgcp,gpd->jgdc', c_re * ai + c_im * ar, bb_im, precision=hp))
    none = jnp.zeros_like(kj[0])
    kt = jnp.stack([jnp.stack([kj[i - a] if i >= a else none for i in range(S5_BLOCK)])
                    for a in range(S5_BLOCK)])
    kt = kt.reshape(S5_BLOCK, S5_BLOCK, nq, S5_CHUNK_GROUPS, c, c)
    tmat = block_diag(kt, (2, 0, 3, 4, 1, 5))

    dr, di = powers(lam_r[S5_BLOCK], lam_i[S5_BLOCK], SUBLANES)
    dtab = jnp.stack([jnp.stack(dr), jnp.stack(di)], axis=1)
    dtab = dtab.reshape(SUBLANES, 2, nq, S5_CHUNK_GROUPS * p).transpose(2, 0, 1, 3).reshape(nq, SUBLANES, width)
    return wagg.astype(bf16), tmat.astype(bf16), wout.astype(bf16), dtab


def _conv_kernel(v_ref, halo_ref, w_ref, b_ref, g_ref, beta_ref, o_ref, ext_scr, acc_scr):
    tt = v_ref.shape[0]
    pad = halo_ref.shape[0]
    first = pl.program_id(0) == 0
    ext_scr[0:pad, :] = jnp.where(first, 0.0, halo_ref[...])
    ext_scr[pad:pad + tt, :] = v_ref[...]
    rows, cols = 32, 256
    for c0 in range(0, v_ref.shape[1], cols):
        for r0 in range(0, tt, rows):
            acc = jnp.broadcast_to(b_ref[:, c0:c0 + cols], (rows, cols))
            for k in range(CONV_WIDTH):
                off = pad - (CONV_WIDTH - 1) + k + r0
                acc = acc + w_ref[k:k + 1, c0:c0 + cols] * ext_scr[off:off + rows, c0:c0 + cols]
            acc_scr[r0:r0 + rows, c0:c0 + cols] = acc
    y = _layer_norm(acc_scr[...], g_ref[...], beta_ref[...])
    o_ref[...] = (y * jax.nn.sigmoid(y)).astype(o_ref.dtype)


def _conv_module(v, w_dw, b_dw, g, beta, tt=128, pad=32):
    n, d = v.shape
    w_pad = jnp.zeros((pad, d), f32).at[:CONV_WIDTH].set(w_dw)
    ratio = tt // pad
    row = lambda i: (i, 0)
    fixed = lambda i: (0, 0)
    return pl.pallas_call(
        _conv_kernel,
        grid=(n // tt,),
        in_specs=[pl.BlockSpec((tt, d), row),
                  pl.BlockSpec((pad, d), lambda i: (jnp.maximum(i * ratio - 1, 0), 0)),
                  pl.BlockSpec((pad, d), fixed), pl.BlockSpec((1, d), fixed),
                  pl.BlockSpec((1, d), fixed), pl.BlockSpec((1, d), fixed)],
        out_specs=pl.BlockSpec((tt, d), row),
        out_shape=jax.ShapeDtypeStruct((n, d), bf16),
        scratch_shapes=[pltpu.VMEM((tt + pad, d), f32), pltpu.VMEM((tt, d), f32)],
        compiler_params=_params("parallel"),
        name="conv_module",
    )(v, v, w_pad, b_dw, g, beta)


def _pool_kernel(h_ref, halo_ref, w_ref, scale_ref, g_ref, beta_ref, of_ref, ob_ref, ext_scr):
    tt = h_ref.shape[0]
    pad = halo_ref.shape[0]
    i = pl.program_id(0)
    ext_scr[0:pad, :] = jnp.where(i == 0, 0.0, halo_ref[...])
    ext_scr[pad:pad + tt, :] = h_ref[...]
    t = i * tt + lax.broadcasted_iota(jnp.int32, (tt, 1), 0)
    ys = []
    for k, win in enumerate(POOL_WINDOWS):
        cs = slice(k * POOL_CH, (k + 1) * POOL_CH)
        x = h_ref[:, cs]
        acc = x
        for j in range(1, win):
            acc = acc + ext_scr[pad - j:pad - j + tt, cs]
        cnt = jnp.minimum(t + 1, win).astype(f32)
        ys.append(_dot(acc / cnt - x, w_ref[k]))
    y = jnp.concatenate(ys, axis=1) * scale_ref[...]
    out = _layer_norm(DN_ALPHA * h_ref[...] + y, g_ref[...], beta_ref[...])
    of_ref[...] = out
    ob_ref[...] = out.astype(bf16)


def _pool_layer(h, w_grp, scale, g, beta, tt=256, pad=16):
    n, d = h.shape
    ratio = tt // pad
    row = lambda i: (i, 0)
    fixed = lambda i: (0, 0)
    return pl.pallas_call(
        _pool_kernel,
        grid=(n // tt,),
        in_specs=[pl.BlockSpec((tt, d), row),
                  pl.BlockSpec((pad, d), lambda i: (jnp.maximum(i * ratio - 1, 0), 0)),
                  pl.BlockSpec(w_grp.shape, lambda i: (0, 0, 0)),
                  pl.BlockSpec((1, d), fixed), pl.BlockSpec((1, d), fixed), pl.BlockSpec((1, d), fixed)],
        out_specs=[pl.BlockSpec((tt, d), row), pl.BlockSpec((tt, d), row)],
        out_shape=[jax.ShapeDtypeStruct((n, d), f32), jax.ShapeDtypeStruct((n, d), bf16)],
        scratch_shapes=[pltpu.VMEM((tt + pad, d), f32)],
        compiler_params=_params("parallel"),
        name="pool_layer",
    )(h, h, w_grp, scale, g, beta)


R_E1, R_E2, R_G1, R_G2, R_RANK1, R_RANK2 = range(6)


def _router_kernel(h_ref, w_ref, b_ref, info_ref, cnt_ref):
    tm = h_ref.shape[0]

    @pl.when(pl.program_id(0) == 0)
    def _():
        cnt_ref[...] = jnp.zeros_like(cnt_ref)

    logits = jnp.dot(h_ref[...], w_ref[...], preferred_element_type=f32,
                     precision=lax.Precision.HIGHEST) + b_ref[...]
    lane = lax.broadcasted_iota(jnp.int32, logits.shape, 1)
    real = lane < N_EXPERTS
    e = jnp.exp(logits - jnp.max(logits, axis=-1, keepdims=True))
    probs = e / jnp.sum(e, axis=-1, keepdims=True)

    a = probs
    b = pltpu.roll(probs, 1, 1)
    c = pltpu.roll(probs, 2, 1)
    d = pltpu.roll(probs, 3, 1)
    hi1, lo1 = jnp.maximum(a, b), jnp.minimum(a, b)
    hi2, lo2 = jnp.maximum(c, d), jnp.minimum(c, d)
    score = jnp.maximum(hi1, hi2) + jnp.maximum(jnp.minimum(hi1, hi2), jnp.maximum(lo1, lo2))
    best = None
    g_sel = None
    for grp in range(N_EXPERTS // EXPERTS_PER_GROUP):
        last = grp * EXPERTS_PER_GROUP + EXPERTS_PER_GROUP - 1
        s = jnp.max(jnp.where(lane == last, score, -1.0), axis=-1, keepdims=True)
        if grp == 0:
            best, g_sel = s, jnp.zeros_like(s, dtype=jnp.int32)
        else:
            better = s > best
            best = jnp.where(better, s, best)
            g_sel = jnp.where(better, grp, g_sel)

    in_grp = real & ((lane // EXPERTS_PER_GROUP) == g_sel)
    masked = jnp.where(in_grp, probs, -1.0)
    lane_f = lane.astype(f32)
    p1 = jnp.max(masked, axis=-1, keepdims=True)
    e1 = jnp.min(jnp.where(masked == p1, lane_f, float(LANES)), axis=-1, keepdims=True)
    masked2 = jnp.where(lane_f == e1, -2.0, masked)
    p2 = jnp.max(masked2, axis=-1, keepdims=True)
    e2 = jnp.min(jnp.where(masked2 == p2, lane_f, float(LANES)), axis=-1, keepdims=True)
    tot = p1 + p2

    oh1 = (lane_f == e1).astype(f32)
    oh2 = (lane_f == e2).astype(f32)
    both = oh1 + oh2
    ri = lax.broadcasted_iota(jnp.int32, (tm, tm), 0)
    ci = lax.broadcasted_iota(jnp.int32, (tm, tm), 1)
    tri = (ci < ri).astype(bf16)
    before = cnt_ref[0:1, :] + jnp.dot(tri, both.astype(bf16), preferred_element_type=f32)
    rank1 = jnp.sum(before * oh1, axis=-1, keepdims=True)
    rank2 = jnp.sum(before * oh2, axis=-1, keepdims=True)
    cnt_ref[...] = cnt_ref[...] + jnp.sum(both, axis=0, keepdims=True)

    info = jnp.zeros(logits.shape, f32)
    for slot, val in ((R_E1, e1), (R_E2, e2), (R_G1, p1 / tot), (R_G2, p2 / tot),
                      (R_RANK1, rank1), (R_RANK2, rank2)):
        info = jnp.where(lane == slot, val, info)
    info_ref[...] = info


def _router(h, router_w, router_b, tm=512):
    n, d = h.shape
    w = jnp.zeros((d, LANES), f32).at[:, :N_EXPERTS].set(router_w)
    b = jnp.full((1, LANES), -1e30, f32).at[0, :N_EXPERTS].set(router_b)
    return pl.pallas_call(
        _router_kernel,
        grid=(n // tm,),
        in_specs=[pl.BlockSpec((tm, d), lambda i: (i, 0)), pl.BlockSpec((d, LANES), lambda i: (0, 0)),
                  pl.BlockSpec((1, LANES), lambda i: (0, 0))],
        out_specs=[pl.BlockSpec((tm, LANES), lambda i: (i, 0)), pl.BlockSpec((SUBLANES, LANES), lambda i: (0, 0))],
        out_shape=[jax.ShapeDtypeStruct((n, LANES), f32), jax.ShapeDtypeStruct((SUBLANES, LANES), f32)],
        compiler_params=_params("arbitrary"),
        name="router",
    )(h, w, b)


def _row_copy(src_ref, dst_ref, sem, src_row, dst_row):
    return pltpu.make_async_copy(src_ref.at[pl.ds(src_row, 1)], dst_ref.at[pl.ds(dst_row, 1)], sem)


def _gather_kernel(idx_ref, src_ref, o_ref, sem):
    tm = o_ref.shape[0]
    base = pl.program_id(0) * tm

    def start(j, carry):
        _row_copy(src_ref, o_ref, sem, idx_ref[base + j], j).start()
        return carry

    def wait(j, carry):
        _row_copy(src_ref, o_ref, sem, 0, j).wait()
        return carry

    lax.fori_loop(0, tm, start, 0)
    lax.fori_loop(0, tm, wait, 0)


def _gather_rows(src, idx, tm=GATHER_ROWS):
    m = idx.shape[0]
    d = src.shape[1]
    return pl.pallas_call(
        _gather_kernel,
        grid_spec=pltpu.PrefetchScalarGridSpec(
            num_scalar_prefetch=1, grid=(m // tm,),
            in_specs=[pl.BlockSpec(memory_space=pl.ANY)],
            out_specs=pl.BlockSpec((tm, d), lambda i, idx_ref: (i, 0)),
            scratch_shapes=[pltpu.SemaphoreType.DMA(())]),
        out_shape=jax.ShapeDtypeStruct((m, d), src.dtype),
        compiler_params=_params("arbitrary"),
        name="gather_rows",
    )(idx, src)


def _expert_kernel(blk_e_ref, blk_valid_ref, x_ref, wg_ref, wu_ref, wd_ref, o_ref):
    i = pl.program_id(0)

    @pl.when(blk_valid_ref[i] != 0)
    def _():
        x = x_ref[...].astype(bf16)
        gate = _dot(x, wg_ref[0])
        hid = gate * jax.nn.sigmoid(gate) * _dot(x, wu_ref[0])
        o_ref[...] = _dot(hid, wd_ref[0])

    @pl.when(blk_valid_ref[i] == 0)
    def _():
        o_ref[...] = jnp.zeros_like(o_ref)


def _experts(xb, blk_e, blk_valid, w_gate, w_up, w_down):
    cap, d = xb.shape
    de = w_gate.shape[2]
    row = lambda i, e, v: (i, 0)
    return pl.pallas_call(
        _expert_kernel,
        grid_spec=pltpu.PrefetchScalarGridSpec(
            num_scalar_prefetch=2, grid=(cap // MOE_ROWS,),
            in_specs=[pl.BlockSpec((MOE_ROWS, d), row),
                      pl.BlockSpec((1, d, de), lambda i, e, v: (e[i], 0, 0)),
                      pl.BlockSpec((1, d, de), lambda i, e, v: (e[i], 0, 0)),
                      pl.BlockSpec((1, de, d), lambda i, e, v: (e[i], 0, 0))],
            out_specs=pl.BlockSpec((MOE_ROWS, d), row)),
        out_shape=jax.ShapeDtypeStruct((cap, d), f32),
        compiler_params=_params("arbitrary"),
        name="experts",
    )(blk_e, blk_valid, xb, w_gate, w_up, w_down)


def _moe_ln_kernel(h_ref, y1_ref, y2_ref, info_ref, g_ref, beta_ref, of_ref, ob_ref):
    info = info_ref[...]
    g1 = info[:, R_G1:R_G1 + 1]
    g2 = info[:, R_G2:R_G2 + 1]
    f = g1 * y1_ref[...] + g2 * y2_ref[...]
    out = _layer_norm(DN_ALPHA * h_ref[...] + f, g_ref[...], beta_ref[...])
    of_ref[...] = out
    ob_ref[...] = out.astype(bf16)


def _moe_ln(h, yc, info, g, beta, tm=256):
    n, d = h.shape
    nb = n // tm
    row = lambda i: (i, 0)
    fixed = lambda i: (0, 0)
    return pl.pallas_call(
        _moe_ln_kernel,
        grid=(nb,),
        in_specs=[pl.BlockSpec((tm, d), row), pl.BlockSpec((tm, d), row),
                  pl.BlockSpec((tm, d), lambda i: (i + nb, 0)), pl.BlockSpec((tm, LANES), row),
                  pl.BlockSpec((1, d), fixed), pl.BlockSpec((1, d), fixed)],
        out_specs=[pl.BlockSpec((tm, d), row), pl.BlockSpec((tm, d), row)],
        out_shape=[jax.ShapeDtypeStruct((n, d), f32), jax.ShapeDtypeStruct((n, d), bf16)],
        compiler_params=_params("parallel"),
        name="moe_ln",
    )(h, yc, yc, info, g, beta)


def _moe_layer(hf, router_w, router_b, w_gate, w_up, w_down, g, beta):
    n, d = hf.shape
    info, cnt = _router(hf, router_w, router_b)
    experts = info[:, R_E1:R_E2 + 1].astype(jnp.int32)
    rank = info[:, R_RANK1:R_RANK2 + 1].astype(jnp.int32)
    counts = cnt[0, :N_EXPERTS].astype(jnp.int32)
    pcounts = ((counts + MOE_ROWS - 1) // MOE_ROWS) * MOE_ROWS
    pends = jnp.cumsum(pcounts)
    pstarts = pends - pcounts
    dest = pstarts[experts] + rank
    cap = 2 * n + N_EXPERTS * MOE_ROWS
    tok = jnp.repeat(jnp.arange(n, dtype=jnp.int32), 2)
    row_tok = jnp.zeros((cap,), jnp.int32).at[dest.reshape(-1)].set(tok, unique_indices=True)
    blk_start = jnp.arange(cap // MOE_ROWS, dtype=jnp.int32) * MOE_ROWS
    blk_e = jnp.minimum(jnp.searchsorted(pends, blk_start, side='right'), N_EXPERTS - 1).astype(jnp.int32)
    blk_valid = (blk_start < pends[-1]).astype(jnp.int32)

    xb = _gather_rows(hf, row_tok)
    y = _experts(xb, blk_e, blk_valid, w_gate, w_up, w_down)
    yc = _gather_rows(y, dest.T.reshape(-1))
    return _moe_ln(hf, yc, info, g, beta)


def kernel(x, s5_w_in, s5_b_re, s5_b_im, s5_c_re, s5_c_im, s5_a_re, s5_a_im, s5_log_step, s5_d, s5_w_glu, s5_w_out, cv_w_pw1, cv_b_pw1, cv_w_dw, cv_b_dw, cv_ln_g, cv_ln_b, cv_w_pw2, cv_b_pw2, pl_w, pl_scale, router_w, router_b, moe_w_gate, moe_w_up, moe_w_down, ln_mix_g, ln_mix_b, ln_ffn_g, ln_ffn_b):
    bsz, seq, d = x.shape
    hf = x.reshape(bsz * seq, d)
    hb = hf.astype(bf16)
    row = lambda v: v.reshape(1, -1)
    for i in range(DEPTH):
        mixer, j = i % N_MIXERS, i // N_MIXERS
        g_mix, b_mix = row(ln_mix_g[i]), row(ln_mix_b[i])
        if mixer == 0:
            u = _mm(hb, s5_w_in[j].astype(bf16))
            wagg, tmat, wout, dtab = _s5_derive(s5_b_re[j], s5_b_im[j], s5_c_re[j], s5_c_im[j],
                                                s5_a_re[j], s5_a_im[j], s5_log_step[j])
            z = _s5_core(u, wagg, tmat, wout, dtab, row(s5_d[j]))
            v = _mm_glu(z, s5_w_glu[j].astype(bf16), jnp.zeros((1, 2 * d), f32), bf16)
            hf, hb = _mm_res_ln(v, s5_w_out[j].astype(bf16), jnp.zeros((1, d), f32), hf, g_mix, b_mix)
        elif mixer == 1:
            v = _mm_glu(hb, cv_w_pw1[j].astype(bf16), row(cv_b_pw1[j]), f32)
            cv = _conv_module(v, cv_w_dw[j], row(cv_b_dw[j]), row(cv_ln_g[j]), row(cv_ln_b[j]))
            hf, hb = _mm_res_ln(cv, cv_w_pw2[j].astype(bf16), row(cv_b_pw2[j]), hf, g_mix, b_mix)
        else:
            hf, hb = _pool_layer(hf, pl_w[j].astype(bf16), row(pl_scale[j]), g_mix, b_mix)
        hf, hb = _moe_layer(hf, router_w, router_b, moe_w_gate[i].astype(bf16), moe_w_up[i].astype(bf16),
                            moe_w_down[i].astype(bf16), row(ln_ffn_g[i]), row(ln_ffn_b[i]))
    return hf.reshape(bsz, seq, d)
```

```python
import functools
import math

import jax
import jax.numpy as jnp
from jax import lax
from jax.experimental import pallas as pl
from jax.experimental.pallas import tpu as pltpu

f32 = jnp.float32
bf16 = jnp.bfloat16

D_MODEL = 2048
DEPTH = 4
N_MIXERS = 3
S5_GROUP = 16
S5_STATE = 64
CONV_WIDTH = 31
POOL_WINDOWS = (2, 4, 8, 16)
POOL_CH = D_MODEL // len(POOL_WINDOWS)
N_EXPERTS = 16
EXPERTS_PER_GROUP = 4
D_EXPERT = D_MODEL // 2
DN_ALPHA = (2 * DEPTH) ** 0.25
LN_EPS = 1e-5

LANES = 128
SUBLANES = 8
VMEM_LIMIT = 56 * 1024 * 1024
S5_BLOCK = 8
S5_CHUNK_GROUPS = LANES // S5_GROUP
MOE_ROWS = 256
GATHER_ROWS = 128
PACK_ROWS = D_MODEL // 2 // LANES
WIDE_ROWS = D_MODEL // LANES


def _params(*sem):
    return pltpu.CompilerParams(dimension_semantics=sem, vmem_limit_bytes=VMEM_LIMIT)


def _layer_norm(r, g, b):
    mu = jnp.mean(r, axis=-1, keepdims=True)
    xc = r - mu
    var = jnp.mean(xc * xc, axis=-1, keepdims=True)
    return xc * lax.rsqrt(var + LN_EPS) * g + b


def _dot(a, b):
    return jnp.dot(a.astype(bf16), b.astype(bf16), preferred_element_type=f32)


def _store_packed(out, hp_ref):
    tm, d = out.shape
    for c in range(PACK_ROWS):
        lo = out[:, c * LANES:(c + 1) * LANES].astype(bf16).astype(f32)
        hi = out[:, d // 2 + c * LANES:d // 2 + (c + 1) * LANES].astype(bf16).astype(f32)
        word = (lax.bitcast_convert_type(lo, jnp.uint32) >> 16) | lax.bitcast_convert_type(hi, jnp.uint32)
        hp_ref[pl.ds(c, tm, stride=PACK_ROWS), :] = word


def _load_packed(hp_ref, x_scr):
    tm, d = x_scr.shape
    for c in range(PACK_ROWS):
        word = hp_ref[pl.ds(c, tm, stride=PACK_ROWS), :]
        lo = lax.bitcast_convert_type(word << 16, f32)
        hi = lax.bitcast_convert_type(word & jnp.uint32(0xFFFF0000), f32)
        x_scr[:, c * LANES:(c + 1) * LANES] = lo.astype(bf16)
        x_scr[:, d // 2 + c * LANES:d // 2 + (c + 1) * LANES] = hi.astype(bf16)


def _mm_kernel(x_ref, w_ref, o_ref):
    o_ref[...] = _dot(x_ref[...], w_ref[...]).astype(o_ref.dtype)


def _mm(x, w, tm=512, tn=1024):
    m, k = x.shape
    n = w.shape[1]
    return pl.pallas_call(
        _mm_kernel,
        grid=(n // tn, m // tm),
        in_specs=[pl.BlockSpec((tm, k), lambda j, i: (i, 0)),
                  pl.BlockSpec((k, tn), lambda j, i: (0, j))],
        out_specs=pl.BlockSpec((tm, tn), lambda j, i: (i, j)),
        out_shape=jax.ShapeDtypeStruct((m, n), f32),
        compiler_params=_params("parallel", "parallel"),
        name="mm",
    )(x, w)


def _mm_glu_kernel(x_ref, wa_ref, wg_ref, ba_ref, bg_ref, o_ref):
    x = x_ref[...].astype(bf16)
    a = _dot(x, wa_ref[...]) + ba_ref[...]
    g = _dot(x, wg_ref[...]) + bg_ref[...]
    o_ref[...] = (a * jax.nn.sigmoid(g)).astype(o_ref.dtype)


def _mm_glu(x, w, b, out_dtype, tm=512, tn=512):
    m, k = x.shape
    n = w.shape[1] // 2
    nb = n // tn
    return pl.pallas_call(
        _mm_glu_kernel,
        grid=(nb, m // tm),
        in_specs=[pl.BlockSpec((tm, k), lambda j, i: (i, 0)),
                  pl.BlockSpec((k, tn), lambda j, i: (0, j)),
                  pl.BlockSpec((k, tn), lambda j, i: (0, j + nb)),
                  pl.BlockSpec((1, tn), lambda j, i: (0, j)),
                  pl.BlockSpec((1, tn), lambda j, i: (0, j + nb))],
        out_specs=pl.BlockSpec((tm, tn), lambda j, i: (i, j)),
        out_shape=jax.ShapeDtypeStruct((m, n), out_dtype),
        compiler_params=_params("parallel", "parallel"),
        name="mm_glu",
    )(x, w, w, b, b)


def _mm_res_ln_kernel(x_ref, w_ref, b_ref, h_ref, g_ref, beta_ref, of_ref, ob_ref, hp_ref):
    y = _dot(x_ref[...], w_ref[...]) + b_ref[...]
    out = _layer_norm(DN_ALPHA * h_ref[...] + y, g_ref[...], beta_ref[...])
    of_ref[...] = out
    ob_ref[...] = out.astype(bf16)
    _store_packed(out, hp_ref)


def _mm_res_ln(x, w, b, h, g, beta, tm=256):
    m, k = x.shape
    d = w.shape[1]
    row = lambda i: (i, 0)
    fixed = lambda i: (0, 0)
    return pl.pallas_call(
        _mm_res_ln_kernel,
        grid=(m // tm,),
        in_specs=[pl.BlockSpec((tm, k), row), pl.BlockSpec((k, d), fixed), pl.BlockSpec((1, d), fixed),
                  pl.BlockSpec((tm, d), row), pl.BlockSpec((1, d), fixed), pl.BlockSpec((1, d), fixed)],
        out_specs=[pl.BlockSpec((tm, d), row), pl.BlockSpec((tm, d), row),
                   pl.BlockSpec((tm * PACK_ROWS, LANES), row)],
        out_shape=[jax.ShapeDtypeStruct((m, d), f32), jax.ShapeDtypeStruct((m, d), bf16),
                   jax.ShapeDtypeStruct((m * PACK_ROWS, LANES), jnp.uint32)],
        compiler_params=_params("parallel"),
        name="mm_res_ln",
    )(x, w, b, h, g, beta)


def _cmul_add(xr, xi, ar, ai, sr, si):
    return xr + ar * sr - ai * si, xi + ar * si + ai * sr


def _expand_block_diag(compact_ref, rep_ref, w_scr, row_shift, col_shift):
    width = w_scr.shape[0]
    step = 2 * LANES
    for c0 in range(0, width, step):
        w = jnp.dot(compact_ref[0], rep_ref[:, c0:c0 + step], preferred_element_type=f32)
        row_g = (lax.broadcasted_iota(jnp.int32, w.shape, 0) >> row_shift) & (S5_CHUNK_GROUPS - 1)
        col_g = ((lax.broadcasted_iota(jnp.int32, w.shape, 1) + c0) >> col_shift) & (S5_CHUNK_GROUPS - 1)
        w_scr[:, c0:c0 + step] = jnp.where(row_g == col_g, w, 0.0).astype(bf16)


def _s5_kernel(u_ref, agg_ref, toe_ref, proj_ref, rep_state_ref, rep_chan_ref, dtab_ref, d_ref, z_ref,
               xb_scr, v_scr, wagg_scr, t_scr, wout_scr):
    m_rows = v_scr.shape[0]
    half = v_scr.shape[1] // 2
    ncol = half // LANES
    chan_shift = S5_GROUP.bit_length() - 1
    state_shift = S5_STATE.bit_length() - 1
    _expand_block_diag(agg_ref, rep_state_ref, wagg_scr, chan_shift, state_shift)
    _expand_block_diag(toe_ref, rep_chan_ref, t_scr, chan_shift, chan_shift)
    _expand_block_diag(proj_ref, rep_chan_ref, wout_scr, state_shift, chan_shift)
    for i in range(S5_BLOCK):
        xb_scr[:, i * LANES:(i + 1) * LANES] = u_ref[pl.ds(i, m_rows, stride=S5_BLOCK), :].astype(bf16)
    xb = xb_scr[...]
    v_scr[...] = jnp.dot(xb, wagg_scr[...], preferred_element_type=f32)

    tab = dtab_ref[0]
    sub = lax.broadcasted_iota(jnp.int32, (SUBLANES, LANES), 0)

    def col(j, part):
        lo = part * half + j * LANES
        return slice(lo, lo + LANES)

    def bcast(row):
        return jnp.broadcast_to(row, (SUBLANES, LANES))

    def body(r, carry):
        r0 = pl.multiple_of(r * SUBLANES, SUBLANES)
        new = []
        for j in range(ncol):
            cr, ci = carry[2 * j], carry[2 * j + 1]
            xr = v_scr[pl.ds(r0, SUBLANES), col(j, 0)]
            xi = v_scr[pl.ds(r0, SUBLANES), col(j, 1)]
            for shift in (1, 2, 4):
                ar = bcast(tab[shift - 1:shift, col(j, 0)])
                ai = bcast(tab[shift - 1:shift, col(j, 1)])
                sr = jnp.where(sub >= shift, pltpu.roll(xr, shift, 0), 0.0)
                si = jnp.where(sub >= shift, pltpu.roll(xi, shift, 0), 0.0)
                xr, xi = _cmul_add(xr, xi, ar, ai, sr, si)
            xr, xi = _cmul_add(xr, xi, tab[:, col(j, 0)], tab[:, col(j, 1)], cr, ci)
            v_scr[pl.ds(r0, SUBLANES), col(j, 0)] = jnp.where(sub >= 1, pltpu.roll(xr, 1, 0), cr)
            v_scr[pl.ds(r0, SUBLANES), col(j, 1)] = jnp.where(sub >= 1, pltpu.roll(xi, 1, 0), ci)
            new.append(bcast(xr[SUBLANES - 1:SUBLANES, :]))
            new.append(bcast(xi[SUBLANES - 1:SUBLANES, :]))
        return tuple(new)

    zero = jnp.zeros((SUBLANES, LANES), f32)
    lax.fori_loop(0, m_rows // SUBLANES, body, (zero,) * (2 * ncol))

    sp = v_scr[...].astype(bf16)
    for i0 in range(0, S5_BLOCK, 2):
        cs = slice(i0 * LANES, (i0 + 2) * LANES)
        y = (jnp.dot(xb, t_scr[:, cs], preferred_element_type=f32)
             + jnp.dot(sp, wout_scr[:, cs], preferred_element_type=f32))
        for i in (i0, i0 + 1):
            yi = y[:, (i - i0) * LANES:(i - i0 + 1) * LANES] + d_ref[...] * u_ref[pl.ds(i, m_rows, stride=S5_BLOCK), :]
            z_ref[pl.ds(i, m_rows, stride=S5_BLOCK), :] = 0.5 * yi * (1.0 + lax.erf(yi * (1.0 / math.sqrt(2.0))))


def _s5_core(u, agg, toe, proj, dtab, d_skip):
    n, d = u.shape
    nq = d // LANES
    m_rows = n // S5_BLOCK
    width = agg.shape[1]
    col = jnp.arange(width)[None, :]
    lane = jnp.arange(LANES)[:, None]
    half = width // 2
    rep_state = ((col // half == lane // S5_STATE) & (col % S5_STATE == lane % S5_STATE)).astype(bf16)
    rep_chan = ((col // LANES == lane // S5_GROUP) & (col % S5_GROUP == lane % S5_GROUP)).astype(bf16)
    wspec = pl.BlockSpec((1, width, LANES), lambda q: (q, 0, 0))
    rspec = pl.BlockSpec((LANES, width), lambda q: (0, 0))
    return pl.pallas_call(
        _s5_kernel,
        grid=(nq,),
        in_specs=[pl.BlockSpec((n, LANES), lambda q: (0, q)), wspec, wspec, wspec, rspec, rspec,
                  pl.BlockSpec((1, SUBLANES, width), lambda q: (q, 0, 0)),
                  pl.BlockSpec((1, LANES), lambda q: (0, q))],
        out_specs=pl.BlockSpec((n, LANES), lambda q: (0, q)),
        out_shape=jax.ShapeDtypeStruct((n, d), f32),
        scratch_shapes=[pltpu.VMEM((m_rows, width), bf16), pltpu.VMEM((m_rows, width), f32),
                        pltpu.VMEM((width, width), bf16), pltpu.VMEM((width, width), bf16),
                        pltpu.VMEM((width, width), bf16)],
        compiler_params=_params("parallel"),
        name="s5_core",
    )(u, agg, toe, proj, rep_state, rep_chan, dtab, d_skip)


def _s5_derive(b_re, b_im, c_re, c_im, a_re, a_im, log_step):
    highest = lax.Precision.HIGHEST
    g, p, c = b_re.shape
    nq = g // S5_CHUNK_GROUPS
    dt = jnp.exp(log_step)[:, None]
    mag = jnp.exp(a_re * dt)
    lb_re = mag * jnp.cos(a_im * dt)
    lb_im = mag * jnp.sin(a_im * dt)
    den = a_re * a_re + a_im * a_im
    n_re = lb_re - 1.0
    n_im = lb_im
    f_re = (n_re * a_re + n_im * a_im) / den
    f_im = (n_im * a_re - n_re * a_im) / den
    bb_re = f_re[..., None] * b_re - f_im[..., None] * b_im
    bb_im = f_re[..., None] * b_im + f_im[..., None] * b_re

    def powers(br, bi, count):
        rs, is_ = [br], [bi]
        for _ in range(count - 1):
            rs.append(rs[-1] * br - is_[-1] * bi)
            is_.append(rs[-2] * bi + is_[-1] * br)
        return rs, is_

    pr, pi = powers(lb_re, lb_im, S5_BLOCK)
    lam_r = [jnp.ones_like(lb_re)] + pr
    lam_i = [jnp.zeros_like(lb_im)] + pi
    width = S5_BLOCK * LANES

    def compact(w, perm):
        return w.transpose(perm).reshape(nq, width, LANES).astype(bf16)

    ar = jnp.stack([lam_r[S5_BLOCK - 1 - i] for i in range(S5_BLOCK)])[..., None]
    ai = jnp.stack([lam_i[S5_BLOCK - 1 - i] for i in range(S5_BLOCK)])[..., None]
    ab = jnp.stack([ar * bb_re - ai * bb_im, ar * bb_im + ai * bb_re], axis=-1)
    ab = ab.reshape(S5_BLOCK, nq, S5_CHUNK_GROUPS, p, c, 2)
    wagg = compact(ab, (1, 0, 2, 4, 5, 3))

    ar = jnp.stack(lam_r[1:S5_BLOCK + 1])[:, :, None, :]
    ai = jnp.stack(lam_i[1:S5_BLOCK + 1])[:, :, None, :]
    cl = jnp.stack([c_re * ar - c_im * ai, -(c_re * ai + c_im * ar)], axis=-1)
    cl = cl.reshape(S5_BLOCK, nq, S5_CHUNK_GROUPS, c, p, 2)
    wout = compact(cl, (1, 5, 2, 4, 0, 3))

    ar = jnp.stack(lam_r[:S5_BLOCK])[:, :, None, :]
    ai = jnp.stack(lam_i[:S5_BLOCK])[:, :, None, :]
    kj = (jnp.einsum('jgcp,gpd->jgdc', c_re * ar - c_im * ai, bb_re, precision=highest)
          - jnp.einsum('jgcp,gpd->jgdc', c_re * ai + c_im * ar, bb_im, precision=highest))
    none = jnp.zeros_like(kj[0])
    kt = jnp.stack([jnp.stack([kj[i - a] if i >= a else none for i in range(S5_BLOCK)])
                    for a in range(S5_BLOCK)])
    kt = kt.reshape(S5_BLOCK, S5_BLOCK, nq, S5_CHUNK_GROUPS, c, c)
    tmat = compact(kt, (2, 0, 3, 4, 1, 5))

    dr, di = powers(lam_r[S5_BLOCK], lam_i[S5_BLOCK], SUBLANES)
    dtab = jnp.stack([jnp.stack(dr), jnp.stack(di)], axis=1)
    dtab = dtab.reshape(SUBLANES, 2, nq, S5_CHUNK_GROUPS * p).transpose(2, 0, 1, 3).reshape(nq, SUBLANES, width)
    return wagg, tmat, wout, dtab


def _conv_kernel(v_ref, halo_ref, w_ref, b_ref, g_ref, beta_ref, o_ref, ext_scr, acc_scr):
    tt = v_ref.shape[0]
    pad = halo_ref.shape[0]
    first = pl.program_id(0) == 0
    ext_scr[0:pad, :] = jnp.where(first, 0.0, halo_ref[...])
    ext_scr[pad:pad + tt, :] = v_ref[...]
    rows, cols = 32, 256
    for c0 in range(0, v_ref.shape[1], cols):
        for r0 in range(0, tt, rows):
            acc = jnp.broadcast_to(b_ref[:, c0:c0 + cols], (rows, cols))
            for k in range(CONV_WIDTH):
                off = pad - (CONV_WIDTH - 1) + k + r0
                acc = acc + w_ref[k:k + 1, c0:c0 + cols] * ext_scr[off:off + rows, c0:c0 + cols]
            acc_scr[r0:r0 + rows, c0:c0 + cols] = acc
    y = _layer_norm(acc_scr[...], g_ref[...], beta_ref[...])
    o_ref[...] = (y * jax.nn.sigmoid(y)).astype(o_ref.dtype)


def _conv_module(v, w_dw, b_dw, g, beta, tt=128, pad=32):
    n, d = v.shape
    w_pad = jnp.zeros((pad, d), f32).at[:CONV_WIDTH].set(w_dw)
    ratio = tt // pad
    row = lambda i: (i, 0)
    fixed = lambda i: (0, 0)
    return pl.pallas_call(
        _conv_kernel,
        grid=(n // tt,),
        in_specs=[pl.BlockSpec((tt, d), row),
                  pl.BlockSpec((pad, d), lambda i: (jnp.maximum(i * ratio - 1, 0), 0)),
                  pl.BlockSpec((pad, d), fixed), pl.BlockSpec((1, d), fixed),
                  pl.BlockSpec((1, d), fixed), pl.BlockSpec((1, d), fixed)],
        out_specs=pl.BlockSpec((tt, d), row),
        out_shape=jax.ShapeDtypeStruct((n, d), bf16),
        scratch_shapes=[pltpu.VMEM((tt + pad, d), f32), pltpu.VMEM((tt, d), f32)],
        compiler_params=_params("parallel"),
        name="conv_module",
    )(v, v, w_pad, b_dw, g, beta)


def _pool_kernel(h_ref, halo_ref, w_ref, scale_ref, g_ref, beta_ref, of_ref, ob_ref, hp_ref, ext_scr):
    tt = h_ref.shape[0]
    pad = halo_ref.shape[0]
    i = pl.program_id(0)
    ext_scr[0:pad, :] = jnp.where(i == 0, 0.0, halo_ref[...])
    ext_scr[pad:pad + tt, :] = h_ref[...]
    t = i * tt + lax.broadcasted_iota(jnp.int32, (tt, 1), 0)
    ys = []
    for k, win in enumerate(POOL_WINDOWS):
        cs = slice(k * POOL_CH, (k + 1) * POOL_CH)
        x = h_ref[:, cs]
        acc = x
        for j in range(1, win):
            acc = acc + ext_scr[pad - j:pad - j + tt, cs]
        cnt = jnp.minimum(t + 1, win).astype(f32)
        ys.append(_dot(acc / cnt - x, w_ref[k]))
    y = jnp.concatenate(ys, axis=1) * scale_ref[...]
    out = _layer_norm(DN_ALPHA * h_ref[...] + y, g_ref[...], beta_ref[...])
    of_ref[...] = out
    ob_ref[...] = out.astype(bf16)
    _store_packed(out, hp_ref)


def _pool_layer(h, w_grp, scale, g, beta, tt=256, pad=16):
    n, d = h.shape
    ratio = tt // pad
    row = lambda i: (i, 0)
    fixed = lambda i: (0, 0)
    return pl.pallas_call(
        _pool_kernel,
        grid=(n // tt,),
        in_specs=[pl.BlockSpec((tt, d), row),
                  pl.BlockSpec((pad, d), lambda i: (jnp.maximum(i * ratio - 1, 0), 0)),
                  pl.BlockSpec(w_grp.shape, lambda i: (0, 0, 0)),
                  pl.BlockSpec((1, d), fixed), pl.BlockSpec((1, d), fixed), pl.BlockSpec((1, d), fixed)],
        out_specs=[pl.BlockSpec((tt, d), row), pl.BlockSpec((tt, d), row),
                   pl.BlockSpec((tt * PACK_ROWS, LANES), row)],
        out_shape=[jax.ShapeDtypeStruct((n, d), f32), jax.ShapeDtypeStruct((n, d), bf16),
                   jax.ShapeDtypeStruct((n * PACK_ROWS, LANES), jnp.uint32)],
        scratch_shapes=[pltpu.VMEM((tt + pad, d), f32)],
        compiler_params=_params("parallel"),
        name="pool_layer",
    )(h, h, w_grp, scale, g, beta)


R_E1, R_E2, R_G1, R_G2, R_RANK1, R_RANK2 = range(6)


def _router_kernel(h_ref, w_ref, b_ref, info_ref, cnt_ref):
    tm = h_ref.shape[0]

    @pl.when(pl.program_id(0) == 0)
    def _():
        cnt_ref[...] = jnp.zeros_like(cnt_ref)

    logits = jnp.dot(h_ref[...], w_ref[...], preferred_element_type=f32,
                     precision=lax.Precision.HIGHEST) + b_ref[...]
    lane = lax.broadcasted_iota(jnp.int32, logits.shape, 1)
    real = lane < N_EXPERTS
    e = jnp.exp(logits - jnp.max(logits, axis=-1, keepdims=True))
    probs = e / jnp.sum(e, axis=-1, keepdims=True)

    a = probs
    b = pltpu.roll(probs, 1, 1)
    c = pltpu.roll(probs, 2, 1)
    d = pltpu.roll(probs, 3, 1)
    hi1, lo1 = jnp.maximum(a, b), jnp.minimum(a, b)
    hi2, lo2 = jnp.maximum(c, d), jnp.minimum(c, d)
    score = jnp.maximum(hi1, hi2) + jnp.maximum(jnp.minimum(hi1, hi2), jnp.maximum(lo1, lo2))
    best = None
    g_sel = None
    for grp in range(N_EXPERTS // EXPERTS_PER_GROUP):
        last = grp * EXPERTS_PER_GROUP + EXPERTS_PER_GROUP - 1
        s = jnp.max(jnp.where(lane == last, score, -1.0), axis=-1, keepdims=True)
        if grp == 0:
            best, g_sel = s, jnp.zeros_like(s, dtype=jnp.int32)
        else:
            better = s > best
            best = jnp.where(better, s, best)
            g_sel = jnp.where(better, grp, g_sel)

    in_grp = real & ((lane // EXPERTS_PER_GROUP) == g_sel)
    masked = jnp.where(in_grp, probs, -1.0)
    lane_f = lane.astype(f32)
    p1 = jnp.max(masked, axis=-1, keepdims=True)
    e1 = jnp.min(jnp.where(masked == p1, lane_f, float(LANES)), axis=-1, keepdims=True)
    masked2 = jnp.where(lane_f == e1, -2.0, masked)
    p2 = jnp.max(masked2, axis=-1, keepdims=True)
    e2 = jnp.min(jnp.where(masked2 == p2, lane_f, float(LANES)), axis=-1, keepdims=True)
    tot = p1 + p2

    oh1 = (lane_f == e1).astype(f32)
    oh2 = (lane_f == e2).astype(f32)
    both = oh1 + oh2
    ri = lax.broadcasted_iota(jnp.int32, (tm, tm), 0)
    ci = lax.broadcasted_iota(jnp.int32, (tm, tm), 1)
    tri = (ci < ri).astype(bf16)
    before = cnt_ref[0:1, :] + jnp.dot(tri, both.astype(bf16), preferred_element_type=f32)
    rank1 = jnp.sum(before * oh1, axis=-1, keepdims=True)
    rank2 = jnp.sum(before * oh2, axis=-1, keepdims=True)
    cnt_ref[...] = cnt_ref[...] + jnp.sum(both, axis=0, keepdims=True)

    info = jnp.zeros(logits.shape, f32)
    for slot, val in ((R_E1, e1), (R_E2, e2), (R_G1, p1 / tot), (R_G2, p2 / tot),
                      (R_RANK1, rank1), (R_RANK2, rank2)):
        info = jnp.where(lane == slot, val, info)
    info_ref[...] = info


def _router(h, router_w, router_b, tm=512):
    n, d = h.shape
    w = jnp.zeros((d, LANES), f32).at[:, :N_EXPERTS].set(router_w)
    b = jnp.full((1, LANES), -1e30, f32).at[0, :N_EXPERTS].set(router_b)
    return pl.pallas_call(
        _router_kernel,
        grid=(n // tm,),
        in_specs=[pl.BlockSpec((tm, d), lambda i: (i, 0)), pl.BlockSpec((d, LANES), lambda i: (0, 0)),
                  pl.BlockSpec((1, LANES), lambda i: (0, 0))],
        out_specs=[pl.BlockSpec((tm, LANES), lambda i: (i, 0)), pl.BlockSpec((SUBLANES, LANES), lambda i: (0, 0))],
        out_shape=[jax.ShapeDtypeStruct((n, LANES), f32), jax.ShapeDtypeStruct((SUBLANES, LANES), f32)],
        compiler_params=_params("arbitrary"),
        name="router",
    )(h, w, b)


def _token_copy(src_ref, dst_ref, sem, src_tok, dst_tok, rows):
    src = pl.multiple_of(src_tok * rows, SUBLANES)
    dst = pl.multiple_of(dst_tok * rows, SUBLANES)
    return pltpu.make_async_copy(src_ref.at[pl.ds(src, rows)], dst_ref.at[pl.ds(dst, rows)], sem)


def _gather_kernel(idx_ref, src_ref, o_ref, sem, *, rows):
    tm = o_ref.shape[0] // rows
    base = pl.program_id(0) * tm

    def start(j, carry):
        _token_copy(src_ref, o_ref, sem, idx_ref[base + j], j, rows).start()
        return carry

    def wait(j, carry):
        _token_copy(src_ref, o_ref, sem, 0, j, rows).wait()
        return carry

    lax.fori_loop(0, tm, start, 0)
    lax.fori_loop(0, tm, wait, 0)


def _gather_tokens(src, idx, rows, tm=GATHER_ROWS):
    m = idx.shape[0]
    return pl.pallas_call(
        functools.partial(_gather_kernel, rows=rows),
        grid_spec=pltpu.PrefetchScalarGridSpec(
            num_scalar_prefetch=1, grid=(m // tm,),
            in_specs=[pl.BlockSpec(memory_space=pl.ANY)],
            out_specs=pl.BlockSpec((tm * rows, LANES), lambda i, idx_ref: (i, 0)),
            scratch_shapes=[pltpu.SemaphoreType.DMA(())]),
        out_shape=jax.ShapeDtypeStruct((m * rows, LANES), src.dtype),
        compiler_params=_params("arbitrary"),
        name="gather_tokens",
    )(idx, src)


def _expert_kernel(blk_e_ref, blk_valid_ref, x_ref, wg_ref, wu_ref, wd_ref, o_ref, x_scr):
    i = pl.program_id(0)
    tm = x_scr.shape[0]

    @pl.when(blk_valid_ref[i] != 0)
    def _():
        _load_packed(x_ref, x_scr)
        x = x_scr[...]
        gate = _dot(x, wg_ref[0])
        hid = gate * jax.nn.sigmoid(gate) * _dot(x, wu_ref[0])
        y = _dot(hid, wd_ref[0])
        for c in range(WIDE_ROWS):
            o_ref[pl.ds(c, tm, stride=WIDE_ROWS), :] = y[:, c * LANES:(c + 1) * LANES]

    @pl.when(blk_valid_ref[i] == 0)
    def _():
        o_ref[...] = jnp.zeros_like(o_ref)


def _experts(xp, blk_e, blk_valid, w_gate, w_up, w_down):
    cap = xp.shape[0] // PACK_ROWS
    _, d, de = w_gate.shape
    row = lambda i, e, v: (i, 0)
    return pl.pallas_call(
        _expert_kernel,
        grid_spec=pltpu.PrefetchScalarGridSpec(
            num_scalar_prefetch=2, grid=(cap // MOE_ROWS,),
            in_specs=[pl.BlockSpec((MOE_ROWS * PACK_ROWS, LANES), row),
                      pl.BlockSpec((1, d, de), lambda i, e, v: (e[i], 0, 0)),
                      pl.BlockSpec((1, d, de), lambda i, e, v: (e[i], 0, 0)),
                      pl.BlockSpec((1, de, d), lambda i, e, v: (e[i], 0, 0))],
            out_specs=pl.BlockSpec((MOE_ROWS * WIDE_ROWS, LANES), row),
            scratch_shapes=[pltpu.VMEM((MOE_ROWS, d), bf16)]),
        out_shape=jax.ShapeDtypeStruct((cap * WIDE_ROWS, LANES), f32),
        compiler_params=_params("arbitrary"),
        name="experts",
    )(blk_e, blk_valid, xp, w_gate, w_up, w_down)


def _load_wide(y_ref, tm):
    return jnp.concatenate([y_ref[pl.ds(c, tm, stride=WIDE_ROWS), :] for c in range(WIDE_ROWS)], axis=1)


def _moe_ln_kernel(h_ref, y1_ref, y2_ref, info_ref, g_ref, beta_ref, of_ref, ob_ref):
    tm = h_ref.shape[0]
    info = info_ref[...]
    g1 = info[:, R_G1:R_G1 + 1]
    g2 = info[:, R_G2:R_G2 + 1]
    f = g1 * _load_wide(y1_ref, tm) + g2 * _load_wide(y2_ref, tm)
    out = _layer_norm(DN_ALPHA * h_ref[...] + f, g_ref[...], beta_ref[...])
    of_ref[...] = out
    ob_ref[...] = out.astype(bf16)


def _moe_ln(h, yc, info, g, beta, tm=256):
    n, d = h.shape
    nb = n // tm
    row = lambda i: (i, 0)
    fixed = lambda i: (0, 0)
    return pl.pallas_call(
        _moe_ln_kernel,
        grid=(nb,),
        in_specs=[pl.BlockSpec((tm, d), row), pl.BlockSpec((tm * WIDE_ROWS, LANES), row),
                  pl.BlockSpec((tm * WIDE_ROWS, LANES), lambda i: (i + nb, 0)), pl.BlockSpec((tm, LANES), row),
                  pl.BlockSpec((1, d), fixed), pl.BlockSpec((1, d), fixed)],
        out_specs=[pl.BlockSpec((tm, d), row), pl.BlockSpec((tm, d), row)],
        out_shape=[jax.ShapeDtypeStruct((n, d), f32), jax.ShapeDtypeStruct((n, d), bf16)],
        compiler_params=_params("parallel"),
        name="moe_ln",
    )(h, yc, yc, info, g, beta)


def _moe_layer(hf, hp, router_w, router_b, w_gate, w_up, w_down, g, beta):
    n, d = hf.shape
    info, cnt = _router(hf, router_w, router_b)
    experts = info[:, R_E1:R_E2 + 1].astype(jnp.int32)
    rank = info[:, R_RANK1:R_RANK2 + 1].astype(jnp.int32)
    counts = cnt[0, :N_EXPERTS].astype(jnp.int32)
    pcounts = ((counts + MOE_ROWS - 1) // MOE_ROWS) * MOE_ROWS
    pends = jnp.cumsum(pcounts)
    pstarts = pends - pcounts
    dest = pstarts[experts] + rank
    cap = 2 * n + N_EXPERTS * MOE_ROWS
    tok = jnp.repeat(jnp.arange(n, dtype=jnp.int32), 2)
    row_tok = (jnp.arange(cap, dtype=jnp.int32) % n).at[dest.reshape(-1)].set(tok, unique_indices=True)
    blk_start = jnp.arange(cap // MOE_ROWS, dtype=jnp.int32) * MOE_ROWS
    blk_e = jnp.sum((blk_start[:, None] >= pends[None, :]).astype(jnp.int32), axis=1)
    blk_e = jnp.minimum(blk_e, N_EXPERTS - 1)
    blk_valid = (blk_start < pends[-1]).astype(jnp.int32)

    xp = _gather_tokens(hp, row_tok, PACK_ROWS)
    y = _experts(xp, blk_e, blk_valid, w_gate, w_up, w_down)
    yc = _gather_tokens(y, dest.T.reshape(-1), WIDE_ROWS)
    return _moe_ln(hf, yc, info, g, beta)


def kernel(x, s5_w_in, s5_b_re, s5_b_im, s5_c_re, s5_c_im, s5_a_re, s5_a_im, s5_log_step, s5_d, s5_w_glu, s5_w_out, cv_w_pw1, cv_b_pw1, cv_w_dw, cv_b_dw, cv_ln_g, cv_ln_b, cv_w_pw2, cv_b_pw2, pl_w, pl_scale, router_w, router_b, moe_w_gate, moe_w_up, moe_w_down, ln_mix_g, ln_mix_b, ln_ffn_g, ln_ffn_b):
    bsz, seq, d = x.shape
    hf = x.reshape(bsz * seq, d)
    hb = hf.astype(bf16)
    row = lambda v: v.reshape(1, -1)
    for i in range(DEPTH):
        mixer, j = i % N_MIXERS, i // N_MIXERS
        g_mix, b_mix = row(ln_mix_g[i]), row(ln_mix_b[i])
        if mixer == 0:
            u = _mm(hb, s5_w_in[j].astype(bf16))
            wagg, tmat, wout, dtab = _s5_derive(s5_b_re[j], s5_b_im[j], s5_c_re[j], s5_c_im[j],
                                                s5_a_re[j], s5_a_im[j], s5_log_step[j])
            z = _s5_core(u, wagg, tmat, wout, dtab, row(s5_d[j]))
            v = _mm_glu(z, s5_w_glu[j].astype(bf16), jnp.zeros((1, 2 * d), f32), bf16)
            hf, hb, hp = _mm_res_ln(v, s5_w_out[j].astype(bf16), jnp.zeros((1, d), f32), hf, g_mix, b_mix)
        elif mixer == 1:
            v = _mm_glu(hb, cv_w_pw1[j].astype(bf16), row(cv_b_pw1[j]), f32)
            cv = _conv_module(v, cv_w_dw[j], row(cv_b_dw[j]), row(cv_ln_g[j]), row(cv_ln_b[j]))
            hf, hb, hp = _mm_res_ln(cv, cv_w_pw2[j].astype(bf16), row(cv_b_pw2[j]), hf, g_mix, b_mix)
        else:
            hf, hb, hp = _pool_layer(hf, pl_w[j].astype(bf16), row(pl_scale[j]), g_mix, b_mix)
        hf, hb = _moe_layer(hf, hp, router_w, router_b, moe_w_gate[i].astype(bf16), moe_w_up[i].astype(bf16),
                            moe_w_down[i].astype(bf16), row(ln_ffn_g[i]), row(ln_ffn_b[i]))
    return hf.reshape(bsz, seq, d)
```

```python
import functools
import math

import jax
import jax.numpy as jnp
from jax import lax
from jax.experimental import pallas as pl
from jax.experimental.pallas import tpu as pltpu

f32 = jnp.float32
bf16 = jnp.bfloat16

D_MODEL = 2048
DEPTH = 4
N_MIXERS = 3
S5_GROUP = 16
S5_STATE = 64
CONV_WIDTH = 31
POOL_WINDOWS = (2, 4, 8, 16)
POOL_CH = D_MODEL // len(POOL_WINDOWS)
N_EXPERTS = 16
EXPERTS_PER_GROUP = 4
D_EXPERT = D_MODEL // 2
DN_ALPHA = (2 * DEPTH) ** 0.25
LN_EPS = 1e-5

LANES = 128
SUBLANES = 8
VMEM_LIMIT = 56 * 1024 * 1024
S5_BLOCK = 8
S5_CHUNK_GROUPS = LANES // S5_GROUP
MOE_ROWS = 256
GATHER_TOKENS = 256
GATHER_UNROLL = 8
GATHER_WINDOW = 64
PACK_ROWS = D_MODEL // 2 // LANES
WIDE_ROWS = D_MODEL // LANES


def _params(*sem):
    return pltpu.CompilerParams(dimension_semantics=sem, vmem_limit_bytes=VMEM_LIMIT)


def _layer_norm(r, g, b):
    mu = jnp.mean(r, axis=-1, keepdims=True)
    xc = r - mu
    var = jnp.mean(xc * xc, axis=-1, keepdims=True)
    return xc * lax.rsqrt(var + LN_EPS) * g + b


def _dot(a, b):
    return jnp.dot(a.astype(bf16), b.astype(bf16), preferred_element_type=f32)


def _store_packed(out, hp_ref):
    tm, d = out.shape
    for c in range(PACK_ROWS):
        lo = out[:, c * LANES:(c + 1) * LANES].astype(bf16).astype(f32)
        hi = out[:, d // 2 + c * LANES:d // 2 + (c + 1) * LANES].astype(bf16).astype(f32)
        word = (lax.bitcast_convert_type(lo, jnp.uint32) >> 16) | lax.bitcast_convert_type(hi, jnp.uint32)
        hp_ref[pl.ds(c, tm, stride=PACK_ROWS), :] = word


def _load_packed(hp_ref, x_scr):
    tm, d = x_scr.shape
    for c in range(PACK_ROWS):
        word = hp_ref[pl.ds(c, tm, stride=PACK_ROWS), :]
        lo = lax.bitcast_convert_type(word << 16, f32)
        hi = lax.bitcast_convert_type(word & jnp.uint32(0xFFFF0000), f32)
        x_scr[:, c * LANES:(c + 1) * LANES] = lo.astype(bf16)
        x_scr[:, d // 2 + c * LANES:d // 2 + (c + 1) * LANES] = hi.astype(bf16)


def _mm_kernel(x_ref, w_ref, o_ref):
    o_ref[...] = _dot(x_ref[...], w_ref[...]).astype(o_ref.dtype)


def _mm(x, w, tm=512, tn=1024):
    m, k = x.shape
    n = w.shape[1]
    return pl.pallas_call(
        _mm_kernel,
        grid=(n // tn, m // tm),
        in_specs=[pl.BlockSpec((tm, k), lambda j, i: (i, 0)),
                  pl.BlockSpec((k, tn), lambda j, i: (0, j))],
        out_specs=pl.BlockSpec((tm, tn), lambda j, i: (i, j)),
        out_shape=jax.ShapeDtypeStruct((m, n), f32),
        compiler_params=_params("parallel", "parallel"),
        name="mm",
    )(x, w)


def _mm_glu_kernel(x_ref, wa_ref, wg_ref, ba_ref, bg_ref, o_ref):
    x = x_ref[...].astype(bf16)
    a = _dot(x, wa_ref[...]) + ba_ref[...]
    g = _dot(x, wg_ref[...]) + bg_ref[...]
    o_ref[...] = (a * jax.nn.sigmoid(g)).astype(o_ref.dtype)


def _mm_glu(x, w, b, out_dtype, tm=512, tn=512):
    m, k = x.shape
    n = w.shape[1] // 2
    nb = n // tn
    return pl.pallas_call(
        _mm_glu_kernel,
        grid=(nb, m // tm),
        in_specs=[pl.BlockSpec((tm, k), lambda j, i: (i, 0)),
                  pl.BlockSpec((k, tn), lambda j, i: (0, j)),
                  pl.BlockSpec((k, tn), lambda j, i: (0, j + nb)),
                  pl.BlockSpec((1, tn), lambda j, i: (0, j)),
                  pl.BlockSpec((1, tn), lambda j, i: (0, j + nb))],
        out_specs=pl.BlockSpec((tm, tn), lambda j, i: (i, j)),
        out_shape=jax.ShapeDtypeStruct((m, n), out_dtype),
        compiler_params=_params("parallel", "parallel"),
        name="mm_glu",
    )(x, w, w, b, b)


def _mm_res_ln_kernel(x_ref, w_ref, b_ref, h_ref, g_ref, beta_ref, of_ref, ob_ref, hp_ref):
    y = _dot(x_ref[...], w_ref[...]) + b_ref[...]
    out = _layer_norm(DN_ALPHA * h_ref[...] + y, g_ref[...], beta_ref[...])
    of_ref[...] = out
    ob_ref[...] = out.astype(bf16)
    _store_packed(out, hp_ref)


def _mm_res_ln(x, w, b, h, g, beta, tm=256):
    m, k = x.shape
    d = w.shape[1]
    row = lambda i: (i, 0)
    fixed = lambda i: (0, 0)
    return pl.pallas_call(
        _mm_res_ln_kernel,
        grid=(m // tm,),
        in_specs=[pl.BlockSpec((tm, k), row), pl.BlockSpec((k, d), fixed), pl.BlockSpec((1, d), fixed),
                  pl.BlockSpec((tm, d), row), pl.BlockSpec((1, d), fixed), pl.BlockSpec((1, d), fixed)],
        out_specs=[pl.BlockSpec((tm, d), row), pl.BlockSpec((tm, d), row),
                   pl.BlockSpec((tm * PACK_ROWS, LANES), row)],
        out_shape=[jax.ShapeDtypeStruct((m, d), f32), jax.ShapeDtypeStruct((m, d), bf16),
                   jax.ShapeDtypeStruct((m * PACK_ROWS, LANES), jnp.uint32)],
        compiler_params=_params("parallel"),
        name="mm_res_ln",
    )(x, w, b, h, g, beta)


def _cmul_add(xr, xi, ar, ai, sr, si):
    return xr + ar * sr - ai * si, xi + ar * si + ai * sr


def _expand_block_diag(compact_ref, rep_ref, w_scr, row_shift, col_shift):
    width = w_scr.shape[0]
    step = 2 * LANES
    for c0 in range(0, width, step):
        w = jnp.dot(compact_ref[0], rep_ref[:, c0:c0 + step], preferred_element_type=f32)
        row_g = (lax.broadcasted_iota(jnp.int32, w.shape, 0) >> row_shift) & (S5_CHUNK_GROUPS - 1)
        col_g = ((lax.broadcasted_iota(jnp.int32, w.shape, 1) + c0) >> col_shift) & (S5_CHUNK_GROUPS - 1)
        w_scr[:, c0:c0 + step] = jnp.where(row_g == col_g, w, 0.0).astype(bf16)


def _s5_kernel(u_ref, agg_ref, toe_ref, proj_ref, rep_state_ref, rep_chan_ref, dtab_ref, d_ref, z_ref,
               xb_scr, v_scr, wagg_scr, t_scr, wout_scr):
    m_rows = v_scr.shape[0]
    half = v_scr.shape[1] // 2
    ncol = half // LANES
    chan_shift = S5_GROUP.bit_length() - 1
    state_shift = S5_STATE.bit_length() - 1
    _expand_block_diag(agg_ref, rep_state_ref, wagg_scr, chan_shift, state_shift)
    _expand_block_diag(toe_ref, rep_chan_ref, t_scr, chan_shift, chan_shift)
    _expand_block_diag(proj_ref, rep_chan_ref, wout_scr, state_shift, chan_shift)
    for i in range(S5_BLOCK):
        xb_scr[:, i * LANES:(i + 1) * LANES] = u_ref[pl.ds(i, m_rows, stride=S5_BLOCK), :].astype(bf16)
    xb = xb_scr[...]
    v_scr[...] = jnp.dot(xb, wagg_scr[...], preferred_element_type=f32)

    tab = dtab_ref[0]
    sub = lax.broadcasted_iota(jnp.int32, (SUBLANES, LANES), 0)

    def col(j, part):
        lo = part * half + j * LANES
        return slice(lo, lo + LANES)

    def bcast(row):
        return jnp.broadcast_to(row, (SUBLANES, LANES))

    def body(r, carry):
        r0 = pl.multiple_of(r * SUBLANES, SUBLANES)
        new = []
        for j in range(ncol):
            cr, ci = carry[2 * j], carry[2 * j + 1]
            xr = v_scr[pl.ds(r0, SUBLANES), col(j, 0)]
            xi = v_scr[pl.ds(r0, SUBLANES), col(j, 1)]
            for shift in (1, 2, 4):
                ar = bcast(tab[shift - 1:shift, col(j, 0)])
                ai = bcast(tab[shift - 1:shift, col(j, 1)])
                sr = jnp.where(sub >= shift, pltpu.roll(xr, shift, 0), 0.0)
                si = jnp.where(sub >= shift, pltpu.roll(xi, shift, 0), 0.0)
                xr, xi = _cmul_add(xr, xi, ar, ai, sr, si)
            xr, xi = _cmul_add(xr, xi, tab[:, col(j, 0)], tab[:, col(j, 1)], cr, ci)
            v_scr[pl.ds(r0, SUBLANES), col(j, 0)] = jnp.where(sub >= 1, pltpu.roll(xr, 1, 0), cr)
            v_scr[pl.ds(r0, SUBLANES), col(j, 1)] = jnp.where(sub >= 1, pltpu.roll(xi, 1, 0), ci)
            new.append(bcast(xr[SUBLANES - 1:SUBLANES, :]))
            new.append(bcast(xi[SUBLANES - 1:SUBLANES, :]))
        return tuple(new)

    zero = jnp.zeros((SUBLANES, LANES), f32)
    lax.fori_loop(0, m_rows // SUBLANES, body, (zero,) * (2 * ncol))

    sp = v_scr[...].astype(bf16)
    for i0 in range(0, S5_BLOCK, 2):
        cs = slice(i0 * LANES, (i0 + 2) * LANES)
        y = (jnp.dot(xb, t_scr[:, cs], preferred_element_type=f32)
             + jnp.dot(sp, wout_scr[:, cs], preferred_element_type=f32))
        for i in (i0, i0 + 1):
            yi = y[:, (i - i0) * LANES:(i - i0 + 1) * LANES] + d_ref[...] * u_ref[pl.ds(i, m_rows, stride=S5_BLOCK), :]
            z_ref[pl.ds(i, m_rows, stride=S5_BLOCK), :] = 0.5 * yi * (1.0 + lax.erf(yi * (1.0 / math.sqrt(2.0))))


def _s5_core(u, agg, toe, proj, dtab, d_skip):
    n, d = u.shape
    nq = d // LANES
    m_rows = n // S5_BLOCK
    width = agg.shape[1]
    col = jnp.arange(width)[None, :]
    lane = jnp.arange(LANES)[:, None]
    half = width // 2
    rep_state = ((col // half == lane // S5_STATE) & (col % S5_STATE == lane % S5_STATE)).astype(bf16)
    rep_chan = ((col // LANES == lane // S5_GROUP) & (col % S5_GROUP == lane % S5_GROUP)).astype(bf16)
    wspec = pl.BlockSpec((1, width, LANES), lambda q: (q, 0, 0))
    rspec = pl.BlockSpec((LANES, width), lambda q: (0, 0))
    return pl.pallas_call(
        _s5_kernel,
        grid=(nq,),
        in_specs=[pl.BlockSpec((n, LANES), lambda q: (0, q)), wspec, wspec, wspec, rspec, rspec,
                  pl.BlockSpec((1, SUBLANES, width), lambda q: (q, 0, 0)),
                  pl.BlockSpec((1, LANES), lambda q: (0, q))],
        out_specs=pl.BlockSpec((n, LANES), lambda q: (0, q)),
        out_shape=jax.ShapeDtypeStruct((n, d), f32),
        scratch_shapes=[pltpu.VMEM((m_rows, width), bf16), pltpu.VMEM((m_rows, width), f32),
                        pltpu.VMEM((width, width), bf16), pltpu.VMEM((width, width), bf16),
                        pltpu.VMEM((width, width), bf16)],
        compiler_params=_params("parallel"),
        name="s5_core",
    )(u, agg, toe, proj, rep_state, rep_chan, dtab, d_skip)


def _s5_derive(b_re, b_im, c_re, c_im, a_re, a_im, log_step):
    highest = lax.Precision.HIGHEST
    g, p, c = b_re.shape
    nq = g // S5_CHUNK_GROUPS
    dt = jnp.exp(log_step)[:, None]
    mag = jnp.exp(a_re * dt)
    lb_re = mag * jnp.cos(a_im * dt)
    lb_im = mag * jnp.sin(a_im * dt)
    den = a_re * a_re + a_im * a_im
    n_re = lb_re - 1.0
    n_im = lb_im
    f_re = (n_re * a_re + n_im * a_im) / den
    f_im = (n_im * a_re - n_re * a_im) / den
    bb_re = f_re[..., None] * b_re - f_im[..., None] * b_im
    bb_im = f_re[..., None] * b_im + f_im[..., None] * b_re

    def powers(br, bi, count):
        rs, is_ = [br], [bi]
        for _ in range(count - 1):
            rs.append(rs[-1] * br - is_[-1] * bi)
            is_.append(rs[-2] * bi + is_[-1] * br)
        return rs, is_

    pr, pi = powers(lb_re, lb_im, S5_BLOCK)
    lam_r = [jnp.ones_like(lb_re)] + pr
    lam_i = [jnp.zeros_like(lb_im)] + pi
    width = S5_BLOCK * LANES

    def compact(w, perm):
        return w.transpose(perm).reshape(nq, width, LANES).astype(bf16)

    ar = jnp.stack([lam_r[S5_BLOCK - 1 - i] for i in range(S5_BLOCK)])[..., None]
    ai = jnp.stack([lam_i[S5_BLOCK - 1 - i] for i in range(S5_BLOCK)])[..., None]
    ab = jnp.stack([ar * bb_re - ai * bb_im, ar * bb_im + ai * bb_re], axis=-1)
    ab = ab.reshape(S5_BLOCK, nq, S5_CHUNK_GROUPS, p, c, 2)
    wagg = compact(ab, (1, 0, 2, 4, 5, 3))

    ar = jnp.stack(lam_r[1:S5_BLOCK + 1])[:, :, None, :]
    ai = jnp.stack(lam_i[1:S5_BLOCK + 1])[:, :, None, :]
    cl = jnp.stack([c_re * ar - c_im * ai, -(c_re * ai + c_im * ar)], axis=-1)
    cl = cl.reshape(S5_BLOCK, nq, S5_CHUNK_GROUPS, c, p, 2)
    wout = compact(cl, (1, 5, 2, 4, 0, 3))

    ar = jnp.stack(lam_r[:S5_BLOCK])[:, :, None, :]
    ai = jnp.stack(lam_i[:S5_BLOCK])[:, :, None, :]
    kj = (jnp.einsum('jgcp,gpd->jgdc', c_re * ar - c_im * ai, bb_re, precision=highest)
          - jnp.einsum('jgcp,gpd->jgdc', c_re * ai + c_im * ar, bb_im, precision=highest))
    none = jnp.zeros_like(kj[0])
    kt = jnp.stack([jnp.stack([kj[i - a] if i >= a else none for i in range(S5_BLOCK)])
                    for a in range(S5_BLOCK)])
    kt = kt.reshape(S5_BLOCK, S5_BLOCK, nq, S5_CHUNK_GROUPS, c, c)
    tmat = compact(kt, (2, 0, 3, 4, 1, 5))

    dr, di = powers(lam_r[S5_BLOCK], lam_i[S5_BLOCK], SUBLANES)
    dtab = jnp.stack([jnp.stack(dr), jnp.stack(di)], axis=1)
    dtab = dtab.reshape(SUBLANES, 2, nq, S5_CHUNK_GROUPS * p).transpose(2, 0, 1, 3).reshape(nq, SUBLANES, width)
    return wagg, tmat, wout, dtab


def _conv_kernel(v_ref, halo_ref, w_ref, b_ref, g_ref, beta_ref, o_ref, ext_scr, rot_scr, acc_scr):
    tt = v_ref.shape[0]
    pad = halo_ref.shape[0]
    first = pl.program_id(0) == 0
    ext_scr[0:pad, :] = jnp.where(first, 0.0, halo_ref[...])
    ext_scr[pad:pad + tt, :] = v_ref[...]
    span = rot_scr.shape[1]
    for sh in range(1, SUBLANES):
        rot_scr[sh - 1] = ext_scr[sh:sh + span, :]
    rows, cols = 32, 256
    for c0 in range(0, v_ref.shape[1], cols):
        for r0 in range(0, tt, rows):
            acc = jnp.broadcast_to(b_ref[:, c0:c0 + cols], (rows, cols))
            for k in range(CONV_WIDTH):
                whole, sh = divmod(pad - (CONV_WIDTH - 1) + k, SUBLANES)
                off = whole * SUBLANES + r0
                if sh == 0:
                    win = ext_scr[off:off + rows, c0:c0 + cols]
                else:
                    win = rot_scr[sh - 1, off:off + rows, c0:c0 + cols]
                acc = acc + w_ref[k:k + 1, c0:c0 + cols] * win
            acc_scr[r0:r0 + rows, c0:c0 + cols] = acc
    y = _layer_norm(acc_scr[...], g_ref[...], beta_ref[...])
    o_ref[...] = (y * jax.nn.sigmoid(y)).astype(o_ref.dtype)


def _conv_module(v, w_dw, b_dw, g, beta, tt=128, pad=32):
    n, d = v.shape
    w_pad = jnp.zeros((pad, d), f32).at[:CONV_WIDTH].set(w_dw)
    ratio = tt // pad
    row = lambda i: (i, 0)
    fixed = lambda i: (0, 0)
    return pl.pallas_call(
        _conv_kernel,
        grid=(n // tt,),
        in_specs=[pl.BlockSpec((tt, d), row),
                  pl.BlockSpec((pad, d), lambda i: (jnp.maximum(i * ratio - 1, 0), 0)),
                  pl.BlockSpec((pad, d), fixed), pl.BlockSpec((1, d), fixed),
                  pl.BlockSpec((1, d), fixed), pl.BlockSpec((1, d), fixed)],
        out_specs=pl.BlockSpec((tt, d), row),
        out_shape=jax.ShapeDtypeStruct((n, d), bf16),
        scratch_shapes=[pltpu.VMEM((tt + pad, d), f32),
                        pltpu.VMEM((SUBLANES - 1, tt + pad - SUBLANES, d), f32),
                        pltpu.VMEM((tt, d), f32)],
        compiler_params=_params("parallel"),
        name="conv_module",
    )(v, v, w_pad, b_dw, g, beta)


def _pool_kernel(h_ref, halo_ref, w_ref, scale_ref, g_ref, beta_ref, of_ref, ob_ref, hp_ref, ext_scr):
    tt = h_ref.shape[0]
    pad = halo_ref.shape[0]
    i = pl.program_id(0)
    ext_scr[0:pad, :] = jnp.where(i == 0, 0.0, halo_ref[...])
    ext_scr[pad:pad + tt, :] = h_ref[...]
    t = i * tt + lax.broadcasted_iota(jnp.int32, (tt, 1), 0)
    ys = []
    for k, win in enumerate(POOL_WINDOWS):
        cs = slice(k * POOL_CH, (k + 1) * POOL_CH)
        x = h_ref[:, cs]
        acc = x
        for j in range(1, win):
            acc = acc + ext_scr[pad - j:pad - j + tt, cs]
        cnt = jnp.minimum(t + 1, win).astype(f32)
        ys.append(_dot(acc / cnt - x, w_ref[k]))
    y = jnp.concatenate(ys, axis=1) * scale_ref[...]
    out = _layer_norm(DN_ALPHA * h_ref[...] + y, g_ref[...], beta_ref[...])
    of_ref[...] = out
    ob_ref[...] = out.astype(bf16)
    _store_packed(out, hp_ref)


def _pool_layer(h, w_grp, scale, g, beta, tt=256, pad=16):
    n, d = h.shape
    ratio = tt // pad
    row = lambda i: (i, 0)
    fixed = lambda i: (0, 0)
    return pl.pallas_call(
        _pool_kernel,
        grid=(n // tt,),
        in_specs=[pl.BlockSpec((tt, d), row),
                  pl.BlockSpec((pad, d), lambda i: (jnp.maximum(i * ratio - 1, 0), 0)),
                  pl.BlockSpec(w_grp.shape, lambda i: (0, 0, 0)),
                  pl.BlockSpec((1, d), fixed), pl.BlockSpec((1, d), fixed), pl.BlockSpec((1, d), fixed)],
        out_specs=[pl.BlockSpec((tt, d), row), pl.BlockSpec((tt, d), row),
                   pl.BlockSpec((tt * PACK_ROWS, LANES), row)],
        out_shape=[jax.ShapeDtypeStruct((n, d), f32), jax.ShapeDtypeStruct((n, d), bf16),
                   jax.ShapeDtypeStruct((n * PACK_ROWS, LANES), jnp.uint32)],
        scratch_shapes=[pltpu.VMEM((tt + pad, d), f32)],
        compiler_params=_params("parallel"),
        name="pool_layer",
    )(h, h, w_grp, scale, g, beta)


R_E1, R_E2, R_G1, R_G2, R_RANK1, R_RANK2 = range(6)


def _router_kernel(h_ref, w_ref, b_ref, info_ref, cnt_ref):
    tm = h_ref.shape[0]

    @pl.when(pl.program_id(0) == 0)
    def _():
        cnt_ref[...] = jnp.zeros_like(cnt_ref)

    logits = jnp.dot(h_ref[...], w_ref[...], preferred_element_type=f32,
                     precision=lax.Precision.HIGHEST) + b_ref[...]
    lane = lax.broadcasted_iota(jnp.int32, logits.shape, 1)
    real = lane < N_EXPERTS
    e = jnp.exp(logits - jnp.max(logits, axis=-1, keepdims=True))
    probs = e / jnp.sum(e, axis=-1, keepdims=True)

    a = probs
    b = pltpu.roll(probs, 1, 1)
    c = pltpu.roll(probs, 2, 1)
    d = pltpu.roll(probs, 3, 1)
    hi1, lo1 = jnp.maximum(a, b), jnp.minimum(a, b)
    hi2, lo2 = jnp.maximum(c, d), jnp.minimum(c, d)
    score = jnp.maximum(hi1, hi2) + jnp.maximum(jnp.minimum(hi1, hi2), jnp.maximum(lo1, lo2))
    best = None
    g_sel = None
    for grp in range(N_EXPERTS // EXPERTS_PER_GROUP):
        last = grp * EXPERTS_PER_GROUP + EXPERTS_PER_GROUP - 1
        s = jnp.max(jnp.where(lane == last, score, -1.0), axis=-1, keepdims=True)
        if grp == 0:
            best, g_sel = s, jnp.zeros_like(s, dtype=jnp.int32)
        else:
            better = s > best
            best = jnp.where(better, s, best)
            g_sel = jnp.where(better, grp, g_sel)

    in_grp = real & ((lane // EXPERTS_PER_GROUP) == g_sel)
    masked = jnp.where(in_grp, probs, -1.0)
    lane_f = lane.astype(f32)
    p1 = jnp.max(masked, axis=-1, keepdims=True)
    e1 = jnp.min(jnp.where(masked == p1, lane_f, float(LANES)), axis=-1, keepdims=True)
    masked2 = jnp.where(lane_f == e1, -2.0, masked)
    p2 = jnp.max(masked2, axis=-1, keepdims=True)
    e2 = jnp.min(jnp.where(masked2 == p2, lane_f, float(LANES)), axis=-1, keepdims=True)
    tot = p1 + p2

    oh1 = (lane_f == e1).astype(f32)
    oh2 = (lane_f == e2).astype(f32)
    both = oh1 + oh2
    ri = lax.broadcasted_iota(jnp.int32, (tm, tm), 0)
    ci = lax.broadcasted_iota(jnp.int32, (tm, tm), 1)
    tri = (ci < ri).astype(bf16)
    before = cnt_ref[0:1, :] + jnp.dot(tri, both.astype(bf16), preferred_element_type=f32)
    rank1 = jnp.sum(before * oh1, axis=-1, keepdims=True)
    rank2 = jnp.sum(before * oh2, axis=-1, keepdims=True)
    cnt_ref[...] = cnt_ref[...] + jnp.sum(both, axis=0, keepdims=True)

    info = jnp.zeros(logits.shape, f32)
    for slot, val in ((R_E1, e1), (R_E2, e2), (R_G1, p1 / tot), (R_G2, p2 / tot),
                      (R_RANK1, rank1), (R_RANK2, rank2)):
        info = jnp.where(lane == slot, val, info)
    info_ref[...] = info


def _router(h, router_w, router_b, tm=512):
    n, d = h.shape
    w = jnp.zeros((d, LANES), f32).at[:, :N_EXPERTS].set(router_w)
    b = jnp.full((1, LANES), -1e30, f32).at[0, :N_EXPERTS].set(router_b)
    return pl.pallas_call(
        _router_kernel,
        grid=(n // tm,),
        in_specs=[pl.BlockSpec((tm, d), lambda i: (i, 0)), pl.BlockSpec((d, LANES), lambda i: (0, 0)),
                  pl.BlockSpec((1, LANES), lambda i: (0, 0))],
        out_specs=[pl.BlockSpec((tm, LANES), lambda i: (i, 0)), pl.BlockSpec((SUBLANES, LANES), lambda i: (0, 0))],
        out_shape=[jax.ShapeDtypeStruct((n, LANES), f32), jax.ShapeDtypeStruct((SUBLANES, LANES), f32)],
        compiler_params=_params("arbitrary"),
        name="router",
    )(h, w, b)


def _token_copy(src_ref, dst_ref, sem, src_tok, dst_tok, rows):
    src = pl.multiple_of(src_tok * rows, SUBLANES)
    dst = pl.multiple_of(dst_tok * rows, SUBLANES)
    return pltpu.make_async_copy(src_ref.at[pl.ds(src, rows)], dst_ref.at[pl.ds(dst, rows)], sem)


def _gather_kernel(idx_ref, src_ref, o_ref, sem, *, rows):
    tm = o_ref.shape[0] // rows
    base = pl.program_id(0) * tm

    def issue(j0, carry):
        for k in range(GATHER_UNROLL):
            j = j0 * GATHER_UNROLL + k
            _token_copy(src_ref, o_ref, sem, idx_ref[base + j], j, rows).start()

        @pl.when(j0 >= GATHER_WINDOW // GATHER_UNROLL)
        def _():
            for k in range(GATHER_UNROLL):
                _token_copy(src_ref, o_ref, sem, 0, 0, rows).wait()
        return carry

    def drain(j, carry):
        _token_copy(src_ref, o_ref, sem, 0, 0, rows).wait()
        return carry

    lax.fori_loop(0, tm // GATHER_UNROLL, issue, 0)
    lax.fori_loop(0, GATHER_WINDOW, drain, 0)


def _gather_tokens(src, idx, rows, tm=GATHER_TOKENS):
    m = idx.shape[0]
    assert m % tm == 0 and tm % GATHER_UNROLL == 0 and tm >= GATHER_WINDOW
    return pl.pallas_call(
        functools.partial(_gather_kernel, rows=rows),
        grid_spec=pltpu.PrefetchScalarGridSpec(
            num_scalar_prefetch=1, grid=(m // tm,),
            in_specs=[pl.BlockSpec(memory_space=pl.ANY)],
            out_specs=pl.BlockSpec((tm * rows, LANES), lambda i, idx_ref: (i, 0)),
            scratch_shapes=[pltpu.SemaphoreType.DMA(())]),
        out_shape=jax.ShapeDtypeStruct((m * rows, LANES), src.dtype),
        compiler_params=_params("arbitrary"),
        name="gather_tokens",
    )(idx, src)


def _expert_kernel(blk_e_ref, blk_valid_ref, x_ref, wg_ref, wu_ref, wd_ref, o_ref, x_scr):
    i = pl.program_id(0)
    tm = x_scr.shape[0]

    @pl.when(blk_valid_ref[i] != 0)
    def _():
        _load_packed(x_ref, x_scr)
        x = x_scr[...]
        gate = _dot(x, wg_ref[0])
        hid = gate * jax.nn.sigmoid(gate) * _dot(x, wu_ref[0])
        y = _dot(hid, wd_ref[0])
        for c in range(WIDE_ROWS):
            o_ref[pl.ds(c, tm, stride=WIDE_ROWS), :] = y[:, c * LANES:(c + 1) * LANES]

    @pl.when(blk_valid_ref[i] == 0)
    def _():
        o_ref[...] = jnp.zeros_like(o_ref)


def _experts(xp, blk_e, blk_valid, w_gate, w_up, w_down):
    cap = xp.shape[0] // PACK_ROWS
    _, d, de = w_gate.shape
    row = lambda i, e, v: (i, 0)
    expert = lambda i, e, v: (e[i], 0, 0)
    return pl.pallas_call(
        _expert_kernel,
        grid_spec=pltpu.PrefetchScalarGridSpec(
            num_scalar_prefetch=2, grid=(cap // MOE_ROWS,),
            in_specs=[pl.BlockSpec((MOE_ROWS * PACK_ROWS, LANES), row),
                      pl.BlockSpec((1, d, de), expert), pl.BlockSpec((1, d, de), expert),
                      pl.BlockSpec((1, de, d), expert)],
            out_specs=pl.BlockSpec((MOE_ROWS * WIDE_ROWS, LANES), row),
            scratch_shapes=[pltpu.VMEM((MOE_ROWS, d), bf16)]),
        out_shape=jax.ShapeDtypeStruct((cap * WIDE_ROWS, LANES), f32),
        compiler_params=_params("arbitrary"),
        name="experts",
    )(blk_e, blk_valid, xp, w_gate, w_up, w_down)


def _load_wide(y_ref, tm):
    return jnp.concatenate([y_ref[pl.ds(c, tm, stride=WIDE_ROWS), :] for c in range(WIDE_ROWS)], axis=1)


def _moe_ln_kernel(h_ref, y1_ref, y2_ref, info_ref, g_ref, beta_ref, of_ref, ob_ref):
    tm = h_ref.shape[0]
    info = info_ref[...]
    g1 = info[:, R_G1:R_G1 + 1]
    g2 = info[:, R_G2:R_G2 + 1]
    f = g1 * _load_wide(y1_ref, tm) + g2 * _load_wide(y2_ref, tm)
    out = _layer_norm(DN_ALPHA * h_ref[...] + f, g_ref[...], beta_ref[...])
    of_ref[...] = out
    ob_ref[...] = out.astype(bf16)


def _moe_ln(h, yc, info, g, beta, tm=256):
    n, d = h.shape
    nb = n // tm
    row = lambda i: (i, 0)
    fixed = lambda i: (0, 0)
    return pl.pallas_call(
        _moe_ln_kernel,
        grid=(nb,),
        in_specs=[pl.BlockSpec((tm, d), row), pl.BlockSpec((tm * WIDE_ROWS, LANES), row),
                  pl.BlockSpec((tm * WIDE_ROWS, LANES), lambda i: (i + nb, 0)), pl.BlockSpec((tm, LANES), row),
                  pl.BlockSpec((1, d), fixed), pl.BlockSpec((1, d), fixed)],
        out_specs=[pl.BlockSpec((tm, d), row), pl.BlockSpec((tm, d), row)],
        out_shape=[jax.ShapeDtypeStruct((n, d), f32), jax.ShapeDtypeStruct((n, d), bf16)],
        compiler_params=_params("parallel"),
        name="moe_ln",
    )(h, yc, yc, info, g, beta)


def _moe_layer(hf, hp, router_w, router_b, w_gate, w_up, w_down, g, beta):
    n, d = hf.shape
    info, cnt = _router(hf, router_w, router_b)
    experts = info[:, R_E1:R_E2 + 1].astype(jnp.int32)
    rank = info[:, R_RANK1:R_RANK2 + 1].astype(jnp.int32)
    counts = cnt[0, :N_EXPERTS].astype(jnp.int32)
    pcounts = ((counts + MOE_ROWS - 1) // MOE_ROWS) * MOE_ROWS
    pends = jnp.cumsum(pcounts)
    pstarts = pends - pcounts
    dest = pstarts[experts] + rank
    cap = 2 * n + N_EXPERTS * MOE_ROWS
    tok = jnp.repeat(jnp.arange(n, dtype=jnp.int32), 2)
    row_tok = (jnp.arange(cap, dtype=jnp.int32) % n).at[dest.reshape(-1)].set(tok, unique_indices=True)
    blk_start = jnp.arange(cap // MOE_ROWS, dtype=jnp.int32) * MOE_ROWS
    blk_e = jnp.sum((blk_start[:, None] >= pends[None, :]).astype(jnp.int32), axis=1)
    blk_e = jnp.minimum(blk_e, N_EXPERTS - 1)
    blk_valid = (blk_start < pends[-1]).astype(jnp.int32)
    last_e = jnp.max(jnp.where(counts > 0, jnp.arange(N_EXPERTS, dtype=jnp.int32), 0))
    blk_e = jnp.where(blk_valid != 0, blk_e, last_e)

    xp = _gather_tokens(hp, row_tok, PACK_ROWS)
    y = _experts(xp, blk_e, blk_valid, w_gate, w_up, w_down)
    yc = _gather_tokens(y, dest.T.reshape(-1), WIDE_ROWS)
    return _moe_ln(hf, yc, info, g, beta)


def kernel(x, s5_w_in, s5_b_re, s5_b_im, s5_c_re, s5_c_im, s5_a_re, s5_a_im, s5_log_step, s5_d, s5_w_glu, s5_w_out, cv_w_pw1, cv_b_pw1, cv_w_dw, cv_b_dw, cv_ln_g, cv_ln_b, cv_w_pw2, cv_b_pw2, pl_w, pl_scale, router_w, router_b, moe_w_gate, moe_w_up, moe_w_down, ln_mix_g, ln_mix_b, ln_ffn_g, ln_ffn_b):
    bsz, seq, d = x.shape
    hf = x.reshape(bsz * seq, d)
    hb = hf.astype(bf16)
    row = lambda v: v.reshape(1, -1)

    def layer_bf16(w, layer):
        idx = lax.optimization_barrier(jnp.asarray(layer, jnp.int32))
        return lax.dynamic_index_in_dim(w, idx, axis=0, keepdims=False).astype(bf16)

    for i in range(DEPTH):
        mixer, j = i % N_MIXERS, i // N_MIXERS
        g_mix, b_mix = row(ln_mix_g[i]), row(ln_mix_b[i])
        if mixer == 0:
            u = _mm(hb, s5_w_in[j].astype(bf16))
            wagg, tmat, wout, dtab = _s5_derive(s5_b_re[j], s5_b_im[j], s5_c_re[j], s5_c_im[j],
                                                s5_a_re[j], s5_a_im[j], s5_log_step[j])
            z = _s5_core(u, wagg, tmat, wout, dtab, row(s5_d[j]))
            v = _mm_glu(z, s5_w_glu[j].astype(bf16), jnp.zeros((1, 2 * d), f32), bf16)
            hf, hb, hp = _mm_res_ln(v, s5_w_out[j].astype(bf16), jnp.zeros((1, d), f32), hf, g_mix, b_mix)
        elif mixer == 1:
            v = _mm_glu(hb, cv_w_pw1[j].astype(bf16), row(cv_b_pw1[j]), f32)
            cv = _conv_module(v, cv_w_dw[j], row(cv_b_dw[j]), row(cv_ln_g[j]), row(cv_ln_b[j]))
            hf, hb, hp = _mm_res_ln(cv, cv_w_pw2[j].astype(bf16), row(cv_b_pw2[j]), hf, g_mix, b_mix)
        else:
            hf, hb, hp = _pool_layer(hf, pl_w[j].astype(bf16), row(pl_scale[j]), g_mix, b_mix)
        hf, hb = _moe_layer(hf, hp, router_w, router_b, layer_bf16(moe_w_gate, i), layer_bf16(moe_w_up, i),
                            layer_bf16(moe_w_down, i), row(ln_ffn_g[i]), row(ln_ffn_b[i]))
    return hf.reshape(bsz, seq, d)
```

```python
import functools
import math

import jax
import jax.numpy as jnp
from jax import lax
from jax.experimental import pallas as pl
from jax.experimental.pallas import tpu as pltpu

f32 = jnp.float32
bf16 = jnp.bfloat16

D_MODEL = 2048
DEPTH = 4
N_MIXERS = 3
S5_GROUP = 16
S5_STATE = 64
CONV_WIDTH = 31
POOL_WINDOWS = (2, 4, 8, 16)
POOL_CH = D_MODEL // len(POOL_WINDOWS)
N_EXPERTS = 16
EXPERTS_PER_GROUP = 4
D_EXPERT = D_MODEL // 2
DN_ALPHA = (2 * DEPTH) ** 0.25
LN_EPS = 1e-5

LANES = 128
SUBLANES = 8
VMEM_LIMIT = 56 * 1024 * 1024
S5_BLOCK = 8
S5_CHUNK_GROUPS = LANES // S5_GROUP
MOE_ROWS = 256
GATHER_TOKENS = 256
GATHER_UNROLL = 8
GATHER_WINDOW = 64
PACK_ROWS = D_MODEL // 2 // LANES
WIDE_ROWS = D_MODEL // LANES


def _params(*sem):
    return pltpu.CompilerParams(dimension_semantics=sem, vmem_limit_bytes=VMEM_LIMIT)


def _layer_norm(r, g, b):
    mu = jnp.mean(r, axis=-1, keepdims=True)
    xc = r - mu
    var = jnp.mean(xc * xc, axis=-1, keepdims=True)
    return xc * lax.rsqrt(var + LN_EPS) * g + b


def _dot(a, b):
    return jnp.dot(a.astype(bf16), b.astype(bf16), preferred_element_type=f32)


def _store_packed(out, hp_ref):
    tm, d = out.shape
    for c in range(PACK_ROWS):
        lo = out[:, c * LANES:(c + 1) * LANES].astype(bf16).astype(f32)
        hi = out[:, d // 2 + c * LANES:d // 2 + (c + 1) * LANES].astype(bf16).astype(f32)
        word = (lax.bitcast_convert_type(lo, jnp.uint32) >> 16) | lax.bitcast_convert_type(hi, jnp.uint32)
        hp_ref[pl.ds(c, tm, stride=PACK_ROWS), :] = word


def _load_packed(hp_ref, x_scr):
    tm, d = x_scr.shape
    for c in range(PACK_ROWS):
        word = hp_ref[pl.ds(c, tm, stride=PACK_ROWS), :]
        lo = lax.bitcast_convert_type(word << 16, f32)
        hi = lax.bitcast_convert_type(word & jnp.uint32(0xFFFF0000), f32)
        x_scr[:, c * LANES:(c + 1) * LANES] = lo.astype(bf16)
        x_scr[:, d // 2 + c * LANES:d // 2 + (c + 1) * LANES] = hi.astype(bf16)


def _mm_kernel(x_ref, w_ref, o_ref):
    o_ref[...] = _dot(x_ref[...], w_ref[...]).astype(o_ref.dtype)


def _mm(x, w, tm=512, tn=1024):
    m, k = x.shape
    n = w.shape[1]
    return pl.pallas_call(
        _mm_kernel,
        grid=(n // tn, m // tm),
        in_specs=[pl.BlockSpec((tm, k), lambda j, i: (i, 0)),
                  pl.BlockSpec((k, tn), lambda j, i: (0, j))],
        out_specs=pl.BlockSpec((tm, tn), lambda j, i: (i, j)),
        out_shape=jax.ShapeDtypeStruct((m, n), f32),
        compiler_params=_params("parallel", "parallel"),
        name="mm",
    )(x, w)


def _mm_glu_kernel(x_ref, wa_ref, wg_ref, ba_ref, bg_ref, o_ref):
    x = x_ref[...].astype(bf16)
    a = _dot(x, wa_ref[...]) + ba_ref[...]
    g = _dot(x, wg_ref[...]) + bg_ref[...]
    o_ref[...] = (a * jax.nn.sigmoid(g)).astype(o_ref.dtype)


def _mm_glu(x, w, b, out_dtype, tm=512, tn=512):
    m, k = x.shape
    n = w.shape[1] // 2
    nb = n // tn
    return pl.pallas_call(
        _mm_glu_kernel,
        grid=(nb, m // tm),
        in_specs=[pl.BlockSpec((tm, k), lambda j, i: (i, 0)),
                  pl.BlockSpec((k, tn), lambda j, i: (0, j)),
                  pl.BlockSpec((k, tn), lambda j, i: (0, j + nb)),
                  pl.BlockSpec((1, tn), lambda j, i: (0, j)),
                  pl.BlockSpec((1, tn), lambda j, i: (0, j + nb))],
        out_specs=pl.BlockSpec((tm, tn), lambda j, i: (i, j)),
        out_shape=jax.ShapeDtypeStruct((m, n), out_dtype),
        compiler_params=_params("parallel", "parallel"),
        name="mm_glu",
    )(x, w, w, b, b)


def _mm_res_ln_kernel(x_ref, w_ref, b_ref, h_ref, g_ref, beta_ref, of_ref, ob_ref, hp_ref):
    y = _dot(x_ref[...], w_ref[...]) + b_ref[...]
    out = _layer_norm(DN_ALPHA * h_ref[...] + y, g_ref[...], beta_ref[...])
    of_ref[...] = out
    ob_ref[...] = out.astype(bf16)
    _store_packed(out, hp_ref)


def _mm_res_ln(x, w, b, h, g, beta, tm=256):
    m, k = x.shape
    d = w.shape[1]
    row = lambda i: (i, 0)
    fixed = lambda i: (0, 0)
    return pl.pallas_call(
        _mm_res_ln_kernel,
        grid=(m // tm,),
        in_specs=[pl.BlockSpec((tm, k), row), pl.BlockSpec((k, d), fixed), pl.BlockSpec((1, d), fixed),
                  pl.BlockSpec((tm, d), row), pl.BlockSpec((1, d), fixed), pl.BlockSpec((1, d), fixed)],
        out_specs=[pl.BlockSpec((tm, d), row), pl.BlockSpec((tm, d), row),
                   pl.BlockSpec((tm * PACK_ROWS, LANES), row)],
        out_shape=[jax.ShapeDtypeStruct((m, d), f32), jax.ShapeDtypeStruct((m, d), bf16),
                   jax.ShapeDtypeStruct((m * PACK_ROWS, LANES), jnp.uint32)],
        compiler_params=_params("parallel"),
        name="mm_res_ln",
    )(x, w, b, h, g, beta)


def _cmul_add(xr, xi, ar, ai, sr, si):
    return xr + ar * sr - ai * si, xi + ar * si + ai * sr


def _expand_block_diag(compact_ref, rep_ref, w_scr, row_shift, col_shift):
    width = w_scr.shape[0]
    step = 2 * LANES
    for c0 in range(0, width, step):
        w = jnp.dot(compact_ref[0], rep_ref[:, c0:c0 + step], preferred_element_type=f32)
        row_g = (lax.broadcasted_iota(jnp.int32, w.shape, 0) >> row_shift) & (S5_CHUNK_GROUPS - 1)
        col_g = ((lax.broadcasted_iota(jnp.int32, w.shape, 1) + c0) >> col_shift) & (S5_CHUNK_GROUPS - 1)
        w_scr[:, c0:c0 + step] = jnp.where(row_g == col_g, w, 0.0).astype(bf16)


def _s5_kernel(u_ref, agg_ref, toe_ref, proj_ref, rep_state_ref, rep_chan_ref, dtab_ref, d_ref, z_ref,
               xb_scr, v_scr, wagg_scr, t_scr, wout_scr):
    m_rows = v_scr.shape[0]
    half = v_scr.shape[1] // 2
    ncol = half // LANES
    chan_shift = S5_GROUP.bit_length() - 1
    state_shift = S5_STATE.bit_length() - 1
    _expand_block_diag(agg_ref, rep_state_ref, wagg_scr, chan_shift, state_shift)
    _expand_block_diag(toe_ref, rep_chan_ref, t_scr, chan_shift, chan_shift)
    _expand_block_diag(proj_ref, rep_chan_ref, wout_scr, state_shift, chan_shift)
    for i in range(S5_BLOCK):
        xb_scr[:, i * LANES:(i + 1) * LANES] = u_ref[pl.ds(i, m_rows, stride=S5_BLOCK), :].astype(bf16)
    xb = xb_scr[...]
    v_scr[...] = jnp.dot(xb, wagg_scr[...], preferred_element_type=f32)

    tab = dtab_ref[0]
    sub = lax.broadcasted_iota(jnp.int32, (SUBLANES, LANES), 0)

    def col(j, part):
        lo = part * half + j * LANES
        return slice(lo, lo + LANES)

    def bcast(row):
        return jnp.broadcast_to(row, (SUBLANES, LANES))

    def body(r, carry):
        r0 = pl.multiple_of(r * SUBLANES, SUBLANES)
        new = []
        for j in range(ncol):
            cr, ci = carry[2 * j], carry[2 * j + 1]
            xr = v_scr[pl.ds(r0, SUBLANES), col(j, 0)]
            xi = v_scr[pl.ds(r0, SUBLANES), col(j, 1)]
            for shift in (1, 2, 4):
                ar = bcast(tab[shift - 1:shift, col(j, 0)])
                ai = bcast(tab[shift - 1:shift, col(j, 1)])
                sr = jnp.where(sub >= shift, pltpu.roll(xr, shift, 0), 0.0)
                si = jnp.where(sub >= shift, pltpu.roll(xi, shift, 0), 0.0)
                xr, xi = _cmul_add(xr, xi, ar, ai, sr, si)
            xr, xi = _cmul_add(xr, xi, tab[:, col(j, 0)], tab[:, col(j, 1)], cr, ci)
            v_scr[pl.ds(r0, SUBLANES), col(j, 0)] = jnp.where(sub >= 1, pltpu.roll(xr, 1, 0), cr)
            v_scr[pl.ds(r0, SUBLANES), col(j, 1)] = jnp.where(sub >= 1, pltpu.roll(xi, 1, 0), ci)
            new.append(bcast(xr[SUBLANES - 1:SUBLANES, :]))
            new.append(bcast(xi[SUBLANES - 1:SUBLANES, :]))
        return tuple(new)

    zero = jnp.zeros((SUBLANES, LANES), f32)
    lax.fori_loop(0, m_rows // SUBLANES, body, (zero,) * (2 * ncol))

    sp = v_scr[...].astype(bf16)
    for i0 in range(0, S5_BLOCK, 2):
        cs = slice(i0 * LANES, (i0 + 2) * LANES)
        y = (jnp.dot(xb, t_scr[:, cs], preferred_element_type=f32)
             + jnp.dot(sp, wout_scr[:, cs], preferred_element_type=f32))
        for i in (i0, i0 + 1):
            yi = y[:, (i - i0) * LANES:(i - i0 + 1) * LANES] + d_ref[...] * u_ref[pl.ds(i, m_rows, stride=S5_BLOCK), :]
            z_ref[pl.ds(i, m_rows, stride=S5_BLOCK), :] = 0.5 * yi * (1.0 + lax.erf(yi * (1.0 / math.sqrt(2.0))))


def _s5_core(u, agg, toe, proj, dtab, d_skip):
    n, d = u.shape
    nq = d // LANES
    m_rows = n // S5_BLOCK
    width = agg.shape[1]
    col = jnp.arange(width)[None, :]
    lane = jnp.arange(LANES)[:, None]
    half = width // 2
    rep_state = ((col // half == lane // S5_STATE) & (col % S5_STATE == lane % S5_STATE)).astype(bf16)
    rep_chan = ((col // LANES == lane // S5_GROUP) & (col % S5_GROUP == lane % S5_GROUP)).astype(bf16)
    wspec = pl.BlockSpec((1, width, LANES), lambda q: (q, 0, 0))
    rspec = pl.BlockSpec((LANES, width), lambda q: (0, 0))
    return pl.pallas_call(
        _s5_kernel,
        grid=(nq,),
        in_specs=[pl.BlockSpec((n, LANES), lambda q: (0, q)), wspec, wspec, wspec, rspec, rspec,
                  pl.BlockSpec((1, SUBLANES, width), lambda q: (q, 0, 0)),
                  pl.BlockSpec((1, LANES), lambda q: (0, q))],
        out_specs=pl.BlockSpec((n, LANES), lambda q: (0, q)),
        out_shape=jax.ShapeDtypeStruct((n, d), f32),
        scratch_shapes=[pltpu.VMEM((m_rows, width), bf16), pltpu.VMEM((m_rows, width), f32),
                        pltpu.VMEM((width, width), bf16), pltpu.VMEM((width, width), bf16),
                        pltpu.VMEM((width, width), bf16)],
        compiler_params=_params("parallel"),
        name="s5_core",
    )(u, agg, toe, proj, rep_state, rep_chan, dtab, d_skip)


def _s5_derive(b_re, b_im, c_re, c_im, a_re, a_im, log_step):
    highest = lax.Precision.HIGHEST
    g, p, c = b_re.shape
    nq = g // S5_CHUNK_GROUPS
    dt = jnp.exp(log_step)[:, None]
    mag = jnp.exp(a_re * dt)
    lb_re = mag * jnp.cos(a_im * dt)
    lb_im = mag * jnp.sin(a_im * dt)
    den = a_re * a_re + a_im * a_im
    n_re = lb_re - 1.0
    n_im = lb_im
    f_re = (n_re * a_re + n_im * a_im) / den
    f_im = (n_im * a_re - n_re * a_im) / den
    bb_re = f_re[..., None] * b_re - f_im[..., None] * b_im
    bb_im = f_re[..., None] * b_im + f_im[..., None] * b_re

    def powers(br, bi, count):
        rs, is_ = [br], [bi]
        for _ in range(count - 1):
            rs.append(rs[-1] * br - is_[-1] * bi)
            is_.append(rs[-2] * bi + is_[-1] * br)
        return rs, is_

    pr, pi = powers(lb_re, lb_im, S5_BLOCK)
    lam_r = [jnp.ones_like(lb_re)] + pr
    lam_i = [jnp.zeros_like(lb_im)] + pi
    width = S5_BLOCK * LANES

    def compact(w, perm):
        return w.transpose(perm).reshape(nq, width, LANES).astype(bf16)

    ar = jnp.stack([lam_r[S5_BLOCK - 1 - i] for i in range(S5_BLOCK)])[..., None]
    ai = jnp.stack([lam_i[S5_BLOCK - 1 - i] for i in range(S5_BLOCK)])[..., None]
    ab = jnp.stack([ar * bb_re - ai * bb_im, ar * bb_im + ai * bb_re], axis=-1)
    ab = ab.reshape(S5_BLOCK, nq, S5_CHUNK_GROUPS, p, c, 2)
    wagg = compact(ab, (1, 0, 2, 4, 5, 3))

    ar = jnp.stack(lam_r[1:S5_BLOCK + 1])[:, :, None, :]
    ai = jnp.stack(lam_i[1:S5_BLOCK + 1])[:, :, None, :]
    cl = jnp.stack([c_re * ar - c_im * ai, -(c_re * ai + c_im * ar)], axis=-1)
    cl = cl.reshape(S5_BLOCK, nq, S5_CHUNK_GROUPS, c, p, 2)
    wout = compact(cl, (1, 5, 2, 4, 0, 3))

    ar = jnp.stack(lam_r[:S5_BLOCK])[:, :, None, :]
    ai = jnp.stack(lam_i[:S5_BLOCK])[:, :, None, :]
    kj = (jnp.einsum('jgcp,gpd->jgdc', c_re * ar - c_im * ai, bb_re, precision=highest)
          - jnp.einsum('jgcp,gpd->jgdc', c_re * ai + c_im * ar, bb_im, precision=highest))
    none = jnp.zeros_like(kj[0])
    kt = jnp.stack([jnp.stack([kj[i - a] if i >= a else none for i in range(S5_BLOCK)])
                    for a in range(S5_BLOCK)])
    kt = kt.reshape(S5_BLOCK, S5_BLOCK, nq, S5_CHUNK_GROUPS, c, c)
    tmat = compact(kt, (2, 0, 3, 4, 1, 5))

    dr, di = powers(lam_r[S5_BLOCK], lam_i[S5_BLOCK], SUBLANES)
    dtab = jnp.stack([jnp.stack(dr), jnp.stack(di)], axis=1)
    dtab = dtab.reshape(SUBLANES, 2, nq, S5_CHUNK_GROUPS * p).transpose(2, 0, 1, 3).reshape(nq, SUBLANES, width)
    return wagg, tmat, wout, dtab


def _conv_kernel(v_ref, halo_ref, w_ref, b_ref, g_ref, beta_ref, o_ref, ext_scr, rot_scr, acc_scr):
    tt = v_ref.shape[0]
    pad = halo_ref.shape[0]
    first = pl.program_id(0) == 0
    ext_scr[0:pad, :] = jnp.where(first, 0.0, halo_ref[...])
    ext_scr[pad:pad + tt, :] = v_ref[...]
    span = rot_scr.shape[1]
    for sh in range(1, SUBLANES):
        rot_scr[sh - 1] = ext_scr[sh:sh + span, :]
    rows, cols = 32, 256
    for c0 in range(0, v_ref.shape[1], cols):
        for r0 in range(0, tt, rows):
            acc = jnp.broadcast_to(b_ref[:, c0:c0 + cols], (rows, cols))
            for k in range(CONV_WIDTH):
                whole, sh = divmod(pad - (CONV_WIDTH - 1) + k, SUBLANES)
                off = whole * SUBLANES + r0
                if sh == 0:
                    win = ext_scr[off:off + rows, c0:c0 + cols]
                else:
                    win = rot_scr[sh - 1, off:off + rows, c0:c0 + cols]
                acc = acc + w_ref[k:k + 1, c0:c0 + cols] * win
            acc_scr[r0:r0 + rows, c0:c0 + cols] = acc
    y = _layer_norm(acc_scr[...], g_ref[...], beta_ref[...])
    o_ref[...] = (y * jax.nn.sigmoid(y)).astype(o_ref.dtype)


def _conv_module(v, w_dw, b_dw, g, beta, tt=128, pad=32):
    n, d = v.shape
    w_pad = jnp.zeros((pad, d), f32).at[:CONV_WIDTH].set(w_dw)
    ratio = tt // pad
    row = lambda i: (i, 0)
    fixed = lambda i: (0, 0)
    return pl.pallas_call(
        _conv_kernel,
        grid=(n // tt,),
        in_specs=[pl.BlockSpec((tt, d), row),
                  pl.BlockSpec((pad, d), lambda i: (jnp.maximum(i * ratio - 1, 0), 0)),
                  pl.BlockSpec((pad, d), fixed), pl.BlockSpec((1, d), fixed),
                  pl.BlockSpec((1, d), fixed), pl.BlockSpec((1, d), fixed)],
        out_specs=pl.BlockSpec((tt, d), row),
        out_shape=jax.ShapeDtypeStruct((n, d), bf16),
        scratch_shapes=[pltpu.VMEM((tt + pad, d), f32),
                        pltpu.VMEM((SUBLANES - 1, tt + pad - SUBLANES, d), f32),
                        pltpu.VMEM((tt, d), f32)],
        compiler_params=_params("parallel"),
        name="conv_module",
    )(v, v, w_pad, b_dw, g, beta)


def _pool_kernel(h_ref, halo_ref, w_ref, scale_ref, g_ref, beta_ref, of_ref, ob_ref, hp_ref, ext_scr):
    tt = h_ref.shape[0]
    pad = halo_ref.shape[0]
    i = pl.program_id(0)
    ext_scr[0:pad, :] = jnp.where(i == 0, 0.0, halo_ref[...])
    ext_scr[pad:pad + tt, :] = h_ref[...]
    t = i * tt + lax.broadcasted_iota(jnp.int32, (tt, 1), 0)
    ys = []
    for k, win in enumerate(POOL_WINDOWS):
        cs = slice(k * POOL_CH, (k + 1) * POOL_CH)
        x = h_ref[:, cs]
        acc = x
        for j in range(1, win):
            acc = acc + ext_scr[pad - j:pad - j + tt, cs]
        cnt = jnp.minimum(t + 1, win).astype(f32)
        ys.append(_dot(acc / cnt - x, w_ref[k]))
    y = jnp.concatenate(ys, axis=1) * scale_ref[...]
    out = _layer_norm(DN_ALPHA * h_ref[...] + y, g_ref[...], beta_ref[...])
    of_ref[...] = out
    ob_ref[...] = out.astype(bf16)
    _store_packed(out, hp_ref)


def _pool_layer(h, w_grp, scale, g, beta, tt=256, pad=16):
    n, d = h.shape
    ratio = tt // pad
    row = lambda i: (i, 0)
    fixed = lambda i: (0, 0)
    return pl.pallas_call(
        _pool_kernel,
        grid=(n // tt,),
        in_specs=[pl.BlockSpec((tt, d), row),
                  pl.BlockSpec((pad, d), lambda i: (jnp.maximum(i * ratio - 1, 0), 0)),
                  pl.BlockSpec(w_grp.shape, lambda i: (0, 0, 0)),
                  pl.BlockSpec((1, d), fixed), pl.BlockSpec((1, d), fixed), pl.BlockSpec((1, d), fixed)],
        out_specs=[pl.BlockSpec((tt, d), row), pl.BlockSpec((tt, d), row),
                   pl.BlockSpec((tt * PACK_ROWS, LANES), row)],
        out_shape=[jax.ShapeDtypeStruct((n, d), f32), jax.ShapeDtypeStruct((n, d), bf16),
                   jax.ShapeDtypeStruct((n * PACK_ROWS, LANES), jnp.uint32)],
        scratch_shapes=[pltpu.VMEM((tt + pad, d), f32)],
        compiler_params=_params("parallel"),
        name="pool_layer",
    )(h, h, w_grp, scale, g, beta)


R_E1, R_E2, R_G1, R_G2, R_RANK1, R_RANK2 = range(6)


def _router_kernel(h_ref, whi_ref, wlo_ref, b_ref, info_ref, cnt_ref):
    tm = h_ref.shape[0]

    @pl.when(pl.program_id(0) == 0)
    def _():
        cnt_ref[...] = jnp.zeros_like(cnt_ref)

    h = h_ref[...]
    h_hi = h.astype(bf16)
    h_lo = (h - h_hi.astype(f32)).astype(bf16)
    logits = (jnp.dot(h_hi, whi_ref[...], preferred_element_type=f32)
              + jnp.dot(h_lo, whi_ref[...], preferred_element_type=f32)
              + jnp.dot(h_hi, wlo_ref[...], preferred_element_type=f32)) + b_ref[...]
    lane = lax.broadcasted_iota(jnp.int32, logits.shape, 1)
    real = lane < N_EXPERTS
    e = jnp.exp(logits - jnp.max(logits, axis=-1, keepdims=True))
    probs = e / jnp.sum(e, axis=-1, keepdims=True)

    a = probs
    b = pltpu.roll(probs, 1, 1)
    c = pltpu.roll(probs, 2, 1)
    d = pltpu.roll(probs, 3, 1)
    hi1, lo1 = jnp.maximum(a, b), jnp.minimum(a, b)
    hi2, lo2 = jnp.maximum(c, d), jnp.minimum(c, d)
    score = jnp.maximum(hi1, hi2) + jnp.maximum(jnp.minimum(hi1, hi2), jnp.maximum(lo1, lo2))
    best = None
    g_sel = None
    for grp in range(N_EXPERTS // EXPERTS_PER_GROUP):
        last = grp * EXPERTS_PER_GROUP + EXPERTS_PER_GROUP - 1
        s = jnp.max(jnp.where(lane == last, score, -1.0), axis=-1, keepdims=True)
        if grp == 0:
            best, g_sel = s, jnp.zeros_like(s, dtype=jnp.int32)
        else:
            better = s > best
            best = jnp.where(better, s, best)
            g_sel = jnp.where(better, grp, g_sel)

    in_grp = real & ((lane // EXPERTS_PER_GROUP) == g_sel)
    masked = jnp.where(in_grp, probs, -1.0)
    lane_f = lane.astype(f32)
    p1 = jnp.max(masked, axis=-1, keepdims=True)
    e1 = jnp.min(jnp.where(masked == p1, lane_f, float(LANES)), axis=-1, keepdims=True)
    masked2 = jnp.where(lane_f == e1, -2.0, masked)
    p2 = jnp.max(masked2, axis=-1, keepdims=True)
    e2 = jnp.min(jnp.where(masked2 == p2, lane_f, float(LANES)), axis=-1, keepdims=True)
    tot = p1 + p2

    oh1 = (lane_f == e1).astype(f32)
    oh2 = (lane_f == e2).astype(f32)
    both = oh1 + oh2
    ri = lax.broadcasted_iota(jnp.int32, (tm, tm), 0)
    ci = lax.broadcasted_iota(jnp.int32, (tm, tm), 1)
    tri = (ci < ri).astype(bf16)
    before = cnt_ref[0:1, :] + jnp.dot(tri, both.astype(bf16), preferred_element_type=f32)
    rank1 = jnp.sum(before * oh1, axis=-1, keepdims=True)
    rank2 = jnp.sum(before * oh2, axis=-1, keepdims=True)
    cnt_ref[...] = cnt_ref[...] + jnp.sum(both, axis=0, keepdims=True)

    info = jnp.zeros(logits.shape, f32)
    for slot, val in ((R_E1, e1), (R_E2, e2), (R_G1, p1 / tot), (R_G2, p2 / tot),
                      (R_RANK1, rank1), (R_RANK2, rank2)):
        info = jnp.where(lane == slot, val, info)
    info_ref[...] = info


def _router(h, router_w, router_b, tm=512):
    n, d = h.shape
    w = jnp.zeros((d, LANES), f32).at[:, :N_EXPERTS].set(router_w)
    w_hi = w.astype(bf16)
    w_lo = (w - w_hi.astype(f32)).astype(bf16)
    b = jnp.full((1, LANES), -1e30, f32).at[0, :N_EXPERTS].set(router_b)
    return pl.pallas_call(
        _router_kernel,
        grid=(n // tm,),
        in_specs=[pl.BlockSpec((tm, d), lambda i: (i, 0)), pl.BlockSpec((d, LANES), lambda i: (0, 0)),
                  pl.BlockSpec((d, LANES), lambda i: (0, 0)), pl.BlockSpec((1, LANES), lambda i: (0, 0))],
        out_specs=[pl.BlockSpec((tm, LANES), lambda i: (i, 0)), pl.BlockSpec((SUBLANES, LANES), lambda i: (0, 0))],
        out_shape=[jax.ShapeDtypeStruct((n, LANES), f32), jax.ShapeDtypeStruct((SUBLANES, LANES), f32)],
        compiler_params=_params("arbitrary"),
        name="router",
    )(h, w_hi, w_lo, b)


def _token_copy(src_ref, dst_ref, sem, src_tok, dst_tok, rows):
    src = pl.multiple_of(src_tok * rows, SUBLANES)
    dst = pl.multiple_of(dst_tok * rows, SUBLANES)
    return pltpu.make_async_copy(src_ref.at[pl.ds(src, rows)], dst_ref.at[pl.ds(dst, rows)], sem)


def _gather_kernel(idx_ref, src_ref, o_ref, sem, *, rows):
    tm = o_ref.shape[0] // rows
    base = pl.program_id(0) * tm

    def issue(j0, carry):
        for k in range(GATHER_UNROLL):
            j = j0 * GATHER_UNROLL + k
            _token_copy(src_ref, o_ref, sem, idx_ref[base + j], j, rows).start()

        @pl.when(j0 >= GATHER_WINDOW // GATHER_UNROLL)
        def _():
            for k in range(GATHER_UNROLL):
                _token_copy(src_ref, o_ref, sem, 0, 0, rows).wait()
        return carry

    def drain(j, carry):
        _token_copy(src_ref, o_ref, sem, 0, 0, rows).wait()
        return carry

    lax.fori_loop(0, tm // GATHER_UNROLL, issue, 0)
    lax.fori_loop(0, GATHER_WINDOW, drain, 0)


def _gather_tokens(src, idx, rows, tm=GATHER_TOKENS):
    m = idx.shape[0]
    assert m % tm == 0 and tm % GATHER_UNROLL == 0 and tm >= GATHER_WINDOW
    return pl.pallas_call(
        functools.partial(_gather_kernel, rows=rows),
        grid_spec=pltpu.PrefetchScalarGridSpec(
            num_scalar_prefetch=1, grid=(m // tm,),
            in_specs=[pl.BlockSpec(memory_space=pl.ANY)],
            out_specs=pl.BlockSpec((tm * rows, LANES), lambda i, idx_ref: (i, 0)),
            scratch_shapes=[pltpu.SemaphoreType.DMA(())]),
        out_shape=jax.ShapeDtypeStruct((m * rows, LANES), src.dtype),
        compiler_params=_params("arbitrary"),
        name="gather_tokens",
    )(idx, src)


def _expert_kernel(blk_e_ref, blk_valid_ref, row_tok_ref, hp_ref, wg_ref, wu_ref, wd_ref, o_ref,
                   xg_scr, x_scr, sems):
    i = pl.program_id(0)
    tm = x_scr.shape[0]
    nblk = pl.num_programs(0)

    def token_copy(blk, j, tok):
        slot = blk % 2
        return _token_copy(hp_ref, xg_scr.at[slot], sems.at[slot], tok, j, PACK_ROWS)

    def start_gather(blk):
        def body(j0, carry):
            for k in range(GATHER_UNROLL):
                j = j0 * GATHER_UNROLL + k
                token_copy(blk, j, row_tok_ref[blk * tm + j]).start()
            return carry
        lax.fori_loop(0, tm // GATHER_UNROLL, body, 0)

    def wait_gather(blk):
        def body(j0, carry):
            for k in range(GATHER_UNROLL):
                token_copy(blk, 0, 0).wait()
            return carry
        lax.fori_loop(0, tm // GATHER_UNROLL, body, 0)

    @pl.when(i == 0)
    def _():
        start_gather(0)

    @pl.when((i + 1 < nblk) & (blk_valid_ref[jnp.minimum(i + 1, nblk - 1)] != 0))
    def _():
        start_gather(i + 1)

    @pl.when(blk_valid_ref[i] != 0)
    def _():
        wait_gather(i)
        _load_packed(xg_scr.at[i % 2], x_scr)
        x = x_scr[...]
        gate = _dot(x, wg_ref[0])
        hid = gate * jax.nn.sigmoid(gate) * _dot(x, wu_ref[0])
        y = _dot(hid, wd_ref[0])
        for c in range(WIDE_ROWS):
            o_ref[pl.ds(c, tm, stride=WIDE_ROWS), :] = y[:, c * LANES:(c + 1) * LANES]

    @pl.when(blk_valid_ref[i] == 0)
    def _():
        o_ref[...] = jnp.zeros_like(o_ref)


def _experts(hp, row_tok, blk_e, blk_valid, w_gate, w_up, w_down):
    cap = row_tok.shape[0]
    _, d, de = w_gate.shape
    row = lambda i, *_: (i, 0)
    expert = lambda i, e, *_: (e[i], 0, 0)
    return pl.pallas_call(
        _expert_kernel,
        grid_spec=pltpu.PrefetchScalarGridSpec(
            num_scalar_prefetch=3, grid=(cap // MOE_ROWS,),
            in_specs=[pl.BlockSpec(memory_space=pl.ANY),
                      pl.BlockSpec((1, d, de), expert), pl.BlockSpec((1, d, de), expert),
                      pl.BlockSpec((1, de, d), expert)],
            out_specs=pl.BlockSpec((MOE_ROWS * WIDE_ROWS, LANES), row),
            scratch_shapes=[pltpu.VMEM((2, MOE_ROWS * PACK_ROWS, LANES), jnp.uint32),
                            pltpu.VMEM((MOE_ROWS, d), bf16), pltpu.SemaphoreType.DMA((2,))]),
        out_shape=jax.ShapeDtypeStruct((cap * WIDE_ROWS, LANES), f32),
        compiler_params=_params("arbitrary"),
        name="experts",
    )(blk_e, blk_valid, row_tok, hp, w_gate, w_up, w_down)


def _load_wide(y_ref, tm):
    return jnp.concatenate([y_ref[pl.ds(c, tm, stride=WIDE_ROWS), :] for c in range(WIDE_ROWS)], axis=1)


def _moe_ln_kernel(h_ref, y1_ref, y2_ref, info_ref, g_ref, beta_ref, of_ref, ob_ref):
    tm = h_ref.shape[0]
    info = info_ref[...]
    g1 = info[:, R_G1:R_G1 + 1]
    g2 = info[:, R_G2:R_G2 + 1]
    f = g1 * _load_wide(y1_ref, tm) + g2 * _load_wide(y2_ref, tm)
    out = _layer_norm(DN_ALPHA * h_ref[...] + f, g_ref[...], beta_ref[...])
    of_ref[...] = out
    ob_ref[...] = out.astype(bf16)


def _moe_ln(h, yc, info, g, beta, tm=256):
    n, d = h.shape
    nb = n // tm
    row = lambda i: (i, 0)
    fixed = lambda i: (0, 0)
    return pl.pallas_call(
        _moe_ln_kernel,
        grid=(nb,),
        in_specs=[pl.BlockSpec((tm, d), row), pl.BlockSpec((tm * WIDE_ROWS, LANES), row),
                  pl.BlockSpec((tm * WIDE_ROWS, LANES), lambda i: (i + nb, 0)), pl.BlockSpec((tm, LANES), row),
                  pl.BlockSpec((1, d), fixed), pl.BlockSpec((1, d), fixed)],
        out_specs=[pl.BlockSpec((tm, d), row), pl.BlockSpec((tm, d), row)],
        out_shape=[jax.ShapeDtypeStruct((n, d), f32), jax.ShapeDtypeStruct((n, d), bf16)],
        compiler_params=_params("parallel"),
        name="moe_ln",
    )(h, yc, yc, info, g, beta)


def _moe_layer(hf, hp, router_w, router_b, w_gate, w_up, w_down, g, beta):
    n, d = hf.shape
    info, cnt = _router(hf, router_w, router_b)
    experts = info[:, R_E1:R_E2 + 1].astype(jnp.int32)
    rank = info[:, R_RANK1:R_RANK2 + 1].astype(jnp.int32)
    counts = cnt[0, :N_EXPERTS].astype(jnp.int32)
    pcounts = ((counts + MOE_ROWS - 1) // MOE_ROWS) * MOE_ROWS
    pends = jnp.cumsum(pcounts)
    pstarts = pends - pcounts
    dest = pstarts[experts] + rank
    cap = 2 * n + N_EXPERTS * MOE_ROWS
    tok = jnp.repeat(jnp.arange(n, dtype=jnp.int32), 2)
    row_tok = (jnp.arange(cap, dtype=jnp.int32) % n).at[dest.reshape(-1)].set(tok, unique_indices=True)
    blk_start = jnp.arange(cap // MOE_ROWS, dtype=jnp.int32) * MOE_ROWS
    blk_e = jnp.sum((blk_start[:, None] >= pends[None, :]).astype(jnp.int32), axis=1)
    blk_e = jnp.minimum(blk_e, N_EXPERTS - 1)
    blk_valid = (blk_start < pends[-1]).astype(jnp.int32)
    last_e = jnp.max(jnp.where(counts > 0, jnp.arange(N_EXPERTS, dtype=jnp.int32), 0))
    blk_e = jnp.where(blk_valid != 0, blk_e, last_e)

    y = _experts(hp, row_tok, blk_e, blk_valid, w_gate, w_up, w_down)
    yc = _gather_tokens(y, dest.T.reshape(-1), WIDE_ROWS)
    return _moe_ln(hf, yc, info, g, beta)


def kernel(x, s5_w_in, s5_b_re, s5_b_im, s5_c_re, s5_c_im, s5_a_re, s5_a_im, s5_log_step, s5_d, s5_w_glu, s5_w_out, cv_w_pw1, cv_b_pw1, cv_w_dw, cv_b_dw, cv_ln_g, cv_ln_b, cv_w_pw2, cv_b_pw2, pl_w, pl_scale, router_w, router_b, moe_w_gate, moe_w_up, moe_w_down, ln_mix_g, ln_mix_b, ln_ffn_g, ln_ffn_b):
    bsz, seq, d = x.shape
    hf = x.reshape(bsz * seq, d)
    hb = hf.astype(bf16)
    row = lambda v: v.reshape(1, -1)

    def layer_bf16(w, layer):
        idx = lax.optimization_barrier(jnp.asarray(layer, jnp.int32))
        return lax.dynamic_index_in_dim(w, idx, axis=0, keepdims=False).astype(bf16)

    for i in range(DEPTH):
        mixer, j = i % N_MIXERS, i // N_MIXERS
        g_mix, b_mix = row(ln_mix_g[i]), row(ln_mix_b[i])
        if mixer == 0:
            u = _mm(hb, s5_w_in[j].astype(bf16))
            wagg, tmat, wout, dtab = _s5_derive(s5_b_re[j], s5_b_im[j], s5_c_re[j], s5_c_im[j],
                                                s5_a_re[j], s5_a_im[j], s5_log_step[j])
            z = _s5_core(u, wagg, tmat, wout, dtab, row(s5_d[j]))
            v = _mm_glu(z, s5_w_glu[j].astype(bf16), jnp.zeros((1, 2 * d), f32), bf16)
            hf, hb, hp = _mm_res_ln(v, s5_w_out[j].astype(bf16), jnp.zeros((1, d), f32), hf, g_mix, b_mix)
        elif mixer == 1:
            v = _mm_glu(hb, cv_w_pw1[j].astype(bf16), row(cv_b_pw1[j]), f32)
            cv = _conv_module(v, cv_w_dw[j], row(cv_b_dw[j]), row(cv_ln_g[j]), row(cv_ln_b[j]))
            hf, hb, hp = _mm_res_ln(cv, cv_w_pw2[j].astype(bf16), row(cv_b_pw2[j]), hf, g_mix, b_mix)
        else:
            hf, hb, hp = _pool_layer(hf, pl_w[j].astype(bf16), row(pl_scale[j]), g_mix, b_mix)
        hf, hb = _moe_layer(hf, hp, router_w, router_b, layer_bf16(moe_w_gate, i), layer_bf16(moe_w_up, i),
                            layer_bf16(moe_w_down, i), row(ln_ffn_g[i]), row(ln_ffn_b[i]))
    return hf.reshape(bsz, seq, d)
```

```python
import math

import jax
import jax.numpy as jnp
from jax import lax
from jax.experimental import pallas as pl
from jax.experimental.pallas import tpu as pltpu

f32 = jnp.float32
bf16 = jnp.bfloat16

D_MODEL = 2048
DEPTH = 4
N_MIXERS = 3
S5_GROUP = 16
S5_STATE = 64
CONV_WIDTH = 31
POOL_WINDOWS = (2, 4, 8, 16)
POOL_CH = D_MODEL // len(POOL_WINDOWS)
N_EXPERTS = 16
EXPERTS_PER_GROUP = 4
D_EXPERT = D_MODEL // 2
DN_ALPHA = (2 * DEPTH) ** 0.25
LN_EPS = 1e-5

LANES = 128
SUBLANES = 8
VMEM_LIMIT = 56 * 1024 * 1024
S5_BLOCK = 8
S5_CHUNK_GROUPS = LANES // S5_GROUP
MOE_ROWS = 256
GATHER_UNROLL = 8
PACK_ROWS = D_MODEL // 2 // LANES
WIDE_ROWS = D_MODEL // LANES


def _params(*sem):
    return pltpu.CompilerParams(dimension_semantics=sem, vmem_limit_bytes=VMEM_LIMIT)


def _layer_norm(r, g, b):
    mu = jnp.mean(r, axis=-1, keepdims=True)
    xc = r - mu
    var = jnp.mean(xc * xc, axis=-1, keepdims=True)
    return xc * lax.rsqrt(var + LN_EPS) * g + b


def _dot(a, b):
    return jnp.dot(a.astype(bf16), b.astype(bf16), preferred_element_type=f32)


def _store_packed(out, hp_ref):
    tm, d = out.shape
    for c in range(PACK_ROWS):
        lo = out[:, c * LANES:(c + 1) * LANES].astype(bf16).astype(f32)
        hi = out[:, d // 2 + c * LANES:d // 2 + (c + 1) * LANES].astype(bf16).astype(f32)
        word = (lax.bitcast_convert_type(lo, jnp.uint32) >> 16) | lax.bitcast_convert_type(hi, jnp.uint32)
        hp_ref[pl.ds(c, tm, stride=PACK_ROWS), :] = word


def _load_packed(hp_ref, x_scr):
    tm, d = x_scr.shape
    for c in range(PACK_ROWS):
        word = hp_ref[pl.ds(c, tm, stride=PACK_ROWS), :]
        lo = lax.bitcast_convert_type(word << 16, f32)
        hi = lax.bitcast_convert_type(word & jnp.uint32(0xFFFF0000), f32)
        x_scr[:, c * LANES:(c + 1) * LANES] = lo.astype(bf16)
        x_scr[:, d // 2 + c * LANES:d // 2 + (c + 1) * LANES] = hi.astype(bf16)


def _mm_kernel(x_ref, w_ref, o_ref):
    o_ref[...] = _dot(x_ref[...], w_ref[...]).astype(o_ref.dtype)


def _mm(x, w, tm=512, tn=1024):
    m, k = x.shape
    n = w.shape[1]
    return pl.pallas_call(
        _mm_kernel,
        grid=(n // tn, m // tm),
        in_specs=[pl.BlockSpec((tm, k), lambda j, i: (i, 0)),
                  pl.BlockSpec((k, tn), lambda j, i: (0, j))],
        out_specs=pl.BlockSpec((tm, tn), lambda j, i: (i, j)),
        out_shape=jax.ShapeDtypeStruct((m, n), f32),
        compiler_params=_params("parallel", "parallel"),
        name="mm",
    )(x, w)


def _mm_glu_kernel(x_ref, wa_ref, wg_ref, ba_ref, bg_ref, o_ref):
    x = x_ref[...].astype(bf16)
    a = _dot(x, wa_ref[...]) + ba_ref[...]
    g = _dot(x, wg_ref[...]) + bg_ref[...]
    o_ref[...] = (a * jax.nn.sigmoid(g)).astype(o_ref.dtype)


def _mm_glu(x, w, b, out_dtype, tm=512, tn=512):
    m, k = x.shape
    n = w.shape[1] // 2
    nb = n // tn
    return pl.pallas_call(
        _mm_glu_kernel,
        grid=(nb, m // tm),
        in_specs=[pl.BlockSpec((tm, k), lambda j, i: (i, 0)),
                  pl.BlockSpec((k, tn), lambda j, i: (0, j)),
                  pl.BlockSpec((k, tn), lambda j, i: (0, j + nb)),
                  pl.BlockSpec((1, tn), lambda j, i: (0, j)),
                  pl.BlockSpec((1, tn), lambda j, i: (0, j + nb))],
        out_specs=pl.BlockSpec((tm, tn), lambda j, i: (i, j)),
        out_shape=jax.ShapeDtypeStruct((m, n), out_dtype),
        compiler_params=_params("parallel", "parallel"),
        name="mm_glu",
    )(x, w, w, b, b)


def _mm_res_ln_kernel(x_ref, w_ref, b_ref, h_ref, g_ref, beta_ref, of_ref, ob_ref, hp_ref):
    y = _dot(x_ref[...], w_ref[...]) + b_ref[...]
    out = _layer_norm(DN_ALPHA * h_ref[...] + y, g_ref[...], beta_ref[...])
    of_ref[...] = out
    ob_ref[...] = out.astype(bf16)
    _store_packed(out, hp_ref)


def _mm_res_ln(x, w, b, h, g, beta, tm=256):
    m, k = x.shape
    d = w.shape[1]
    row = lambda i: (i, 0)
    fixed = lambda i: (0, 0)
    return pl.pallas_call(
        _mm_res_ln_kernel,
        grid=(m // tm,),
        in_specs=[pl.BlockSpec((tm, k), row), pl.BlockSpec((k, d), fixed), pl.BlockSpec((1, d), fixed),
                  pl.BlockSpec((tm, d), row), pl.BlockSpec((1, d), fixed), pl.BlockSpec((1, d), fixed)],
        out_specs=[pl.BlockSpec((tm, d), row), pl.BlockSpec((tm, d), row),
                   pl.BlockSpec((tm * PACK_ROWS, LANES), row)],
        out_shape=[jax.ShapeDtypeStruct((m, d), f32), jax.ShapeDtypeStruct((m, d), bf16),
                   jax.ShapeDtypeStruct((m * PACK_ROWS, LANES), jnp.uint32)],
        compiler_params=_params("parallel"),
        name="mm_res_ln",
    )(x, w, b, h, g, beta)


def _cmul_add(xr, xi, ar, ai, sr, si):
    return xr + ar * sr - ai * si, xi + ar * si + ai * sr


def _expand_block_diag(compact_ref, rep_ref, w_scr, row_shift, col_shift):
    width = w_scr.shape[0]
    step = 2 * LANES
    for c0 in range(0, width, step):
        w = jnp.dot(compact_ref[0], rep_ref[:, c0:c0 + step], preferred_element_type=f32)
        row_g = (lax.broadcasted_iota(jnp.int32, w.shape, 0) >> row_shift) & (S5_CHUNK_GROUPS - 1)
        col_g = ((lax.broadcasted_iota(jnp.int32, w.shape, 1) + c0) >> col_shift) & (S5_CHUNK_GROUPS - 1)
        w_scr[:, c0:c0 + step] = jnp.where(row_g == col_g, w, 0.0).astype(bf16)


def _s5_kernel(u_ref, agg_ref, toe_ref, proj_ref, rep_state_ref, rep_chan_ref, dtab_ref, d_ref, z_ref,
               xb_scr, v_scr, wagg_scr, t_scr, wout_scr):
    m_rows = v_scr.shape[0]
    half = v_scr.shape[1] // 2
    ncol = half // LANES
    chan_shift = S5_GROUP.bit_length() - 1
    state_shift = S5_STATE.bit_length() - 1
    _expand_block_diag(agg_ref, rep_state_ref, wagg_scr, chan_shift, state_shift)
    _expand_block_diag(toe_ref, rep_chan_ref, t_scr, chan_shift, chan_shift)
    _expand_block_diag(proj_ref, rep_chan_ref, wout_scr, state_shift, chan_shift)
    for i in range(S5_BLOCK):
        xb_scr[:, i * LANES:(i + 1) * LANES] = u_ref[pl.ds(i, m_rows, stride=S5_BLOCK), :].astype(bf16)
    xb = xb_scr[...]
    v_scr[...] = jnp.dot(xb, wagg_scr[...], preferred_element_type=f32)

    tab = dtab_ref[0]
    sub = lax.broadcasted_iota(jnp.int32, (SUBLANES, LANES), 0)

    def col(j, part):
        lo = part * half + j * LANES
        return slice(lo, lo + LANES)

    def bcast(row):
        return jnp.broadcast_to(row, (SUBLANES, LANES))

    def body(r, carry):
        r0 = pl.multiple_of(r * SUBLANES, SUBLANES)
        new = []
        for j in range(ncol):
            cr, ci = carry[2 * j], carry[2 * j + 1]
            xr = v_scr[pl.ds(r0, SUBLANES), col(j, 0)]
            xi = v_scr[pl.ds(r0, SUBLANES), col(j, 1)]
            for shift in (1, 2, 4):
                ar = bcast(tab[shift - 1:shift, col(j, 0)])
                ai = bcast(tab[shift - 1:shift, col(j, 1)])
                sr = jnp.where(sub >= shift, pltpu.roll(xr, shift, 0), 0.0)
                si = jnp.where(sub >= shift, pltpu.roll(xi, shift, 0), 0.0)
                xr, xi = _cmul_add(xr, xi, ar, ai, sr, si)
            xr, xi = _cmul_add(xr, xi, tab[:, col(j, 0)], tab[:, col(j, 1)], cr, ci)
            v_scr[pl.ds(r0, SUBLANES), col(j, 0)] = jnp.where(sub >= 1, pltpu.roll(xr, 1, 0), cr)
            v_scr[pl.ds(r0, SUBLANES), col(j, 1)] = jnp.where(sub >= 1, pltpu.roll(xi, 1, 0), ci)
            new.append(bcast(xr[SUBLANES - 1:SUBLANES, :]))
            new.append(bcast(xi[SUBLANES - 1:SUBLANES, :]))
        return tuple(new)

    zero = jnp.zeros((SUBLANES, LANES), f32)
    lax.fori_loop(0, m_rows // SUBLANES, body, (zero,) * (2 * ncol))

    sp = v_scr[...].astype(bf16)
    for i0 in range(0, S5_BLOCK, 2):
        cs = slice(i0 * LANES, (i0 + 2) * LANES)
        y = (jnp.dot(xb, t_scr[:, cs], preferred_element_type=f32)
             + jnp.dot(sp, wout_scr[:, cs], preferred_element_type=f32))
        for i in (i0, i0 + 1):
            yi = y[:, (i - i0) * LANES:(i - i0 + 1) * LANES] + d_ref[...] * u_ref[pl.ds(i, m_rows, stride=S5_BLOCK), :]
            z_ref[pl.ds(i, m_rows, stride=S5_BLOCK), :] = 0.5 * yi * (1.0 + lax.erf(yi * (1.0 / math.sqrt(2.0))))


def _s5_core(u, agg, toe, proj, dtab, d_skip):
    n, d = u.shape
    nq = d // LANES
    m_rows = n // S5_BLOCK
    width = agg.shape[1]
    col = jnp.arange(width)[None, :]
    lane = jnp.arange(LANES)[:, None]
    half = width // 2
    rep_state = ((col // half == lane // S5_STATE) & (col % S5_STATE == lane % S5_STATE)).astype(bf16)
    rep_chan = ((col // LANES == lane // S5_GROUP) & (col % S5_GROUP == lane % S5_GROUP)).astype(bf16)
    wspec = pl.BlockSpec((1, width, LANES), lambda q: (q, 0, 0))
    rspec = pl.BlockSpec((LANES, width), lambda q: (0, 0))
    return pl.pallas_call(
        _s5_kernel,
        grid=(nq,),
        in_specs=[pl.BlockSpec((n, LANES), lambda q: (0, q)), wspec, wspec, wspec, rspec, rspec,
                  pl.BlockSpec((1, SUBLANES, width), lambda q: (q, 0, 0)),
                  pl.BlockSpec((1, LANES), lambda q: (0, q))],
        out_specs=pl.BlockSpec((n, LANES), lambda q: (0, q)),
        out_shape=jax.ShapeDtypeStruct((n, d), f32),
        scratch_shapes=[pltpu.VMEM((m_rows, width), bf16), pltpu.VMEM((m_rows, width), f32),
                        pltpu.VMEM((width, width), bf16), pltpu.VMEM((width, width), bf16),
                        pltpu.VMEM((width, width), bf16)],
        compiler_params=_params("parallel"),
        name="s5_core",
    )(u, agg, toe, proj, rep_state, rep_chan, dtab, d_skip)


def _s5_derive(b_re, b_im, c_re, c_im, a_re, a_im, log_step):
    highest = lax.Precision.HIGHEST
    g, p, c = b_re.shape
    nq = g // S5_CHUNK_GROUPS
    dt = jnp.exp(log_step)[:, None]
    mag = jnp.exp(a_re * dt)
    lb_re = mag * jnp.cos(a_im * dt)
    lb_im = mag * jnp.sin(a_im * dt)
    den = a_re * a_re + a_im * a_im
    n_re = lb_re - 1.0
    n_im = lb_im
    f_re = (n_re * a_re + n_im * a_im) / den
    f_im = (n_im * a_re - n_re * a_im) / den
    bb_re = f_re[..., None] * b_re - f_im[..., None] * b_im
    bb_im = f_re[..., None] * b_im + f_im[..., None] * b_re

    def powers(br, bi, count):
        rs, is_ = [br], [bi]
        for _ in range(count - 1):
            rs.append(rs[-1] * br - is_[-1] * bi)
            is_.append(rs[-2] * bi + is_[-1] * br)
        return rs, is_

    pr, pi = powers(lb_re, lb_im, S5_BLOCK)
    lam_r = [jnp.ones_like(lb_re)] + pr
    lam_i = [jnp.zeros_like(lb_im)] + pi
    width = S5_BLOCK * LANES

    def compact(w, perm):
        return w.transpose(perm).reshape(nq, width, LANES).astype(bf16)

    ar = jnp.stack([lam_r[S5_BLOCK - 1 - i] for i in range(S5_BLOCK)])[..., None]
    ai = jnp.stack([lam_i[S5_BLOCK - 1 - i] for i in range(S5_BLOCK)])[..., None]
    ab = jnp.stack([ar * bb_re - ai * bb_im, ar * bb_im + ai * bb_re], axis=-1)
    ab = ab.reshape(S5_BLOCK, nq, S5_CHUNK_GROUPS, p, c, 2)
    wagg = compact(ab, (1, 0, 2, 4, 5, 3))

    ar = jnp.stack(lam_r[1:S5_BLOCK + 1])[:, :, None, :]
    ai = jnp.stack(lam_i[1:S5_BLOCK + 1])[:, :, None, :]
    cl = jnp.stack([c_re * ar - c_im * ai, -(c_re * ai + c_im * ar)], axis=-1)
    cl = cl.reshape(S5_BLOCK, nq, S5_CHUNK_GROUPS, c, p, 2)
    wout = compact(cl, (1, 5, 2, 4, 0, 3))

    ar = jnp.stack(lam_r[:S5_BLOCK])[:, :, None, :]
    ai = jnp.stack(lam_i[:S5_BLOCK])[:, :, None, :]
    kj = (jnp.einsum('jgcp,gpd->jgdc', c_re * ar - c_im * ai, bb_re, precision=highest)
          - jnp.einsum('jgcp,gpd->jgdc', c_re * ai + c_im * ar, bb_im, precision=highest))
    none = jnp.zeros_like(kj[0])
    kt = jnp.stack([jnp.stack([kj[i - a] if i >= a else none for i in range(S5_BLOCK)])
                    for a in range(S5_BLOCK)])
    kt = kt.reshape(S5_BLOCK, S5_BLOCK, nq, S5_CHUNK_GROUPS, c, c)
    tmat = compact(kt, (2, 0, 3, 4, 1, 5))

    dr, di = powers(lam_r[S5_BLOCK], lam_i[S5_BLOCK], SUBLANES)
    dtab = jnp.stack([jnp.stack(dr), jnp.stack(di)], axis=1)
    dtab = dtab.reshape(SUBLANES, 2, nq, S5_CHUNK_GROUPS * p).transpose(2, 0, 1, 3).reshape(nq, SUBLANES, width)
    return wagg, tmat, wout, dtab


def _conv_kernel(v_ref, halo_ref, w_ref, b_ref, g_ref, beta_ref, o_ref, ext_scr, rot_scr, acc_scr):
    tt = v_ref.shape[0]
    pad = halo_ref.shape[0]
    first = pl.program_id(0) == 0
    ext_scr[0:pad, :] = jnp.where(first, 0.0, halo_ref[...])
    ext_scr[pad:pad + tt, :] = v_ref[...]
    span = rot_scr.shape[1]
    for sh in range(1, SUBLANES):
        rot_scr[sh - 1] = ext_scr[sh:sh + span, :]
    rows, cols = 32, 256
    for c0 in range(0, v_ref.shape[1], cols):
        for r0 in range(0, tt, rows):
            acc = jnp.broadcast_to(b_ref[:, c0:c0 + cols], (rows, cols))
            for k in range(CONV_WIDTH):
                whole, sh = divmod(pad - (CONV_WIDTH - 1) + k, SUBLANES)
                off = whole * SUBLANES + r0
                if sh == 0:
                    win = ext_scr[off:off + rows, c0:c0 + cols]
                else:
                    win = rot_scr[sh - 1, off:off + rows, c0:c0 + cols]
                acc = acc + w_ref[k:k + 1, c0:c0 + cols] * win
            acc_scr[r0:r0 + rows, c0:c0 + cols] = acc
    y = _layer_norm(acc_scr[...], g_ref[...], beta_ref[...])
    o_ref[...] = (y * jax.nn.sigmoid(y)).astype(o_ref.dtype)


def _conv_module(v, w_dw, b_dw, g, beta, tt=128, pad=32):
    n, d = v.shape
    w_pad = jnp.zeros((pad, d), f32).at[:CONV_WIDTH].set(w_dw)
    ratio = tt // pad
    row = lambda i: (i, 0)
    fixed = lambda i: (0, 0)
    return pl.pallas_call(
        _conv_kernel,
        grid=(n // tt,),
        in_specs=[pl.BlockSpec((tt, d), row),
                  pl.BlockSpec((pad, d), lambda i: (jnp.maximum(i * ratio - 1, 0), 0)),
                  pl.BlockSpec((pad, d), fixed), pl.BlockSpec((1, d), fixed),
                  pl.BlockSpec((1, d), fixed), pl.BlockSpec((1, d), fixed)],
        out_specs=pl.BlockSpec((tt, d), row),
        out_shape=jax.ShapeDtypeStruct((n, d), bf16),
        scratch_shapes=[pltpu.VMEM((tt + pad, d), f32),
                        pltpu.VMEM((SUBLANES - 1, tt + pad - SUBLANES, d), f32),
                        pltpu.VMEM((tt, d), f32)],
        compiler_params=_params("parallel"),
        name="conv_module",
    )(v, v, w_pad, b_dw, g, beta)


def _pool_kernel(h_ref, halo_ref, w_ref, scale_ref, g_ref, beta_ref, of_ref, ob_ref, hp_ref, ext_scr):
    tt = h_ref.shape[0]
    pad = halo_ref.shape[0]
    i = pl.program_id(0)
    ext_scr[0:pad, :] = jnp.where(i == 0, 0.0, halo_ref[...])
    ext_scr[pad:pad + tt, :] = h_ref[...]
    t = i * tt + lax.broadcasted_iota(jnp.int32, (tt, 1), 0)
    ys = []
    for k, win in enumerate(POOL_WINDOWS):
        cs = slice(k * POOL_CH, (k + 1) * POOL_CH)
        x = h_ref[:, cs]
        acc = x
        for j in range(1, win):
            acc = acc + ext_scr[pad - j:pad - j + tt, cs]
        cnt = jnp.minimum(t + 1, win).astype(f32)
        ys.append(_dot(acc / cnt - x, w_ref[k]))
    y = jnp.concatenate(ys, axis=1) * scale_ref[...]
    out = _layer_norm(DN_ALPHA * h_ref[...] + y, g_ref[...], beta_ref[...])
    of_ref[...] = out
    ob_ref[...] = out.astype(bf16)
    _store_packed(out, hp_ref)


def _pool_layer(h, w_grp, scale, g, beta, tt=256, pad=16):
    n, d = h.shape
    ratio = tt // pad
    row = lambda i: (i, 0)
    fixed = lambda i: (0, 0)
    return pl.pallas_call(
        _pool_kernel,
        grid=(n // tt,),
        in_specs=[pl.BlockSpec((tt, d), row),
                  pl.BlockSpec((pad, d), lambda i: (jnp.maximum(i * ratio - 1, 0), 0)),
                  pl.BlockSpec(w_grp.shape, lambda i: (0, 0, 0)),
                  pl.BlockSpec((1, d), fixed), pl.BlockSpec((1, d), fixed), pl.BlockSpec((1, d), fixed)],
        out_specs=[pl.BlockSpec((tt, d), row), pl.BlockSpec((tt, d), row),
                   pl.BlockSpec((tt * PACK_ROWS, LANES), row)],
        out_shape=[jax.ShapeDtypeStruct((n, d), f32), jax.ShapeDtypeStruct((n, d), bf16),
                   jax.ShapeDtypeStruct((n * PACK_ROWS, LANES), jnp.uint32)],
        scratch_shapes=[pltpu.VMEM((tt + pad, d), f32)],
        compiler_params=_params("parallel"),
        name="pool_layer",
    )(h, h, w_grp, scale, g, beta)


R_E1, R_E2, R_G1, R_G2, R_RANK1, R_RANK2 = range(6)


def _router_kernel(h_ref, whi_ref, wlo_ref, b_ref, info_ref, cnt_ref):
    tm = h_ref.shape[0]

    @pl.when(pl.program_id(0) == 0)
    def _():
        cnt_ref[...] = jnp.zeros_like(cnt_ref)

    h = h_ref[...]
    h_hi = h.astype(bf16)
    h_lo = (h - h_hi.astype(f32)).astype(bf16)
    logits = (jnp.dot(h_hi, whi_ref[...], preferred_element_type=f32)
              + jnp.dot(h_lo, whi_ref[...], preferred_element_type=f32)
              + jnp.dot(h_hi, wlo_ref[...], preferred_element_type=f32)) + b_ref[...]
    lane = lax.broadcasted_iota(jnp.int32, logits.shape, 1)
    real = lane < N_EXPERTS
    e = jnp.exp(logits - jnp.max(logits, axis=-1, keepdims=True))
    probs = e / jnp.sum(e, axis=-1, keepdims=True)

    a = probs
    b = pltpu.roll(probs, 1, 1)
    c = pltpu.roll(probs, 2, 1)
    d = pltpu.roll(probs, 3, 1)
    hi1, lo1 = jnp.maximum(a, b), jnp.minimum(a, b)
    hi2, lo2 = jnp.maximum(c, d), jnp.minimum(c, d)
    score = jnp.maximum(hi1, hi2) + jnp.maximum(jnp.minimum(hi1, hi2), jnp.maximum(lo1, lo2))
    best = None
    g_sel = None
    for grp in range(N_EXPERTS // EXPERTS_PER_GROUP):
        last = grp * EXPERTS_PER_GROUP + EXPERTS_PER_GROUP - 1
        s = jnp.max(jnp.where(lane == last, score, -1.0), axis=-1, keepdims=True)
        if grp == 0:
            best, g_sel = s, jnp.zeros_like(s, dtype=jnp.int32)
        else:
            better = s > best
            best = jnp.where(better, s, best)
            g_sel = jnp.where(better, grp, g_sel)

    in_grp = real & ((lane // EXPERTS_PER_GROUP) == g_sel)
    masked = jnp.where(in_grp, probs, -1.0)
    lane_f = lane.astype(f32)
    p1 = jnp.max(masked, axis=-1, keepdims=True)
    e1 = jnp.min(jnp.where(masked == p1, lane_f, float(LANES)), axis=-1, keepdims=True)
    masked2 = jnp.where(lane_f == e1, -2.0, masked)
    p2 = jnp.max(masked2, axis=-1, keepdims=True)
    e2 = jnp.min(jnp.where(masked2 == p2, lane_f, float(LANES)), axis=-1, keepdims=True)
    tot = p1 + p2

    oh1 = (lane_f == e1).astype(f32)
    oh2 = (lane_f == e2).astype(f32)
    both = oh1 + oh2
    ri = lax.broadcasted_iota(jnp.int32, (tm, tm), 0)
    ci = lax.broadcasted_iota(jnp.int32, (tm, tm), 1)
    tri = (ci < ri).astype(bf16)
    before = cnt_ref[0:1, :] + jnp.dot(tri, both.astype(bf16), preferred_element_type=f32)
    rank1 = jnp.sum(before * oh1, axis=-1, keepdims=True)
    rank2 = jnp.sum(before * oh2, axis=-1, keepdims=True)
    cnt_ref[...] = cnt_ref[...] + jnp.sum(both, axis=0, keepdims=True)

    info = jnp.zeros(logits.shape, f32)
    for slot, val in ((R_E1, e1), (R_E2, e2), (R_G1, p1 / tot), (R_G2, p2 / tot),
                      (R_RANK1, rank1), (R_RANK2, rank2)):
        info = jnp.where(lane == slot, val, info)
    info_ref[...] = info


def _router(h, router_w, router_b, tm=512):
    n, d = h.shape
    w = jnp.zeros((d, LANES), f32).at[:, :N_EXPERTS].set(router_w)
    w_hi = w.astype(bf16)
    w_lo = (w - w_hi.astype(f32)).astype(bf16)
    b = jnp.full((1, LANES), -1e30, f32).at[0, :N_EXPERTS].set(router_b)
    return pl.pallas_call(
        _router_kernel,
        grid=(n // tm,),
        in_specs=[pl.BlockSpec((tm, d), lambda i: (i, 0)), pl.BlockSpec((d, LANES), lambda i: (0, 0)),
                  pl.BlockSpec((d, LANES), lambda i: (0, 0)), pl.BlockSpec((1, LANES), lambda i: (0, 0))],
        out_specs=[pl.BlockSpec((tm, LANES), lambda i: (i, 0)), pl.BlockSpec((SUBLANES, LANES), lambda i: (0, 0))],
        out_shape=[jax.ShapeDtypeStruct((n, LANES), f32), jax.ShapeDtypeStruct((SUBLANES, LANES), f32)],
        compiler_params=_params("arbitrary"),
        name="router",
    )(h, w_hi, w_lo, b)


def _token_copy(src_ref, dst_ref, sem, src_tok, dst_tok, rows):
    src = pl.multiple_of(src_tok * rows, SUBLANES)
    dst = pl.multiple_of(dst_tok * rows, SUBLANES)
    return pltpu.make_async_copy(src_ref.at[pl.ds(src, rows)], dst_ref.at[pl.ds(dst, rows)], sem)


def _expert_kernel(blk_e_ref, blk_valid_ref, blk_first_ref, blk_next_ref, row_tok_ref,
                   hp_ref, wg_hbm, wu_hbm, wd_hbm, o_ref,
                   xg_scr, x_scr, stage_g, stage_u, stage_d, wg_scr, wu_scr, wd_scr, sems, wsems):
    i = pl.program_id(0)
    tm = x_scr.shape[0]
    nblk = pl.num_programs(0)

    def weight_copies(e):
        return (pltpu.make_async_copy(wg_hbm.at[e], stage_g, wsems.at[0]),
                pltpu.make_async_copy(wu_hbm.at[e], stage_u, wsems.at[1]),
                pltpu.make_async_copy(wd_hbm.at[e], stage_d, wsems.at[2]))

    @pl.when(i == 0)
    def _():
        for cp in weight_copies(blk_e_ref[0]):
            cp.start()

    def token_copy(blk, j, tok):
        slot = blk % 2
        return _token_copy(hp_ref, xg_scr.at[slot], sems.at[slot], tok, j, PACK_ROWS)

    def start_gather(blk):
        def body(j0, carry):
            for k in range(GATHER_UNROLL):
                j = j0 * GATHER_UNROLL + k
                token_copy(blk, j, row_tok_ref[blk * tm + j]).start()
            return carry
        lax.fori_loop(0, tm // GATHER_UNROLL, body, 0)

    def wait_gather(blk):
        def body(j0, carry):
            for k in range(GATHER_UNROLL):
                token_copy(blk, 0, 0).wait()
            return carry
        lax.fori_loop(0, tm // GATHER_UNROLL, body, 0)

    @pl.when(i == 0)
    def _():
        start_gather(0)

    @pl.when((i + 1 < nblk) & (blk_valid_ref[jnp.minimum(i + 1, nblk - 1)] != 0))
    def _():
        start_gather(i + 1)

    @pl.when(blk_first_ref[i] != 0)
    def _():
        for cp in weight_copies(blk_e_ref[i]):
            cp.wait()
        chunk = 256
        for stage, dst in ((stage_g, wg_scr), (stage_u, wu_scr), (stage_d, wd_scr)):
            for r in range(0, stage.shape[0], chunk):
                dst[r:r + chunk, :] = stage[r:r + chunk, :].astype(bf16)
        nxt = blk_next_ref[i]

        @pl.when(nxt >= 0)
        def _():
            for cp in weight_copies(nxt):
                cp.start()

    @pl.when(blk_valid_ref[i] != 0)
    def _():
        wait_gather(i)
        _load_packed(xg_scr.at[i % 2], x_scr)
        x = x_scr[...]
        gate = _dot(x, wg_scr[...])
        hid = gate * jax.nn.sigmoid(gate) * _dot(x, wu_scr[...])
        y = _dot(hid, wd_scr[...])
        for c in range(WIDE_ROWS):
            o_ref[pl.ds(c, tm, stride=WIDE_ROWS), :] = y[:, c * LANES:(c + 1) * LANES]

    @pl.when(blk_valid_ref[i] == 0)
    def _():
        o_ref[...] = jnp.zeros_like(o_ref)


def _experts(hp, row_tok, blk_e, blk_valid, blk_first, blk_next, w_gate, w_up, w_down):
    cap = row_tok.shape[0]
    _, d, de = w_gate.shape
    row = lambda i, *_: (i, 0)
    any_spec = pl.BlockSpec(memory_space=pl.ANY)
    return pl.pallas_call(
        _expert_kernel,
        grid_spec=pltpu.PrefetchScalarGridSpec(
            num_scalar_prefetch=5, grid=(cap // MOE_ROWS,),
            in_specs=[any_spec, any_spec, any_spec, any_spec],
            out_specs=pl.BlockSpec((MOE_ROWS * WIDE_ROWS, LANES), row),
            scratch_shapes=[pltpu.VMEM((2, MOE_ROWS * PACK_ROWS, LANES), jnp.uint32),
                            pltpu.VMEM((MOE_ROWS, d), bf16),
                            pltpu.VMEM((d, de), f32), pltpu.VMEM((d, de), f32), pltpu.VMEM((de, d), f32),
                            pltpu.VMEM((d, de), bf16), pltpu.VMEM((d, de), bf16), pltpu.VMEM((de, d), bf16),
                            pltpu.SemaphoreType.DMA((2,)), pltpu.SemaphoreType.DMA((3,))]),
        out_shape=jax.ShapeDtypeStruct((cap * WIDE_ROWS, LANES), f32),
        compiler_params=_params("arbitrary"),
        name="experts",
    )(blk_e, blk_valid, blk_first, blk_next, row_tok, hp, w_gate, w_up, w_down)


def _load_wide(y_ref, tm):
    return jnp.concatenate([y_ref[pl.ds(c, tm, stride=WIDE_ROWS), :] for c in range(WIDE_ROWS)], axis=1)


def _moe_ln_kernel(dest_ref, h_ref, y_hbm, info_ref, g_ref, beta_ref, of_ref, ob_ref, y1_scr, y2_scr, sems):
    i = pl.program_id(0)
    ntile = pl.num_programs(0)
    tm = h_ref.shape[0]
    n_tok = ntile * tm

    def token_copy(tile, which, j, row):
        buf = tile % 2
        scr = (y1_scr, y2_scr)[which]
        return _token_copy(y_hbm, scr.at[buf], sems.at[buf], row, j, WIDE_ROWS)

    def start_gather(tile):
        def body(j0, carry):
            for which in range(2):
                for k in range(GATHER_UNROLL):
                    j = j0 * GATHER_UNROLL + k
                    token_copy(tile, which, j, dest_ref[which * n_tok + tile * tm + j]).start()
            return carry
        lax.fori_loop(0, tm // GATHER_UNROLL, body, 0)

    def wait_gather(tile):
        def body(j0, carry):
            for _ in range(2 * GATHER_UNROLL):
                token_copy(tile, 0, 0, 0).wait()
            return carry
        lax.fori_loop(0, tm // GATHER_UNROLL, body, 0)

    @pl.when(i == 0)
    def _():
        start_gather(0)

    @pl.when(i + 1 < ntile)
    def _():
        start_gather(i + 1)

    wait_gather(i)
    info = info_ref[...]
    g1 = info[:, R_G1:R_G1 + 1]
    g2 = info[:, R_G2:R_G2 + 1]
    f = g1 * _load_wide(y1_scr.at[i % 2], tm) + g2 * _load_wide(y2_scr.at[i % 2], tm)
    out = _layer_norm(DN_ALPHA * h_ref[...] + f, g_ref[...], beta_ref[...])
    of_ref[...] = out
    ob_ref[...] = out.astype(bf16)


def _moe_ln(h, y, dest, info, g, beta, tm=256):
    n, d = h.shape
    row = lambda i, *_: (i, 0)
    fixed = lambda i, *_: (0, 0)
    return pl.pallas_call(
        _moe_ln_kernel,
        grid_spec=pltpu.PrefetchScalarGridSpec(
            num_scalar_prefetch=1, grid=(n // tm,),
            in_specs=[pl.BlockSpec((tm, d), row), pl.BlockSpec(memory_space=pl.ANY),
                      pl.BlockSpec((tm, LANES), row), pl.BlockSpec((1, d), fixed), pl.BlockSpec((1, d), fixed)],
            out_specs=[pl.BlockSpec((tm, d), row), pl.BlockSpec((tm, d), row)],
            scratch_shapes=[pltpu.VMEM((2, tm * WIDE_ROWS, LANES), f32), pltpu.VMEM((2, tm * WIDE_ROWS, LANES), f32),
                            pltpu.SemaphoreType.DMA((2,))]),
        out_shape=[jax.ShapeDtypeStruct((n, d), f32), jax.ShapeDtypeStruct((n, d), bf16)],
        compiler_params=_params("arbitrary"),
        name="moe_ln",
    )(dest, h, y, info, g, beta)


def _moe_layer(hf, hp, router_w, router_b, w_gate, w_up, w_down, expert_base, g, beta):
    n, d = hf.shape
    info, cnt = _router(hf, router_w, router_b)
    experts = info[:, R_E1:R_E2 + 1].astype(jnp.int32)
    rank = info[:, R_RANK1:R_RANK2 + 1].astype(jnp.int32)
    counts = cnt[0, :N_EXPERTS].astype(jnp.int32)
    pcounts = ((counts + MOE_ROWS - 1) // MOE_ROWS) * MOE_ROWS
    pends = jnp.cumsum(pcounts)
    pstarts = pends - pcounts
    dest = pstarts[experts] + rank
    cap = 2 * n + N_EXPERTS * MOE_ROWS
    tok = jnp.repeat(jnp.arange(n, dtype=jnp.int32), 2)
    row_tok = (jnp.arange(cap, dtype=jnp.int32) % n).at[dest.reshape(-1)].set(tok, unique_indices=True)
    blk_start = jnp.arange(cap // MOE_ROWS, dtype=jnp.int32) * MOE_ROWS
    blk_e = jnp.sum((blk_start[:, None] >= pends[None, :]).astype(jnp.int32), axis=1)
    blk_e = jnp.minimum(blk_e, N_EXPERTS - 1)
    blk_valid = (blk_start < pends[-1]).astype(jnp.int32)
    last_e = jnp.max(jnp.where(counts > 0, jnp.arange(N_EXPERTS, dtype=jnp.int32), 0))
    blk_e = jnp.where(blk_valid != 0, blk_e, last_e)
    blk_first = blk_valid * jnp.concatenate([jnp.ones((1,), jnp.int32), (blk_e[1:] != blk_e[:-1]).astype(jnp.int32)])
    ids = jnp.arange(N_EXPERTS, dtype=jnp.int32)
    later = (ids[None, :] > ids[:, None]) & (counts[None, :] > 0)
    next_e = jnp.min(jnp.where(later, ids[None, :], N_EXPERTS), axis=1)
    next_e = jnp.where(next_e < N_EXPERTS, next_e + expert_base, -1).astype(jnp.int32)
    blk_next = jnp.sum(jnp.where(blk_e[:, None] == ids[None, :], next_e[None, :], 0), axis=1)

    y = _experts(hp, row_tok, blk_e + expert_base, blk_valid, blk_first, blk_next, w_gate, w_up, w_down)
    return _moe_ln(hf, y, dest.T.reshape(-1), info, g, beta)


def kernel(x, s5_w_in, s5_b_re, s5_b_im, s5_c_re, s5_c_im, s5_a_re, s5_a_im, s5_log_step, s5_d, s5_w_glu, s5_w_out, cv_w_pw1, cv_b_pw1, cv_w_dw, cv_b_dw, cv_ln_g, cv_ln_b, cv_w_pw2, cv_b_pw2, pl_w, pl_scale, router_w, router_b, moe_w_gate, moe_w_up, moe_w_down, ln_mix_g, ln_mix_b, ln_ffn_g, ln_ffn_b):
    bsz, seq, d = x.shape
    hf = x.reshape(bsz * seq, d)
    hb = hf.astype(bf16)
    row = lambda v: v.reshape(1, -1)
    w_gate = moe_w_gate.reshape((-1,) + moe_w_gate.shape[2:])
    w_up = moe_w_up.reshape((-1,) + moe_w_up.shape[2:])
    w_down = moe_w_down.reshape((-1,) + moe_w_down.shape[2:])

    for i in range(DEPTH):
        mixer, j = i % N_MIXERS, i // N_MIXERS
        g_mix, b_mix = row(ln_mix_g[i]), row(ln_mix_b[i])
        if mixer == 0:
            u = _mm(hb, s5_w_in[j].astype(bf16))
            wagg, tmat, wout, dtab = _s5_derive(s5_b_re[j], s5_b_im[j], s5_c_re[j], s5_c_im[j],
                                                s5_a_re[j], s5_a_im[j], s5_log_step[j])
            z = _s5_core(u, wagg, tmat, wout, dtab, row(s5_d[j]))
            v = _mm_glu(z, s5_w_glu[j].astype(bf16), jnp.zeros((1, 2 * d), f32), bf16)
            hf, hb, hp = _mm_res_ln(v, s5_w_out[j].astype(bf16), jnp.zeros((1, d), f32), hf, g_mix, b_mix)
        elif mixer == 1:
            v = _mm_glu(hb, cv_w_pw1[j].astype(bf16), row(cv_b_pw1[j]), f32)
            cv = _conv_module(v, cv_w_dw[j], row(cv_b_dw[j]), row(cv_ln_g[j]), row(cv_ln_b[j]))
            hf, hb, hp = _mm_res_ln(cv, cv_w_pw2[j].astype(bf16), row(cv_b_pw2[j]), hf, g_mix, b_mix)
        else:
            hf, hb, hp = _pool_layer(hf, pl_w[j].astype(bf16), row(pl_scale[j]), g_mix, b_mix)
        hf, hb = _moe_layer(hf, hp, router_w, router_b, w_gate, w_up, w_down, i * N_EXPERTS,
                            row(ln_ffn_g[i]), row(ln_ffn_b[i]))
    return hf.reshape(bsz, seq, d)
```

```python
import math

import jax
import jax.numpy as jnp
from jax import lax
from jax.experimental import pallas as pl
from jax.experimental.pallas import tpu as pltpu

f32 = jnp.float32
bf16 = jnp.bfloat16

D_MODEL = 2048
DEPTH = 4
N_MIXERS = 3
S5_GROUP = 16
S5_STATE = 64
CONV_WIDTH = 31
POOL_WINDOWS = (2, 4, 8, 16)
POOL_CH = D_MODEL // len(POOL_WINDOWS)
N_EXPERTS = 16
EXPERTS_PER_GROUP = 4
D_EXPERT = D_MODEL // 2
DN_ALPHA = (2 * DEPTH) ** 0.25
LN_EPS = 1e-5

LANES = 128
SUBLANES = 8
VMEM_LIMIT = 56 * 1024 * 1024
S5_BLOCK = 8
S5_CHUNK_GROUPS = LANES // S5_GROUP
MOE_ROWS = 256
GATHER_UNROLL = 8
PACK_ROWS = D_MODEL // 2 // LANES
WIDE_ROWS = D_MODEL // LANES


def _params(*sem):
    return pltpu.CompilerParams(dimension_semantics=sem, vmem_limit_bytes=VMEM_LIMIT)


def _layer_norm(r, g, b):
    mu = jnp.mean(r, axis=-1, keepdims=True)
    xc = r - mu
    var = jnp.mean(xc * xc, axis=-1, keepdims=True)
    return xc * lax.rsqrt(var + LN_EPS) * g + b


def _dot(a, b):
    return jnp.dot(a.astype(bf16), b.astype(bf16), preferred_element_type=f32)


def _store_packed(out, hp_ref):
    tm, d = out.shape
    for c in range(PACK_ROWS):
        lo = out[:, c * LANES:(c + 1) * LANES].astype(bf16).astype(f32)
        hi = out[:, d // 2 + c * LANES:d // 2 + (c + 1) * LANES].astype(bf16).astype(f32)
        word = (lax.bitcast_convert_type(lo, jnp.uint32) >> 16) | lax.bitcast_convert_type(hi, jnp.uint32)
        hp_ref[pl.ds(c, tm, stride=PACK_ROWS), :] = word


def _load_packed(hp_ref, x_scr):
    tm, d = x_scr.shape
    for c in range(PACK_ROWS):
        word = hp_ref[pl.ds(c, tm, stride=PACK_ROWS), :]
        lo = lax.bitcast_convert_type(word << 16, f32)
        hi = lax.bitcast_convert_type(word & jnp.uint32(0xFFFF0000), f32)
        x_scr[:, c * LANES:(c + 1) * LANES] = lo.astype(bf16)
        x_scr[:, d // 2 + c * LANES:d // 2 + (c + 1) * LANES] = hi.astype(bf16)


def _mm_kernel(x_ref, w_ref, o_ref):
    o_ref[...] = _dot(x_ref[...], w_ref[...]).astype(o_ref.dtype)


def _mm(x, w, tm=512, tn=1024):
    m, k = x.shape
    n = w.shape[1]
    return pl.pallas_call(
        _mm_kernel,
        grid=(n // tn, m // tm),
        in_specs=[pl.BlockSpec((tm, k), lambda j, i: (i, 0)),
                  pl.BlockSpec((k, tn), lambda j, i: (0, j))],
        out_specs=pl.BlockSpec((tm, tn), lambda j, i: (i, j)),
        out_shape=jax.ShapeDtypeStruct((m, n), f32),
        compiler_params=_params("parallel", "parallel"),
        name="mm",
    )(x, w)


def _mm_glu_kernel(x_ref, wa_ref, wg_ref, ba_ref, bg_ref, o_ref):
    x = x_ref[...].astype(bf16)
    a = _dot(x, wa_ref[...]) + ba_ref[...]
    g = _dot(x, wg_ref[...]) + bg_ref[...]
    o_ref[...] = (a * jax.nn.sigmoid(g)).astype(o_ref.dtype)


def _mm_glu(x, w, b, out_dtype, tm=512, tn=512):
    m, k = x.shape
    n = w.shape[1] // 2
    nb = n // tn
    return pl.pallas_call(
        _mm_glu_kernel,
        grid=(nb, m // tm),
        in_specs=[pl.BlockSpec((tm, k), lambda j, i: (i, 0)),
                  pl.BlockSpec((k, tn), lambda j, i: (0, j)),
                  pl.BlockSpec((k, tn), lambda j, i: (0, j + nb)),
                  pl.BlockSpec((1, tn), lambda j, i: (0, j)),
                  pl.BlockSpec((1, tn), lambda j, i: (0, j + nb))],
        out_specs=pl.BlockSpec((tm, tn), lambda j, i: (i, j)),
        out_shape=jax.ShapeDtypeStruct((m, n), out_dtype),
        compiler_params=_params("parallel", "parallel"),
        name="mm_glu",
    )(x, w, w, b, b)


def _mm_res_ln_kernel(x_ref, w_ref, b_ref, h_ref, g_ref, beta_ref, of_ref, ob_ref, hp_ref):
    y = _dot(x_ref[...], w_ref[...]) + b_ref[...]
    out = _layer_norm(DN_ALPHA * h_ref[...] + y, g_ref[...], beta_ref[...])
    of_ref[...] = out
    ob_ref[...] = out.astype(bf16)
    _store_packed(out, hp_ref)


def _mm_res_ln(x, w, b, h, g, beta, tm=256):
    m, k = x.shape
    d = w.shape[1]
    row = lambda i: (i, 0)
    fixed = lambda i: (0, 0)
    return pl.pallas_call(
        _mm_res_ln_kernel,
        grid=(m // tm,),
        in_specs=[pl.BlockSpec((tm, k), row), pl.BlockSpec((k, d), fixed), pl.BlockSpec((1, d), fixed),
                  pl.BlockSpec((tm, d), row), pl.BlockSpec((1, d), fixed), pl.BlockSpec((1, d), fixed)],
        out_specs=[pl.BlockSpec((tm, d), row), pl.BlockSpec((tm, d), row),
                   pl.BlockSpec((tm * PACK_ROWS, LANES), row)],
        out_shape=[jax.ShapeDtypeStruct((m, d), f32), jax.ShapeDtypeStruct((m, d), bf16),
                   jax.ShapeDtypeStruct((m * PACK_ROWS, LANES), jnp.uint32)],
        compiler_params=_params("parallel"),
        name="mm_res_ln",
    )(x, w, b, h, g, beta)


def _cmul_add(xr, xi, ar, ai, sr, si):
    return xr + ar * sr - ai * si, xi + ar * si + ai * sr


def _expand_block_diag(compact_ref, rep_ref, w_scr, row_shift, col_shift):
    width = w_scr.shape[0]
    step = 2 * LANES
    for c0 in range(0, width, step):
        w = jnp.dot(compact_ref[0], rep_ref[:, c0:c0 + step], preferred_element_type=f32)
        row_g = (lax.broadcasted_iota(jnp.int32, w.shape, 0) >> row_shift) & (S5_CHUNK_GROUPS - 1)
        col_g = ((lax.broadcasted_iota(jnp.int32, w.shape, 1) + c0) >> col_shift) & (S5_CHUNK_GROUPS - 1)
        w_scr[:, c0:c0 + step] = jnp.where(row_g == col_g, w, 0.0).astype(bf16)


def _s5_kernel(u_ref, agg_ref, toe_ref, proj_ref, rep_state_ref, rep_chan_ref, dtab_ref, d_ref, z_ref,
               xb_scr, v_scr, wagg_scr, t_scr, wout_scr):
    m_rows = v_scr.shape[0]
    half = v_scr.shape[1] // 2
    ncol = half // LANES
    chan_shift = S5_GROUP.bit_length() - 1
    state_shift = S5_STATE.bit_length() - 1
    _expand_block_diag(agg_ref, rep_state_ref, wagg_scr, chan_shift, state_shift)
    _expand_block_diag(toe_ref, rep_chan_ref, t_scr, chan_shift, chan_shift)
    _expand_block_diag(proj_ref, rep_state_ref, wout_scr, chan_shift, state_shift)
    for i in range(S5_BLOCK):
        xb_scr[:, i * LANES:(i + 1) * LANES] = u_ref[pl.ds(i, m_rows, stride=S5_BLOCK), :].astype(bf16)
    xb = xb_scr[...]
    v_scr[...] = jnp.dot(xb, wagg_scr[...], preferred_element_type=f32)

    tab = dtab_ref[0]
    sub = lax.broadcasted_iota(jnp.int32, (SUBLANES, LANES), 0)

    def col(j, part):
        lo = part * half + j * LANES
        return slice(lo, lo + LANES)

    def bcast(row):
        return jnp.broadcast_to(row, (SUBLANES, LANES))

    def body(r, carry):
        r0 = pl.multiple_of(r * SUBLANES, SUBLANES)
        new = []
        for j in range(ncol):
            cr, ci = carry[2 * j], carry[2 * j + 1]
            xr = v_scr[pl.ds(r0, SUBLANES), col(j, 0)]
            xi = v_scr[pl.ds(r0, SUBLANES), col(j, 1)]
            for shift in (1, 2, 4):
                ar = bcast(tab[shift - 1:shift, col(j, 0)])
                ai = bcast(tab[shift - 1:shift, col(j, 1)])
                sr = jnp.where(sub >= shift, pltpu.roll(xr, shift, 0), 0.0)
                si = jnp.where(sub >= shift, pltpu.roll(xi, shift, 0), 0.0)
                xr, xi = _cmul_add(xr, xi, ar, ai, sr, si)
            xr, xi = _cmul_add(xr, xi, tab[:, col(j, 0)], tab[:, col(j, 1)], cr, ci)
            v_scr[pl.ds(r0, SUBLANES), col(j, 0)] = jnp.where(sub >= 1, pltpu.roll(xr, 1, 0), cr)
            v_scr[pl.ds(r0, SUBLANES), col(j, 1)] = jnp.where(sub >= 1, pltpu.roll(xi, 1, 0), ci)
            new.append(bcast(xr[SUBLANES - 1:SUBLANES, :]))
            new.append(bcast(xi[SUBLANES - 1:SUBLANES, :]))
        return tuple(new)

    zero = jnp.zeros((SUBLANES, LANES), f32)
    lax.fori_loop(0, m_rows // SUBLANES, body, (zero,) * (2 * ncol))

    sp = v_scr[...].astype(bf16)
    for i0 in range(0, S5_BLOCK, 2):
        cs = slice(i0 * LANES, (i0 + 2) * LANES)
        nt = (((1,), (1,)), ((), ()))
        y = (lax.dot_general(xb, t_scr[cs, :], nt, preferred_element_type=f32)
             + lax.dot_general(sp, wout_scr[cs, :], nt, preferred_element_type=f32))
        for i in (i0, i0 + 1):
            yi = y[:, (i - i0) * LANES:(i - i0 + 1) * LANES] + d_ref[...] * u_ref[pl.ds(i, m_rows, stride=S5_BLOCK), :]
            z_ref[pl.ds(i, m_rows, stride=S5_BLOCK), :] = 0.5 * yi * (1.0 + lax.erf(yi * (1.0 / math.sqrt(2.0))))


def _s5_core(u, agg, toe, proj, dtab, d_skip):
    n, d = u.shape
    nq = d // LANES
    m_rows = n // S5_BLOCK
    width = agg.shape[1]
    col = jnp.arange(width)[None, :]
    lane = jnp.arange(LANES)[:, None]
    half = width // 2
    rep_state = ((col // half == lane // S5_STATE) & (col % S5_STATE == lane % S5_STATE)).astype(bf16)
    rep_chan = ((col // LANES == lane // S5_GROUP) & (col % S5_GROUP == lane % S5_GROUP)).astype(bf16)
    wspec = pl.BlockSpec((1, width, LANES), lambda q: (q, 0, 0))
    rspec = pl.BlockSpec((LANES, width), lambda q: (0, 0))
    return pl.pallas_call(
        _s5_kernel,
        grid=(nq,),
        in_specs=[pl.BlockSpec((n, LANES), lambda q: (0, q)), wspec, wspec, wspec, rspec, rspec,
                  pl.BlockSpec((1, SUBLANES, width), lambda q: (q, 0, 0)),
                  pl.BlockSpec((1, LANES), lambda q: (0, q))],
        out_specs=pl.BlockSpec((n, LANES), lambda q: (0, q)),
        out_shape=jax.ShapeDtypeStruct((n, d), f32),
        scratch_shapes=[pltpu.VMEM((m_rows, width), bf16), pltpu.VMEM((m_rows, width), f32),
                        pltpu.VMEM((width, width), bf16), pltpu.VMEM((width, width), bf16),
                        pltpu.VMEM((width, width), bf16)],
        compiler_params=_params("parallel"),
        name="s5_core",
    )(u, agg, toe, proj, rep_state, rep_chan, dtab, d_skip)


def _s5_derive(b_re, b_im, c_re, c_im, a_re, a_im, log_step):
    highest = lax.Precision.HIGHEST
    g, p, c = b_re.shape
    nq = g // S5_CHUNK_GROUPS
    dt = jnp.exp(log_step)[:, None]
    mag = jnp.exp(a_re * dt)
    lb_re = mag * jnp.cos(a_im * dt)
    lb_im = mag * jnp.sin(a_im * dt)
    den = a_re * a_re + a_im * a_im
    n_re = lb_re - 1.0
    n_im = lb_im
    f_re = (n_re * a_re + n_im * a_im) / den
    f_im = (n_im * a_re - n_re * a_im) / den
    bb_re = f_re[..., None] * b_re - f_im[..., None] * b_im
    bb_im = f_re[..., None] * b_im + f_im[..., None] * b_re

    def powers(br, bi, count):
        rs, is_ = [br], [bi]
        for _ in range(count - 1):
            rs.append(rs[-1] * br - is_[-1] * bi)
            is_.append(rs[-2] * bi + is_[-1] * br)
        return rs, is_

    pr, pi = powers(lb_re, lb_im, S5_BLOCK)
    lam_r = [jnp.ones_like(lb_re)] + pr
    lam_i = [jnp.zeros_like(lb_im)] + pi
    width = S5_BLOCK * LANES

    def compact(w):
        w = w.reshape(S5_BLOCK, nq, S5_CHUNK_GROUPS * c, LANES)
        return w.transpose(1, 0, 2, 3).reshape(nq, width, LANES).astype(bf16)

    bt_re = bb_re.transpose(0, 2, 1)
    bt_im = bb_im.transpose(0, 2, 1)

    ar = jnp.stack([lam_r[S5_BLOCK - 1 - i] for i in range(S5_BLOCK)])[:, :, None, :]
    ai = jnp.stack([lam_i[S5_BLOCK - 1 - i] for i in range(S5_BLOCK)])[:, :, None, :]
    wagg = compact(jnp.concatenate([ar * bt_re - ai * bt_im, ar * bt_im + ai * bt_re], axis=-1))

    ar = jnp.stack(lam_r[1:S5_BLOCK + 1])[:, :, None, :]
    ai = jnp.stack(lam_i[1:S5_BLOCK + 1])[:, :, None, :]
    wout = compact(jnp.concatenate([c_re * ar - c_im * ai, -(c_re * ai + c_im * ar)], axis=-1))

    ar = jnp.stack(lam_r[:S5_BLOCK])[:, :, None, :]
    ai = jnp.stack(lam_i[:S5_BLOCK])[:, :, None, :]
    kj = (jnp.einsum('jgcp,gpd->jgcd', c_re * ar - c_im * ai, bb_re, precision=highest)
          - jnp.einsum('jgcp,gpd->jgcd', c_re * ai + c_im * ar, bb_im, precision=highest))
    none = jnp.zeros_like(kj[0])
    tmat = compact(jnp.stack([jnp.concatenate([kj[i - a] if i >= a else none for a in range(S5_BLOCK)], axis=-1)
                              for i in range(S5_BLOCK)]))

    dr, di = powers(lam_r[S5_BLOCK], lam_i[S5_BLOCK], SUBLANES)
    dtab = jnp.stack([jnp.stack(dr), jnp.stack(di)], axis=1)
    dtab = dtab.reshape(SUBLANES, 2, nq, S5_CHUNK_GROUPS * p).transpose(2, 0, 1, 3).reshape(nq, SUBLANES, width)
    return wagg, tmat, wout, dtab


def _conv_kernel(v_ref, halo_ref, w_ref, b_ref, g_ref, beta_ref, o_ref, ext_scr, rot_scr, acc_scr):
    tt = v_ref.shape[0]
    pad = halo_ref.shape[0]
    first = pl.program_id(0) == 0
    ext_scr[0:pad, :] = jnp.where(first, 0.0, halo_ref[...])
    ext_scr[pad:pad + tt, :] = v_ref[...]
    span = rot_scr.shape[1]
    for sh in range(1, SUBLANES):
        rot_scr[sh - 1] = ext_scr[sh:sh + span, :]
    rows, cols = 32, 256
    for c0 in range(0, v_ref.shape[1], cols):
        for r0 in range(0, tt, rows):
            acc = jnp.broadcast_to(b_ref[:, c0:c0 + cols], (rows, cols))
            for k in range(CONV_WIDTH):
                whole, sh = divmod(pad - (CONV_WIDTH - 1) + k, SUBLANES)
                off = whole * SUBLANES + r0
                if sh == 0:
                    win = ext_scr[off:off + rows, c0:c0 + cols]
                else:
                    win = rot_scr[sh - 1, off:off + rows, c0:c0 + cols]
                acc = acc + w_ref[k:k + 1, c0:c0 + cols] * win
            acc_scr[r0:r0 + rows, c0:c0 + cols] = acc
    y = _layer_norm(acc_scr[...], g_ref[...], beta_ref[...])
    o_ref[...] = (y * jax.nn.sigmoid(y)).astype(o_ref.dtype)


def _conv_module(v, w_dw, b_dw, g, beta, tt=128, pad=32):
    n, d = v.shape
    w_pad = jnp.zeros((pad, d), f32).at[:CONV_WIDTH].set(w_dw)
    ratio = tt // pad
    row = lambda i: (i, 0)
    fixed = lambda i: (0, 0)
    return pl.pallas_call(
        _conv_kernel,
        grid=(n // tt,),
        in_specs=[pl.BlockSpec((tt, d), row),
                  pl.BlockSpec((pad, d), lambda i: (jnp.maximum(i * ratio - 1, 0), 0)),
                  pl.BlockSpec((pad, d), fixed), pl.BlockSpec((1, d), fixed),
                  pl.BlockSpec((1, d), fixed), pl.BlockSpec((1, d), fixed)],
        out_specs=pl.BlockSpec((tt, d), row),
        out_shape=jax.ShapeDtypeStruct((n, d), bf16),
        scratch_shapes=[pltpu.VMEM((tt + pad, d), f32),
                        pltpu.VMEM((SUBLANES - 1, tt + pad - SUBLANES, d), f32),
                        pltpu.VMEM((tt, d), f32)],
        compiler_params=_params("parallel"),
        name="conv_module",
    )(v, v, w_pad, b_dw, g, beta)


def _pool_kernel(h_ref, halo_ref, w_ref, scale_ref, g_ref, beta_ref, of_ref, ob_ref, hp_ref, ext_scr):
    tt = h_ref.shape[0]
    pad = halo_ref.shape[0]
    i = pl.program_id(0)
    ext_scr[0:pad, :] = jnp.where(i == 0, 0.0, halo_ref[...])
    ext_scr[pad:pad + tt, :] = h_ref[...]
    t = i * tt + lax.broadcasted_iota(jnp.int32, (tt, 1), 0)
    ys = []
    for k, win in enumerate(POOL_WINDOWS):
        cs = slice(k * POOL_CH, (k + 1) * POOL_CH)
        x = h_ref[:, cs]
        acc = x
        for j in range(1, win):
            acc = acc + ext_scr[pad - j:pad - j + tt, cs]
        cnt = jnp.minimum(t + 1, win).astype(f32)
        ys.append(_dot(acc / cnt - x, w_ref[k]))
    y = jnp.concatenate(ys, axis=1) * scale_ref[...]
    out = _layer_norm(DN_ALPHA * h_ref[...] + y, g_ref[...], beta_ref[...])
    of_ref[...] = out
    ob_ref[...] = out.astype(bf16)
    _store_packed(out, hp_ref)


def _pool_layer(h, w_grp, scale, g, beta, tt=256, pad=16):
    n, d = h.shape
    ratio = tt // pad
    row = lambda i: (i, 0)
    fixed = lambda i: (0, 0)
    return pl.pallas_call(
        _pool_kernel,
        grid=(n // tt,),
        in_specs=[pl.BlockSpec((tt, d), row),
                  pl.BlockSpec((pad, d), lambda i: (jnp.maximum(i * ratio - 1, 0), 0)),
                  pl.BlockSpec(w_grp.shape, lambda i: (0, 0, 0)),
                  pl.BlockSpec((1, d), fixed), pl.BlockSpec((1, d), fixed), pl.BlockSpec((1, d), fixed)],
        out_specs=[pl.BlockSpec((tt, d), row), pl.BlockSpec((tt, d), row),
                   pl.BlockSpec((tt * PACK_ROWS, LANES), row)],
        out_shape=[jax.ShapeDtypeStruct((n, d), f32), jax.ShapeDtypeStruct((n, d), bf16),
                   jax.ShapeDtypeStruct((n * PACK_ROWS, LANES), jnp.uint32)],
        scratch_shapes=[pltpu.VMEM((tt + pad, d), f32)],
        compiler_params=_params("parallel"),
        name="pool_layer",
    )(h, h, w_grp, scale, g, beta)


R_E1, R_E2, R_G1, R_G2, R_RANK1, R_RANK2 = range(6)


def _router_kernel(h_ref, whi_ref, wlo_ref, b_ref, info_ref, cnt_ref):
    tm = h_ref.shape[0]

    @pl.when(pl.program_id(0) == 0)
    def _():
        cnt_ref[...] = jnp.zeros_like(cnt_ref)

    h = h_ref[...]
    h_hi = h.astype(bf16)
    h_lo = (h - h_hi.astype(f32)).astype(bf16)
    logits = (jnp.dot(h_hi, whi_ref[...], preferred_element_type=f32)
              + jnp.dot(h_lo, whi_ref[...], preferred_element_type=f32)
              + jnp.dot(h_hi, wlo_ref[...], preferred_element_type=f32)) + b_ref[...]
    lane = lax.broadcasted_iota(jnp.int32, logits.shape, 1)
    real = lane < N_EXPERTS
    e = jnp.exp(logits - jnp.max(logits, axis=-1, keepdims=True))
    probs = e / jnp.sum(e, axis=-1, keepdims=True)

    a = probs
    b = pltpu.roll(probs, 1, 1)
    c = pltpu.roll(probs, 2, 1)
    d = pltpu.roll(probs, 3, 1)
    hi1, lo1 = jnp.maximum(a, b), jnp.minimum(a, b)
    hi2, lo2 = jnp.maximum(c, d), jnp.minimum(c, d)
    score = jnp.maximum(hi1, hi2) + jnp.maximum(jnp.minimum(hi1, hi2), jnp.maximum(lo1, lo2))
    best = None
    g_sel = None
    for grp in range(N_EXPERTS // EXPERTS_PER_GROUP):
        last = grp * EXPERTS_PER_GROUP + EXPERTS_PER_GROUP - 1
        s = jnp.max(jnp.where(lane == last, score, -1.0), axis=-1, keepdims=True)
        if grp == 0:
            best, g_sel = s, jnp.zeros_like(s, dtype=jnp.int32)
        else:
            better = s > best
            best = jnp.where(better, s, best)
            g_sel = jnp.where(better, grp, g_sel)

    in_grp = real & ((lane // EXPERTS_PER_GROUP) == g_sel)
    masked = jnp.where(in_grp, probs, -1.0)
    lane_f = lane.astype(f32)
    p1 = jnp.max(masked, axis=-1, keepdims=True)
    e1 = jnp.min(jnp.where(masked == p1, lane_f, float(LANES)), axis=-1, keepdims=True)
    masked2 = jnp.where(lane_f == e1, -2.0, masked)
    p2 = jnp.max(masked2, axis=-1, keepdims=True)
    e2 = jnp.min(jnp.where(masked2 == p2, lane_f, float(LANES)), axis=-1, keepdims=True)
    tot = p1 + p2

    oh1 = (lane_f == e1).astype(f32)
    oh2 = (lane_f == e2).astype(f32)
    both = oh1 + oh2
    ri = lax.broadcasted_iota(jnp.int32, (tm, tm), 0)
    ci = lax.broadcasted_iota(jnp.int32, (tm, tm), 1)
    tri = (ci < ri).astype(bf16)
    before = cnt_ref[0:1, :] + jnp.dot(tri, both.astype(bf16), preferred_element_type=f32)
    rank1 = jnp.sum(before * oh1, axis=-1, keepdims=True)
    rank2 = jnp.sum(before * oh2, axis=-1, keepdims=True)
    cnt_ref[...] = cnt_ref[...] + jnp.sum(both, axis=0, keepdims=True)

    info = jnp.zeros(logits.shape, f32)
    for slot, val in ((R_E1, e1), (R_E2, e2), (R_G1, p1 / tot), (R_G2, p2 / tot),
                      (R_RANK1, rank1), (R_RANK2, rank2)):
        info = jnp.where(lane == slot, val, info)
    info_ref[...] = info


def _router(h, router_w, router_b, tm=512):
    n, d = h.shape
    w = jnp.zeros((d, LANES), f32).at[:, :N_EXPERTS].set(router_w)
    w_hi = w.astype(bf16)
    w_lo = (w - w_hi.astype(f32)).astype(bf16)
    b = jnp.full((1, LANES), -1e30, f32).at[0, :N_EXPERTS].set(router_b)
    return pl.pallas_call(
        _router_kernel,
        grid=(n // tm,),
        in_specs=[pl.BlockSpec((tm, d), lambda i: (i, 0)), pl.BlockSpec((d, LANES), lambda i: (0, 0)),
                  pl.BlockSpec((d, LANES), lambda i: (0, 0)), pl.BlockSpec((1, LANES), lambda i: (0, 0))],
        out_specs=[pl.BlockSpec((tm, LANES), lambda i: (i, 0)), pl.BlockSpec((SUBLANES, LANES), lambda i: (0, 0))],
        out_shape=[jax.ShapeDtypeStruct((n, LANES), f32), jax.ShapeDtypeStruct((SUBLANES, LANES), f32)],
        compiler_params=_params("arbitrary"),
        name="router",
    )(h, w_hi, w_lo, b)


def _token_copy(src_ref, dst_ref, sem, src_tok, dst_tok, rows):
    src = pl.multiple_of(src_tok * rows, SUBLANES)
    dst = pl.multiple_of(dst_tok * rows, SUBLANES)
    return pltpu.make_async_copy(src_ref.at[pl.ds(src, rows)], dst_ref.at[pl.ds(dst, rows)], sem)


def _expert_kernel(blk_e_ref, blk_valid_ref, blk_first_ref, blk_next_ref, row_tok_ref,
                   hp_ref, wg_hbm, wu_hbm, wd_hbm, o_ref,
                   xg_scr, x_scr, stage_g, stage_u, stage_d, wg_scr, wu_scr, wd_scr, sems, wsems):
    i = pl.program_id(0)
    tm = x_scr.shape[0]
    nblk = pl.num_programs(0)

    def weight_copies(e):
        return (pltpu.make_async_copy(wg_hbm.at[e], stage_g, wsems.at[0]),
                pltpu.make_async_copy(wu_hbm.at[e], stage_u, wsems.at[1]),
                pltpu.make_async_copy(wd_hbm.at[e], stage_d, wsems.at[2]))

    @pl.when(i == 0)
    def _():
        for cp in weight_copies(blk_e_ref[0]):
            cp.start()

    def token_copy(slot, j, tok):
        return _token_copy(hp_ref, xg_scr.at[slot], sems.at[slot], tok, j, PACK_ROWS)

    def wait_gather(slot):
        def body(j0, carry):
            for k in range(GATHER_UNROLL):
                token_copy(slot, 0, 0).wait()
            return carry
        lax.fori_loop(0, tm // GATHER_UNROLL, body, 0)

    @pl.when(i == 0)
    def _():
        def body(j0, carry):
            for k in range(GATHER_UNROLL):
                j = j0 * GATHER_UNROLL + k
                token_copy(0, j, row_tok_ref[j]).start()
            return carry
        lax.fori_loop(0, tm // GATHER_UNROLL, body, 0)

    @pl.when(blk_first_ref[i] != 0)
    def _():
        for cp in weight_copies(blk_e_ref[i]):
            cp.wait()
        chunk = 256
        for stage, dst in ((stage_g, wg_scr), (stage_u, wu_scr), (stage_d, wd_scr)):
            for r in range(0, stage.shape[0], chunk):
                dst[r:r + chunk, :] = stage[r:r + chunk, :].astype(bf16)
        nxt = blk_next_ref[i]

        @pl.when(nxt >= 0)
        def _():
            for cp in weight_copies(nxt):
                cp.start()

    this_slot = i % 2
    next_slot = (i + 1) % 2

    @pl.when(blk_valid_ref[i] != 0)
    def _():
        wait_gather(this_slot)
        _load_packed(xg_scr.at[this_slot], x_scr)
        nxt_blk = jnp.minimum(i + 1, nblk - 1)
        for j in range(tm):
            token_copy(next_slot, j, row_tok_ref[nxt_blk * tm + j]).start()
        x = x_scr[...]
        gate = _dot(x, wg_scr[...])
        hid = gate * jax.nn.sigmoid(gate) * _dot(x, wu_scr[...])
        y = _dot(hid, wd_scr[...])
        for c in range(WIDE_ROWS):
            o_ref[pl.ds(c, tm, stride=WIDE_ROWS), :] = y[:, c * LANES:(c + 1) * LANES]

        @pl.when(i == nblk - 1)
        def _():
            wait_gather(next_slot)

    @pl.when(blk_valid_ref[i] == 0)
    def _():
        o_ref[...] = jnp.zeros_like(o_ref)

        @pl.when(blk_valid_ref[jnp.maximum(i - 1, 0)] != 0)
        def _():
            wait_gather(this_slot)


def _experts(hp, row_tok, blk_e, blk_valid, blk_first, blk_next, w_gate, w_up, w_down):
    cap = row_tok.shape[0]
    _, d, de = w_gate.shape
    row = lambda i, *_: (i, 0)
    any_spec = pl.BlockSpec(memory_space=pl.ANY)
    return pl.pallas_call(
        _expert_kernel,
        grid_spec=pltpu.PrefetchScalarGridSpec(
            num_scalar_prefetch=5, grid=(cap // MOE_ROWS,),
            in_specs=[any_spec, any_spec, any_spec, any_spec],
            out_specs=pl.BlockSpec((MOE_ROWS * WIDE_ROWS, LANES), row),
            scratch_shapes=[pltpu.VMEM((2, MOE_ROWS * PACK_ROWS, LANES), jnp.uint32),
                            pltpu.VMEM((MOE_ROWS, d), bf16),
                            pltpu.VMEM((d, de), f32), pltpu.VMEM((d, de), f32), pltpu.VMEM((de, d), f32),
                            pltpu.VMEM((d, de), bf16), pltpu.VMEM((d, de), bf16), pltpu.VMEM((de, d), bf16),
                            pltpu.SemaphoreType.DMA((2,)), pltpu.SemaphoreType.DMA((3,))]),
        out_shape=jax.ShapeDtypeStruct((cap * WIDE_ROWS, LANES), f32),
        compiler_params=_params("arbitrary"),
        name="experts",
    )(blk_e, blk_valid, blk_first, blk_next, row_tok, hp, w_gate, w_up, w_down)


def _load_wide(y_ref, tm):
    return jnp.concatenate([y_ref[pl.ds(c, tm, stride=WIDE_ROWS), :] for c in range(WIDE_ROWS)], axis=1)


def _moe_ln_kernel(dest_ref, h_ref, y_hbm, info_ref, g_ref, beta_ref, of_ref, ob_ref, y1_scr, y2_scr, sems):
    i = pl.program_id(0)
    ntile = pl.num_programs(0)
    tm = h_ref.shape[0]
    n_tok = ntile * tm

    def token_copy(buf, which, j, row):
        scr = (y1_scr, y2_scr)[which]
        return _token_copy(y_hbm, scr.at[buf], sems.at[buf], row, j, WIDE_ROWS)

    def wait_gather(buf):
        def body(j0, carry):
            for _ in range(2 * GATHER_UNROLL):
                token_copy(buf, 0, 0, 0).wait()
            return carry
        lax.fori_loop(0, tm // GATHER_UNROLL, body, 0)

    @pl.when(i == 0)
    def _():
        def body(j0, carry):
            for which in range(2):
                for k in range(GATHER_UNROLL):
                    j = j0 * GATHER_UNROLL + k
                    token_copy(0, which, j, dest_ref[which * n_tok + j]).start()
            return carry
        lax.fori_loop(0, tm // GATHER_UNROLL, body, 0)

    this_buf = i % 2
    next_buf = (i + 1) % 2
    wait_gather(this_buf)
    info = info_ref[...]
    g1 = info[:, R_G1:R_G1 + 1]
    g2 = info[:, R_G2:R_G2 + 1]
    f = g1 * _load_wide(y1_scr.at[this_buf], tm) + g2 * _load_wide(y2_scr.at[this_buf], tm)
    nxt_tile = jnp.minimum(i + 1, ntile - 1)
    for j in range(tm):
        for which in range(2):
            token_copy(next_buf, which, j, dest_ref[which * n_tok + nxt_tile * tm + j]).start()
    out = _layer_norm(DN_ALPHA * h_ref[...] + f, g_ref[...], beta_ref[...])
    of_ref[...] = out
    ob_ref[...] = out.astype(bf16)

    @pl.when(i == ntile - 1)
    def _():
        wait_gather(next_buf)


def _moe_ln(h, y, dest, info, g, beta, tm=256):
    n, d = h.shape
    row = lambda i, *_: (i, 0)
    fixed = lambda i, *_: (0, 0)
    return pl.pallas_call(
        _moe_ln_kernel,
        grid_spec=pltpu.PrefetchScalarGridSpec(
            num_scalar_prefetch=1, grid=(n // tm,),
            in_specs=[pl.BlockSpec((tm, d), row), pl.BlockSpec(memory_space=pl.ANY),
                      pl.BlockSpec((tm, LANES), row), pl.BlockSpec((1, d), fixed), pl.BlockSpec((1, d), fixed)],
            out_specs=[pl.BlockSpec((tm, d), row), pl.BlockSpec((tm, d), row)],
            scratch_shapes=[pltpu.VMEM((2, tm * WIDE_ROWS, LANES), f32), pltpu.VMEM((2, tm * WIDE_ROWS, LANES), f32),
                            pltpu.SemaphoreType.DMA((2,))]),
        out_shape=[jax.ShapeDtypeStruct((n, d), f32), jax.ShapeDtypeStruct((n, d), bf16)],
        compiler_params=_params("arbitrary"),
        name="moe_ln",
    )(dest, h, y, info, g, beta)


def _moe_layer(hf, hp, router_w, router_b, w_gate, w_up, w_down, expert_base, g, beta):
    n, d = hf.shape
    info, cnt = _router(hf, router_w, router_b)
    experts = info[:, R_E1:R_E2 + 1].astype(jnp.int32)
    rank = info[:, R_RANK1:R_RANK2 + 1].astype(jnp.int32)
    counts = cnt[0, :N_EXPERTS].astype(jnp.int32)
    pcounts = ((counts + MOE_ROWS - 1) // MOE_ROWS) * MOE_ROWS
    pends = jnp.cumsum(pcounts)
    pstarts = pends - pcounts
    dest = pstarts[experts] + rank
    cap = 2 * n + N_EXPERTS * MOE_ROWS
    tok = jnp.repeat(jnp.arange(n, dtype=jnp.int32), 2)
    row_tok = (jnp.arange(cap, dtype=jnp.int32) % n).at[dest.reshape(-1)].set(tok, unique_indices=True)
    blk_start = jnp.arange(cap // MOE_ROWS, dtype=jnp.int32) * MOE_ROWS
    blk_e = jnp.sum((blk_start[:, None] >= pends[None, :]).astype(jnp.int32), axis=1)
    blk_e = jnp.minimum(blk_e, N_EXPERTS - 1)
    blk_valid = (blk_start < pends[-1]).astype(jnp.int32)
    last_e = jnp.max(jnp.where(counts > 0, jnp.arange(N_EXPERTS, dtype=jnp.int32), 0))
    blk_e = jnp.where(blk_valid != 0, blk_e, last_e)
    blk_first = blk_valid * jnp.concatenate([jnp.ones((1,), jnp.int32), (blk_e[1:] != blk_e[:-1]).astype(jnp.int32)])
    ids = jnp.arange(N_EXPERTS, dtype=jnp.int32)
    later = (ids[None, :] > ids[:, None]) & (counts[None, :] > 0)
    next_e = jnp.min(jnp.where(later, ids[None, :], N_EXPERTS), axis=1)
    next_e = jnp.where(next_e < N_EXPERTS, next_e + expert_base, -1).astype(jnp.int32)
    blk_next = jnp.sum(jnp.where(blk_e[:, None] == ids[None, :], next_e[None, :], 0), axis=1)

    y = _experts(hp, row_tok, blk_e + expert_base, blk_valid, blk_first, blk_next, w_gate, w_up, w_down)
    return _moe_ln(hf, y, dest.T.reshape(-1), info, g, beta)


def kernel(x, s5_w_in, s5_b_re, s5_b_im, s5_c_re, s5_c_im, s5_a_re, s5_a_im, s5_log_step, s5_d, s5_w_glu, s5_w_out, cv_w_pw1, cv_b_pw1, cv_w_dw, cv_b_dw, cv_ln_g, cv_ln_b, cv_w_pw2, cv_b_pw2, pl_w, pl_scale, router_w, router_b, moe_w_gate, moe_w_up, moe_w_down, ln_mix_g, ln_mix_b, ln_ffn_g, ln_ffn_b):
    bsz, seq, d = x.shape
    hf = x.reshape(bsz * seq, d)
    hb = hf.astype(bf16)
    row = lambda v: v.reshape(1, -1)
    w_gate = moe_w_gate.reshape((-1,) + moe_w_gate.shape[2:])
    w_up = moe_w_up.reshape((-1,) + moe_w_up.shape[2:])
    w_down = moe_w_down.reshape((-1,) + moe_w_down.shape[2:])

    for i in range(DEPTH):
        mixer, j = i % N_MIXERS, i // N_MIXERS
        g_mix, b_mix = row(ln_mix_g[i]), row(ln_mix_b[i])
        if mixer == 0:
            u = _mm(hb, s5_w_in[j].astype(bf16))
            wagg, tmat, wout, dtab = _s5_derive(s5_b_re[j], s5_b_im[j], s5_c_re[j], s5_c_im[j],
                                                s5_a_re[j], s5_a_im[j], s5_log_step[j])
            z = _s5_core(u, wagg, tmat, wout, dtab, row(s5_d[j]))
            v = _mm_glu(z, s5_w_glu[j].astype(bf16), jnp.zeros((1, 2 * d), f32), bf16)
            hf, hb, hp = _mm_res_ln(v, s5_w_out[j].astype(bf16), jnp.zeros((1, d), f32), hf, g_mix, b_mix)
        elif mixer == 1:
            v = _mm_glu(hb, cv_w_pw1[j].astype(bf16), row(cv_b_pw1[j]), f32)
            cv = _conv_module(v, cv_w_dw[j], row(cv_b_dw[j]), row(cv_ln_g[j]), row(cv_ln_b[j]))
            hf, hb, hp = _mm_res_ln(cv, cv_w_pw2[j].astype(bf16), row(cv_b_pw2[j]), hf, g_mix, b_mix)
        else:
            hf, hb, hp = _pool_layer(hf, pl_w[j].astype(bf16), row(pl_scale[j]), g_mix, b_mix)
        hf, hb = _moe_layer(hf, hp, router_w, router_b, w_gate, w_up, w_down, i * N_EXPERTS,
                            row(ln_ffn_g[i]), row(ln_ffn_b[i]))
    return hf.reshape(bsz, seq, d)
```

```python
import math

import jax
import jax.numpy as jnp
from jax import lax
from jax.experimental import pallas as pl
from jax.experimental.pallas import tpu as pltpu

f32 = jnp.float32
bf16 = jnp.bfloat16

D_MODEL = 2048
DEPTH = 4
N_MIXERS = 3
S5_GROUP = 16
S5_STATE = 64
CONV_WIDTH = 31
POOL_WINDOWS = (2, 4, 8, 16)
POOL_CH = D_MODEL // len(POOL_WINDOWS)
N_EXPERTS = 16
EXPERTS_PER_GROUP = 4
D_EXPERT = D_MODEL // 2
DN_ALPHA = (2 * DEPTH) ** 0.25
LN_EPS = 1e-5

LANES = 128
SUBLANES = 8
VMEM_LIMIT = 56 * 1024 * 1024
S5_BLOCK = 8
S5_CHUNK_GROUPS = LANES // S5_GROUP
MOE_ROWS = 256
GATHER_UNROLL = 8
PACK_ROWS = D_MODEL // 2 // LANES
WIDE_ROWS = D_MODEL // LANES


def _params(*sem):
    return pltpu.CompilerParams(dimension_semantics=sem, vmem_limit_bytes=VMEM_LIMIT)


def _layer_norm(r, g, b):
    mu = jnp.mean(r, axis=-1, keepdims=True)
    xc = r - mu
    var = jnp.mean(xc * xc, axis=-1, keepdims=True)
    return xc * lax.rsqrt(var + LN_EPS) * g + b


def _dot(a, b):
    return jnp.dot(a.astype(bf16), b.astype(bf16), preferred_element_type=f32)


def _store_packed(out, hp_ref):
    tm, d = out.shape
    for c in range(PACK_ROWS):
        lo = out[:, c * LANES:(c + 1) * LANES].astype(bf16).astype(f32)
        hi = out[:, d // 2 + c * LANES:d // 2 + (c + 1) * LANES].astype(bf16).astype(f32)
        word = (lax.bitcast_convert_type(lo, jnp.uint32) >> 16) | lax.bitcast_convert_type(hi, jnp.uint32)
        hp_ref[pl.ds(c, tm, stride=PACK_ROWS), :] = word


def _load_packed(hp_ref, x_scr):
    tm, d = x_scr.shape
    for c in range(PACK_ROWS):
        word = hp_ref[pl.ds(c, tm, stride=PACK_ROWS), :]
        lo = lax.bitcast_convert_type(word << 16, f32)
        hi = lax.bitcast_convert_type(word & jnp.uint32(0xFFFF0000), f32)
        x_scr[:, c * LANES:(c + 1) * LANES] = lo.astype(bf16)
        x_scr[:, d // 2 + c * LANES:d // 2 + (c + 1) * LANES] = hi.astype(bf16)


def _mm_kernel(x_ref, w_ref, o_ref):
    o_ref[...] = _dot(x_ref[...], w_ref[...]).astype(o_ref.dtype)


def _mm(x, w, tm=512, tn=1024):
    m, k = x.shape
    n = w.shape[1]
    return pl.pallas_call(
        _mm_kernel,
        grid=(n // tn, m // tm),
        in_specs=[pl.BlockSpec((tm, k), lambda j, i: (i, 0)),
                  pl.BlockSpec((k, tn), lambda j, i: (0, j))],
        out_specs=pl.BlockSpec((tm, tn), lambda j, i: (i, j)),
        out_shape=jax.ShapeDtypeStruct((m, n), f32),
        compiler_params=_params("parallel", "parallel"),
        name="mm",
    )(x, w)


def _mm_glu_kernel(x_ref, wa_ref, wg_ref, ba_ref, bg_ref, o_ref):
    x = x_ref[...].astype(bf16)
    a = _dot(x, wa_ref[...]) + ba_ref[...]
    g = _dot(x, wg_ref[...]) + bg_ref[...]
    o_ref[...] = (a * jax.nn.sigmoid(g)).astype(o_ref.dtype)


def _mm_glu(x, w, b, out_dtype, tm=512, tn=512):
    m, k = x.shape
    n = w.shape[1] // 2
    nb = n // tn
    return pl.pallas_call(
        _mm_glu_kernel,
        grid=(nb, m // tm),
        in_specs=[pl.BlockSpec((tm, k), lambda j, i: (i, 0)),
                  pl.BlockSpec((k, tn), lambda j, i: (0, j)),
                  pl.BlockSpec((k, tn), lambda j, i: (0, j + nb)),
                  pl.BlockSpec((1, tn), lambda j, i: (0, j)),
                  pl.BlockSpec((1, tn), lambda j, i: (0, j + nb))],
        out_specs=pl.BlockSpec((tm, tn), lambda j, i: (i, j)),
        out_shape=jax.ShapeDtypeStruct((m, n), out_dtype),
        compiler_params=_params("parallel", "parallel"),
        name="mm_glu",
    )(x, w, w, b, b)


def _mm_res_ln_kernel(x_ref, w_ref, b_ref, h_ref, g_ref, beta_ref, of_ref, ob_ref, hp_ref):
    y = _dot(x_ref[...], w_ref[...]) + b_ref[...]
    out = _layer_norm(DN_ALPHA * h_ref[...] + y, g_ref[...], beta_ref[...])
    of_ref[...] = out
    ob_ref[...] = out.astype(bf16)
    _store_packed(out, hp_ref)


def _mm_res_ln(x, w, b, h, g, beta, tm=512):
    m, k = x.shape
    d = w.shape[1]
    row = lambda i: (i, 0)
    fixed = lambda i: (0, 0)
    return pl.pallas_call(
        _mm_res_ln_kernel,
        grid=(m // tm,),
        in_specs=[pl.BlockSpec((tm, k), row), pl.BlockSpec((k, d), fixed, pipeline_mode=pl.Buffered(1)),
                  pl.BlockSpec((1, d), fixed),
                  pl.BlockSpec((tm, d), row), pl.BlockSpec((1, d), fixed), pl.BlockSpec((1, d), fixed)],
        out_specs=[pl.BlockSpec((tm, d), row), pl.BlockSpec((tm, d), row),
                   pl.BlockSpec((tm * PACK_ROWS, LANES), row)],
        out_shape=[jax.ShapeDtypeStruct((m, d), f32), jax.ShapeDtypeStruct((m, d), bf16),
                   jax.ShapeDtypeStruct((m * PACK_ROWS, LANES), jnp.uint32)],
        compiler_params=_params("parallel"),
        name="mm_res_ln",
    )(x, w, b, h, g, beta)


def _cmul_add(xr, xi, ar, ai, sr, si):
    return xr + ar * sr - ai * si, xi + ar * si + ai * sr


def _expand_block_diag(compact_ref, rep_ref, w_scr, row_shift, col_shift):
    width = w_scr.shape[0]
    step = 2 * LANES
    for c0 in range(0, width, step):
        w = jnp.dot(compact_ref[0], rep_ref[:, c0:c0 + step], preferred_element_type=f32)
        row_g = (lax.broadcasted_iota(jnp.int32, w.shape, 0) >> row_shift) & (S5_CHUNK_GROUPS - 1)
        col_g = ((lax.broadcasted_iota(jnp.int32, w.shape, 1) + c0) >> col_shift) & (S5_CHUNK_GROUPS - 1)
        w_scr[:, c0:c0 + step] = jnp.where(row_g == col_g, w, 0.0).astype(bf16)


def _s5_kernel(u_ref, agg_ref, toe_ref, proj_ref, rep_state_ref, rep_chan_ref, dtab_ref, d_ref, z_ref,
               xb_scr, v_scr, wagg_scr, t_scr, wout_scr):
    m_rows = v_scr.shape[0]
    half = v_scr.shape[1] // 2
    ncol = half // LANES
    chan_shift = S5_GROUP.bit_length() - 1
    state_shift = S5_STATE.bit_length() - 1
    _expand_block_diag(agg_ref, rep_state_ref, wagg_scr, chan_shift, state_shift)
    _expand_block_diag(toe_ref, rep_chan_ref, t_scr, chan_shift, chan_shift)
    _expand_block_diag(proj_ref, rep_state_ref, wout_scr, chan_shift, state_shift)
    for i in range(S5_BLOCK):
        xb_scr[:, i * LANES:(i + 1) * LANES] = u_ref[pl.ds(i, m_rows, stride=S5_BLOCK), :].astype(bf16)
    xb = xb_scr[...]
    v_scr[...] = jnp.dot(xb, wagg_scr[...], preferred_element_type=f32)

    tab = dtab_ref[0]
    sub = lax.broadcasted_iota(jnp.int32, (SUBLANES, LANES), 0)

    def col(j, part):
        lo = part * half + j * LANES
        return slice(lo, lo + LANES)

    def bcast(row):
        return jnp.broadcast_to(row, (SUBLANES, LANES))

    def body(r, carry):
        r0 = pl.multiple_of(r * SUBLANES, SUBLANES)
        new = []
        for j in range(ncol):
            cr, ci = carry[2 * j], carry[2 * j + 1]
            xr = v_scr[pl.ds(r0, SUBLANES), col(j, 0)]
            xi = v_scr[pl.ds(r0, SUBLANES), col(j, 1)]
            for shift in (1, 2, 4):
                ar = bcast(tab[shift - 1:shift, col(j, 0)])
                ai = bcast(tab[shift - 1:shift, col(j, 1)])
                sr = jnp.where(sub >= shift, pltpu.roll(xr, shift, 0), 0.0)
                si = jnp.where(sub >= shift, pltpu.roll(xi, shift, 0), 0.0)
                xr, xi = _cmul_add(xr, xi, ar, ai, sr, si)
            xr, xi = _cmul_add(xr, xi, tab[:, col(j, 0)], tab[:, col(j, 1)], cr, ci)
            v_scr[pl.ds(r0, SUBLANES), col(j, 0)] = jnp.where(sub >= 1, pltpu.roll(xr, 1, 0), cr)
            v_scr[pl.ds(r0, SUBLANES), col(j, 1)] = jnp.where(sub >= 1, pltpu.roll(xi, 1, 0), ci)
            new.append(bcast(xr[SUBLANES - 1:SUBLANES, :]))
            new.append(bcast(xi[SUBLANES - 1:SUBLANES, :]))
        return tuple(new)

    zero = jnp.zeros((SUBLANES, LANES), f32)
    lax.fori_loop(0, m_rows // SUBLANES, body, (zero,) * (2 * ncol))

    sp = v_scr[...].astype(bf16)
    for i0 in range(0, S5_BLOCK, 2):
        cs = slice(i0 * LANES, (i0 + 2) * LANES)
        nt = (((1,), (1,)), ((), ()))
        y = (lax.dot_general(xb, t_scr[cs, :], nt, preferred_element_type=f32)
             + lax.dot_general(sp, wout_scr[cs, :], nt, preferred_element_type=f32))
        for i in (i0, i0 + 1):
            yi = y[:, (i - i0) * LANES:(i - i0 + 1) * LANES] + d_ref[...] * u_ref[pl.ds(i, m_rows, stride=S5_BLOCK), :]
            z_ref[pl.ds(i, m_rows, stride=S5_BLOCK), :] = 0.5 * yi * (1.0 + lax.erf(yi * (1.0 / math.sqrt(2.0))))


def _s5_core(u, agg, toe, proj, dtab, d_skip):
    n, d = u.shape
    nq = d // LANES
    m_rows = n // S5_BLOCK
    width = agg.shape[1]
    col = jnp.arange(width)[None, :]
    lane = jnp.arange(LANES)[:, None]
    half = width // 2
    rep_state = ((col // half == lane // S5_STATE) & (col % S5_STATE == lane % S5_STATE)).astype(bf16)
    rep_chan = ((col // LANES == lane // S5_GROUP) & (col % S5_GROUP == lane % S5_GROUP)).astype(bf16)
    wspec = pl.BlockSpec((1, width, LANES), lambda q: (q, 0, 0))
    rspec = pl.BlockSpec((LANES, width), lambda q: (0, 0))
    return pl.pallas_call(
        _s5_kernel,
        grid=(nq,),
        in_specs=[pl.BlockSpec((n, LANES), lambda q: (0, q)), wspec, wspec, wspec, rspec, rspec,
                  pl.BlockSpec((1, SUBLANES, width), lambda q: (q, 0, 0)),
                  pl.BlockSpec((1, LANES), lambda q: (0, q))],
        out_specs=pl.BlockSpec((n, LANES), lambda q: (0, q)),
        out_shape=jax.ShapeDtypeStruct((n, d), f32),
        scratch_shapes=[pltpu.VMEM((m_rows, width), bf16), pltpu.VMEM((m_rows, width), f32),
                        pltpu.VMEM((width, width), bf16), pltpu.VMEM((width, width), bf16),
                        pltpu.VMEM((width, width), bf16)],
        compiler_params=_params("parallel"),
        name="s5_core",
    )(u, agg, toe, proj, rep_state, rep_chan, dtab, d_skip)


def _s5_derive(b_re, b_im, c_re, c_im, a_re, a_im, log_step):
    highest = lax.Precision.HIGHEST
    g, p, c = b_re.shape
    nq = g // S5_CHUNK_GROUPS
    dt = jnp.exp(log_step)[:, None]
    mag = jnp.exp(a_re * dt)
    lb_re = mag * jnp.cos(a_im * dt)
    lb_im = mag * jnp.sin(a_im * dt)
    den = a_re * a_re + a_im * a_im
    n_re = lb_re - 1.0
    n_im = lb_im
    f_re = (n_re * a_re + n_im * a_im) / den
    f_im = (n_im * a_re - n_re * a_im) / den
    bb_re = f_re[..., None] * b_re - f_im[..., None] * b_im
    bb_im = f_re[..., None] * b_im + f_im[..., None] * b_re

    def powers(br, bi, count):
        rs, is_ = [br], [bi]
        for _ in range(count - 1):
            rs.append(rs[-1] * br - is_[-1] * bi)
            is_.append(rs[-2] * bi + is_[-1] * br)
        return rs, is_

    pr, pi = powers(lb_re, lb_im, S5_BLOCK)
    lam_r = [jnp.ones_like(lb_re)] + pr
    lam_i = [jnp.zeros_like(lb_im)] + pi
    width = S5_BLOCK * LANES

    def compact(w):
        w = w.reshape(S5_BLOCK, nq, S5_CHUNK_GROUPS * c, LANES)
        return w.transpose(1, 0, 2, 3).reshape(nq, width, LANES).astype(bf16)

    bt_re = bb_re.transpose(0, 2, 1)
    bt_im = bb_im.transpose(0, 2, 1)

    ar = jnp.stack([lam_r[S5_BLOCK - 1 - i] for i in range(S5_BLOCK)])[:, :, None, :]
    ai = jnp.stack([lam_i[S5_BLOCK - 1 - i] for i in range(S5_BLOCK)])[:, :, None, :]
    wagg = compact(jnp.concatenate([ar * bt_re - ai * bt_im, ar * bt_im + ai * bt_re], axis=-1))

    ar = jnp.stack(lam_r[1:S5_BLOCK + 1])[:, :, None, :]
    ai = jnp.stack(lam_i[1:S5_BLOCK + 1])[:, :, None, :]
    wout = compact(jnp.concatenate([c_re * ar - c_im * ai, -(c_re * ai + c_im * ar)], axis=-1))

    ar = jnp.stack(lam_r[:S5_BLOCK])[:, :, None, :]
    ai = jnp.stack(lam_i[:S5_BLOCK])[:, :, None, :]
    kj = (jnp.einsum('jgcp,gpd->jgcd', c_re * ar - c_im * ai, bb_re, precision=highest)
          - jnp.einsum('jgcp,gpd->jgcd', c_re * ai + c_im * ar, bb_im, precision=highest))
    none = jnp.zeros_like(kj[0])
    tmat = compact(jnp.stack([jnp.concatenate([kj[i - a] if i >= a else none for a in range(S5_BLOCK)], axis=-1)
                              for i in range(S5_BLOCK)]))

    dr, di = powers(lam_r[S5_BLOCK], lam_i[S5_BLOCK], SUBLANES)
    dtab = jnp.stack([jnp.stack(dr), jnp.stack(di)], axis=1)
    dtab = dtab.reshape(SUBLANES, 2, nq, S5_CHUNK_GROUPS * p).transpose(2, 0, 1, 3).reshape(nq, SUBLANES, width)
    return wagg, tmat, wout, dtab


def _conv_kernel(v_ref, halo_ref, w_ref, b_ref, g_ref, beta_ref, o_ref, ext_scr, rot_scr, acc_scr):
    tt = v_ref.shape[0]
    pad = halo_ref.shape[0]
    first = pl.program_id(0) == 0
    ext_scr[0:pad, :] = jnp.where(first, 0.0, halo_ref[...])
    ext_scr[pad:pad + tt, :] = v_ref[...]
    span = rot_scr.shape[1]
    for sh in range(1, SUBLANES):
        rot_scr[sh - 1] = ext_scr[sh:sh + span, :]
    rows, cols = 32, 256
    for c0 in range(0, v_ref.shape[1], cols):
        for r0 in range(0, tt, rows):
            acc = jnp.broadcast_to(b_ref[:, c0:c0 + cols], (rows, cols))
            for k in range(CONV_WIDTH):
                whole, sh = divmod(pad - (CONV_WIDTH - 1) + k, SUBLANES)
                off = whole * SUBLANES + r0
                if sh == 0:
                    win = ext_scr[off:off + rows, c0:c0 + cols]
                else:
                    win = rot_scr[sh - 1, off:off + rows, c0:c0 + cols]
                acc = acc + w_ref[k:k + 1, c0:c0 + cols] * win
            acc_scr[r0:r0 + rows, c0:c0 + cols] = acc
    y = _layer_norm(acc_scr[...], g_ref[...], beta_ref[...])
    o_ref[...] = (y * jax.nn.sigmoid(y)).astype(o_ref.dtype)


def _conv_module(v, w_dw, b_dw, g, beta, tt=128, pad=32):
    n, d = v.shape
    w_pad = jnp.zeros((pad, d), f32).at[:CONV_WIDTH].set(w_dw)
    ratio = tt // pad
    row = lambda i: (i, 0)
    fixed = lambda i: (0, 0)
    return pl.pallas_call(
        _conv_kernel,
        grid=(n // tt,),
        in_specs=[pl.BlockSpec((tt, d), row),
                  pl.BlockSpec((pad, d), lambda i: (jnp.maximum(i * ratio - 1, 0), 0)),
                  pl.BlockSpec((pad, d), fixed), pl.BlockSpec((1, d), fixed),
                  pl.BlockSpec((1, d), fixed), pl.BlockSpec((1, d), fixed)],
        out_specs=pl.BlockSpec((tt, d), row),
        out_shape=jax.ShapeDtypeStruct((n, d), bf16),
        scratch_shapes=[pltpu.VMEM((tt + pad, d), f32),
                        pltpu.VMEM((SUBLANES - 1, tt + pad - SUBLANES, d), f32),
                        pltpu.VMEM((tt, d), f32)],
        compiler_params=_params("parallel"),
        name="conv_module",
    )(v, v, w_pad, b_dw, g, beta)


def _pool_kernel(h_ref, halo_ref, w_ref, scale_ref, g_ref, beta_ref, of_ref, ob_ref, hp_ref, ext_scr):
    tt = h_ref.shape[0]
    pad = halo_ref.shape[0]
    i = pl.program_id(0)
    ext_scr[0:pad, :] = jnp.where(i == 0, 0.0, halo_ref[...])
    ext_scr[pad:pad + tt, :] = h_ref[...]
    t = i * tt + lax.broadcasted_iota(jnp.int32, (tt, 1), 0)
    ys = []
    for k, win in enumerate(POOL_WINDOWS):
        cs = slice(k * POOL_CH, (k + 1) * POOL_CH)
        x = h_ref[:, cs]
        acc = x
        for j in range(1, win):
            acc = acc + ext_scr[pad - j:pad - j + tt, cs]
        cnt = jnp.minimum(t + 1, win).astype(f32)
        ys.append(_dot(acc / cnt - x, w_ref[k]))
    y = jnp.concatenate(ys, axis=1) * scale_ref[...]
    out = _layer_norm(DN_ALPHA * h_ref[...] + y, g_ref[...], beta_ref[...])
    of_ref[...] = out
    ob_ref[...] = out.astype(bf16)
    _store_packed(out, hp_ref)


def _pool_layer(h, w_grp, scale, g, beta, tt=256, pad=16):
    n, d = h.shape
    ratio = tt // pad
    row = lambda i: (i, 0)
    fixed = lambda i: (0, 0)
    return pl.pallas_call(
        _pool_kernel,
        grid=(n // tt,),
        in_specs=[pl.BlockSpec((tt, d), row),
                  pl.BlockSpec((pad, d), lambda i: (jnp.maximum(i * ratio - 1, 0), 0)),
                  pl.BlockSpec(w_grp.shape, lambda i: (0, 0, 0)),
                  pl.BlockSpec((1, d), fixed), pl.BlockSpec((1, d), fixed), pl.BlockSpec((1, d), fixed)],
        out_specs=[pl.BlockSpec((tt, d), row), pl.BlockSpec((tt, d), row),
                   pl.BlockSpec((tt * PACK_ROWS, LANES), row)],
        out_shape=[jax.ShapeDtypeStruct((n, d), f32), jax.ShapeDtypeStruct((n, d), bf16),
                   jax.ShapeDtypeStruct((n * PACK_ROWS, LANES), jnp.uint32)],
        scratch_shapes=[pltpu.VMEM((tt + pad, d), f32)],
        compiler_params=_params("parallel"),
        name="pool_layer",
    )(h, h, w_grp, scale, g, beta)


R_E1, R_E2, R_G1, R_G2, R_RANK1, R_RANK2 = range(6)


def _router_kernel(h_ref, whi_ref, wlo_ref, b_ref, info_ref, info_t_ref, cnt_ref):
    tm = h_ref.shape[0]

    @pl.when(pl.program_id(0) == 0)
    def _():
        cnt_ref[...] = jnp.zeros_like(cnt_ref)

    h = h_ref[...]
    h_hi = h.astype(bf16)
    h_lo = (h - h_hi.astype(f32)).astype(bf16)
    logits = (jnp.dot(h_hi, whi_ref[...], preferred_element_type=f32)
              + jnp.dot(h_lo, whi_ref[...], preferred_element_type=f32)
              + jnp.dot(h_hi, wlo_ref[...], preferred_element_type=f32)) + b_ref[...]
    lane = lax.broadcasted_iota(jnp.int32, logits.shape, 1)
    real = lane < N_EXPERTS
    e = jnp.exp(logits - jnp.max(logits, axis=-1, keepdims=True))
    probs = e / jnp.sum(e, axis=-1, keepdims=True)

    a = probs
    b = pltpu.roll(probs, 1, 1)
    c = pltpu.roll(probs, 2, 1)
    d = pltpu.roll(probs, 3, 1)
    hi1, lo1 = jnp.maximum(a, b), jnp.minimum(a, b)
    hi2, lo2 = jnp.maximum(c, d), jnp.minimum(c, d)
    score = jnp.maximum(hi1, hi2) + jnp.maximum(jnp.minimum(hi1, hi2), jnp.maximum(lo1, lo2))
    best = None
    g_sel = None
    for grp in range(N_EXPERTS // EXPERTS_PER_GROUP):
        last = grp * EXPERTS_PER_GROUP + EXPERTS_PER_GROUP - 1
        s = jnp.max(jnp.where(lane == last, score, -1.0), axis=-1, keepdims=True)
        if grp == 0:
            best, g_sel = s, jnp.zeros_like(s, dtype=jnp.int32)
        else:
            better = s > best
            best = jnp.where(better, s, best)
            g_sel = jnp.where(better, grp, g_sel)

    in_grp = real & ((lane // EXPERTS_PER_GROUP) == g_sel)
    masked = jnp.where(in_grp, probs, -1.0)
    lane_f = lane.astype(f32)
    p1 = jnp.max(masked, axis=-1, keepdims=True)
    e1 = jnp.min(jnp.where(masked == p1, lane_f, float(LANES)), axis=-1, keepdims=True)
    masked2 = jnp.where(lane_f == e1, -2.0, masked)
    p2 = jnp.max(masked2, axis=-1, keepdims=True)
    e2 = jnp.min(jnp.where(masked2 == p2, lane_f, float(LANES)), axis=-1, keepdims=True)
    tot = p1 + p2

    oh1 = (lane_f == e1).astype(f32)
    oh2 = (lane_f == e2).astype(f32)
    both = oh1 + oh2
    ri = lax.broadcasted_iota(jnp.int32, (tm, tm), 0)
    ci = lax.broadcasted_iota(jnp.int32, (tm, tm), 1)
    tri = (ci < ri).astype(bf16)
    before = cnt_ref[0:1, :] + jnp.dot(tri, both.astype(bf16), preferred_element_type=f32)
    rank1 = jnp.sum(before * oh1, axis=-1, keepdims=True)
    rank2 = jnp.sum(before * oh2, axis=-1, keepdims=True)
    cnt_ref[...] = cnt_ref[...] + jnp.sum(both, axis=0, keepdims=True)

    info = jnp.zeros(logits.shape, f32)
    for slot, val in ((R_E1, e1), (R_E2, e2), (R_G1, p1 / tot), (R_G2, p2 / tot),
                      (R_RANK1, rank1), (R_RANK2, rank2)):
        info = jnp.where(lane == slot, val, info)
    info_ref[...] = info
    info_t_ref[...] = info.T[:SUBLANES, :]


def _router(h, router_w, router_b, tm=512):
    n, d = h.shape
    w = jnp.zeros((d, LANES), f32).at[:, :N_EXPERTS].set(router_w)
    w_hi = w.astype(bf16)
    w_lo = (w - w_hi.astype(f32)).astype(bf16)
    b = jnp.full((1, LANES), -1e30, f32).at[0, :N_EXPERTS].set(router_b)
    return pl.pallas_call(
        _router_kernel,
        grid=(n // tm,),
        in_specs=[pl.BlockSpec((tm, d), lambda i: (i, 0)), pl.BlockSpec((d, LANES), lambda i: (0, 0)),
                  pl.BlockSpec((d, LANES), lambda i: (0, 0)), pl.BlockSpec((1, LANES), lambda i: (0, 0))],
        out_specs=[pl.BlockSpec((tm, LANES), lambda i: (i, 0)), pl.BlockSpec((SUBLANES, tm), lambda i: (0, i)),
                   pl.BlockSpec((SUBLANES, LANES), lambda i: (0, 0))],
        out_shape=[jax.ShapeDtypeStruct((n, LANES), f32), jax.ShapeDtypeStruct((SUBLANES, n), f32),
                   jax.ShapeDtypeStruct((SUBLANES, LANES), f32)],
        compiler_params=_params("arbitrary"),
        name="router",
    )(h, w_hi, w_lo, b)


def _token_copy(src_ref, dst_ref, sem, src_tok, dst_tok, rows):
    src = pl.multiple_of(src_tok * rows, SUBLANES)
    dst = pl.multiple_of(dst_tok * rows, SUBLANES)
    return pltpu.make_async_copy(src_ref.at[pl.ds(src, rows)], dst_ref.at[pl.ds(dst, rows)], sem)


def _expert_kernel(blk_e_ref, blk_valid_ref, blk_first_ref, blk_next_ref, row_tok_ref,
                   hp_ref, wg_hbm, wu_hbm, wd_hbm, o_ref,
                   xg_scr, x_scr, stage_g, stage_u, stage_d, wg_scr, wu_scr, wd_scr, sems, wsems):
    i = pl.program_id(0)
    tm = x_scr.shape[0]
    nblk = pl.num_programs(0)

    def weight_copies(e):
        return (pltpu.make_async_copy(wg_hbm.at[e], stage_g, wsems.at[0]),
                pltpu.make_async_copy(wu_hbm.at[e], stage_u, wsems.at[1]),
                pltpu.make_async_copy(wd_hbm.at[e], stage_d, wsems.at[2]))

    @pl.when(i == 0)
    def _():
        for cp in weight_copies(blk_e_ref[0]):
            cp.start()

    def token_copy(slot, j, tok):
        return _token_copy(hp_ref, xg_scr.at[slot], sems.at[slot], tok, j, PACK_ROWS)

    def wait_gather(slot):
        def body(j0, carry):
            for k in range(GATHER_UNROLL):
                token_copy(slot, 0, 0).wait()
            return carry
        lax.fori_loop(0, tm // GATHER_UNROLL, body, 0)

    @pl.when(i == 0)
    def _():
        def body(j0, carry):
            for k in range(GATHER_UNROLL):
                j = j0 * GATHER_UNROLL + k
                token_copy(0, j, row_tok_ref[j]).start()
            return carry
        lax.fori_loop(0, tm // GATHER_UNROLL, body, 0)

    @pl.when(blk_first_ref[i] != 0)
    def _():
        for cp in weight_copies(blk_e_ref[i]):
            cp.wait()
        chunk = 256
        for stage, dst in ((stage_g, wg_scr), (stage_u, wu_scr), (stage_d, wd_scr)):
            for r in range(0, stage.shape[0], chunk):
                dst[r:r + chunk, :] = stage[r:r + chunk, :].astype(bf16)
        nxt = blk_next_ref[i]

        @pl.when(nxt >= 0)
        def _():
            for cp in weight_copies(nxt):
                cp.start()

    this_slot = i % 2
    next_slot = (i + 1) % 2

    @pl.when(blk_valid_ref[i] != 0)
    def _():
        wait_gather(this_slot)
        _load_packed(xg_scr.at[this_slot], x_scr)
        nxt_blk = jnp.minimum(i + 1, nblk - 1)
        for j in range(tm):
            token_copy(next_slot, j, row_tok_ref[nxt_blk * tm + j]).start()
        x = x_scr[...]
        gate = _dot(x, wg_scr[...])
        hid = gate * jax.nn.sigmoid(gate) * _dot(x, wu_scr[...])
        y = _dot(hid, wd_scr[...])
        for c in range(WIDE_ROWS):
            o_ref[pl.ds(c, tm, stride=WIDE_ROWS), :] = y[:, c * LANES:(c + 1) * LANES]

        @pl.when(i == nblk - 1)
        def _():
            wait_gather(next_slot)

    @pl.when(blk_valid_ref[i] == 0)
    def _():
        o_ref[...] = jnp.zeros_like(o_ref)

        @pl.when(blk_valid_ref[jnp.maximum(i - 1, 0)] != 0)
        def _():
            wait_gather(this_slot)


def _experts(hp, row_tok, blk_e, blk_valid, blk_first, blk_next, w_gate, w_up, w_down):
    cap = row_tok.shape[0]
    _, d, de = w_gate.shape
    row = lambda i, *_: (i, 0)
    any_spec = pl.BlockSpec(memory_space=pl.ANY)
    return pl.pallas_call(
        _expert_kernel,
        grid_spec=pltpu.PrefetchScalarGridSpec(
            num_scalar_prefetch=5, grid=(cap // MOE_ROWS,),
            in_specs=[any_spec, any_spec, any_spec, any_spec],
            out_specs=pl.BlockSpec((MOE_ROWS * WIDE_ROWS, LANES), row),
            scratch_shapes=[pltpu.VMEM((2, MOE_ROWS * PACK_ROWS, LANES), jnp.uint32),
                            pltpu.VMEM((MOE_ROWS, d), bf16),
                            pltpu.VMEM((d, de), f32), pltpu.VMEM((d, de), f32), pltpu.VMEM((de, d), f32),
                            pltpu.VMEM((d, de), bf16), pltpu.VMEM((d, de), bf16), pltpu.VMEM((de, d), bf16),
                            pltpu.SemaphoreType.DMA((2,)), pltpu.SemaphoreType.DMA((3,))]),
        out_shape=jax.ShapeDtypeStruct((cap * WIDE_ROWS, LANES), f32),
        compiler_params=_params("arbitrary"),
        name="experts",
    )(blk_e, blk_valid, blk_first, blk_next, row_tok, hp, w_gate, w_up, w_down)


def _load_wide(y_ref, tm):
    return jnp.concatenate([y_ref[pl.ds(c, tm, stride=WIDE_ROWS), :] for c in range(WIDE_ROWS)], axis=1)


def _moe_ln_kernel(dest_ref, h_ref, y_hbm, info_ref, g_ref, beta_ref, of_ref, ob_ref, y1_scr, y2_scr, sems):
    i = pl.program_id(0)
    ntile = pl.num_programs(0)
    tm = h_ref.shape[0]
    n_tok = ntile * tm

    def token_copy(buf, which, j, row):
        scr = (y1_scr, y2_scr)[which]
        return _token_copy(y_hbm, scr.at[buf], sems.at[buf], row, j, WIDE_ROWS)

    def wait_gather(buf):
        def body(j0, carry):
            for _ in range(2 * GATHER_UNROLL):
                token_copy(buf, 0, 0, 0).wait()
            return carry
        lax.fori_loop(0, tm // GATHER_UNROLL, body, 0)

    def start_gather(tile, buf):
        def body(j0, carry):
            for which in range(2):
                for k in range(GATHER_UNROLL):
                    j = j0 * GATHER_UNROLL + k
                    token_copy(buf, which, j, dest_ref[which * n_tok + tile * tm + j]).start()
            return carry
        lax.fori_loop(0, tm // GATHER_UNROLL, body, 0)

    this_buf = i % 2

    @pl.when(i == 0)
    def _():
        start_gather(0, 0)

    @pl.when(i + 1 < ntile)
    def _():
        start_gather(i + 1, (i + 1) % 2)

    wait_gather(this_buf)
    info = info_ref[...]
    g1 = info[:, R_G1:R_G1 + 1]
    g2 = info[:, R_G2:R_G2 + 1]
    f = g1 * _load_wide(y1_scr.at[this_buf], tm) + g2 * _load_wide(y2_scr.at[this_buf], tm)
    out = _layer_norm(DN_ALPHA * h_ref[...] + f, g_ref[...], beta_ref[...])
    of_ref[...] = out
    ob_ref[...] = out.astype(bf16)


def _moe_ln(h, y, dest, info, g, beta, tm=256):
    n, d = h.shape
    row = lambda i, *_: (i, 0)
    fixed = lambda i, *_: (0, 0)
    return pl.pallas_call(
        _moe_ln_kernel,
        grid_spec=pltpu.PrefetchScalarGridSpec(
            num_scalar_prefetch=1, grid=(n // tm,),
            in_specs=[pl.BlockSpec((tm, d), row), pl.BlockSpec(memory_space=pl.ANY),
                      pl.BlockSpec((tm, LANES), row), pl.BlockSpec((1, d), fixed), pl.BlockSpec((1, d), fixed)],
            out_specs=[pl.BlockSpec((tm, d), row), pl.BlockSpec((tm, d), row)],
            scratch_shapes=[pltpu.VMEM((2, tm * WIDE_ROWS, LANES), f32), pltpu.VMEM((2, tm * WIDE_ROWS, LANES), f32),
                            pltpu.SemaphoreType.DMA((2,))]),
        out_shape=[jax.ShapeDtypeStruct((n, d), f32), jax.ShapeDtypeStruct((n, d), bf16)],
        compiler_params=_params("arbitrary"),
        name="moe_ln",
    )(dest, h, y, info, g, beta)


def _moe_layer(hf, hp, router_w, router_b, w_gate, w_up, w_down, expert_base, g, beta):
    n, d = hf.shape
    info, info_t, cnt = _router(hf, router_w, router_b)
    experts = info_t[R_E1:R_E2 + 1].astype(jnp.int32)
    rank = info_t[R_RANK1:R_RANK2 + 1].astype(jnp.int32)
    counts = cnt[0, :N_EXPERTS].astype(jnp.int32)
    pcounts = ((counts + MOE_ROWS - 1) // MOE_ROWS) * MOE_ROWS
    pends = jnp.cumsum(pcounts)
    pstarts = pends - pcounts
    dest = pstarts[experts] + rank
    cap = 2 * n + N_EXPERTS * MOE_ROWS
    dest = dest.reshape(-1)
    tok = jnp.tile(jnp.arange(n, dtype=jnp.int32), 2)
    row_tok = (jnp.arange(cap, dtype=jnp.int32) % n).at[dest].set(tok, unique_indices=True)
    blk_start = jnp.arange(cap // MOE_ROWS, dtype=jnp.int32) * MOE_ROWS
    blk_e = jnp.sum((blk_start[:, None] >= pends[None, :]).astype(jnp.int32), axis=1)
    blk_e = jnp.minimum(blk_e, N_EXPERTS - 1)
    blk_valid = (blk_start < pends[-1]).astype(jnp.int32)
    last_e = jnp.max(jnp.where(counts > 0, jnp.arange(N_EXPERTS, dtype=jnp.int32), 0))
    blk_e = jnp.where(blk_valid != 0, blk_e, last_e)
    blk_first = blk_valid * jnp.concatenate([jnp.ones((1,), jnp.int32), (blk_e[1:] != blk_e[:-1]).astype(jnp.int32)])
    ids = jnp.arange(N_EXPERTS, dtype=jnp.int32)
    later = (ids[None, :] > ids[:, None]) & (counts[None, :] > 0)
    next_e = jnp.min(jnp.where(later, ids[None, :], N_EXPERTS), axis=1)
    next_e = jnp.where(next_e < N_EXPERTS, next_e + expert_base, -1).astype(jnp.int32)
    blk_next = jnp.sum(jnp.where(blk_e[:, None] == ids[None, :], next_e[None, :], 0), axis=1)

    y = _experts(hp, row_tok, blk_e + expert_base, blk_valid, blk_first, blk_next, w_gate, w_up, w_down)
    return _moe_ln(hf, y, dest, info, g, beta)


def kernel(x, s5_w_in, s5_b_re, s5_b_im, s5_c_re, s5_c_im, s5_a_re, s5_a_im, s5_log_step, s5_d, s5_w_glu, s5_w_out, cv_w_pw1, cv_b_pw1, cv_w_dw, cv_b_dw, cv_ln_g, cv_ln_b, cv_w_pw2, cv_b_pw2, pl_w, pl_scale, router_w, router_b, moe_w_gate, moe_w_up, moe_w_down, ln_mix_g, ln_mix_b, ln_ffn_g, ln_ffn_b):
    bsz, seq, d = x.shape
    hf = x.reshape(bsz * seq, d)
    hb = hf.astype(bf16)
    row = lambda v: v.reshape(1, -1)
    w_gate = moe_w_gate.reshape((-1,) + moe_w_gate.shape[2:])
    w_up = moe_w_up.reshape((-1,) + moe_w_up.shape[2:])
    w_down = moe_w_down.reshape((-1,) + moe_w_down.shape[2:])

    for i in range(DEPTH):
        mixer, j = i % N_MIXERS, i // N_MIXERS
        g_mix, b_mix = row(ln_mix_g[i]), row(ln_mix_b[i])
        if mixer == 0:
            u = _mm(hb, s5_w_in[j].astype(bf16))
            wagg, tmat, wout, dtab = _s5_derive(s5_b_re[j], s5_b_im[j], s5_c_re[j], s5_c_im[j],
                                                s5_a_re[j], s5_a_im[j], s5_log_step[j])
            z = _s5_core(u, wagg, tmat, wout, dtab, row(s5_d[j]))
            v = _mm_glu(z, s5_w_glu[j].astype(bf16), jnp.zeros((1, 2 * d), f32), bf16)
            hf, hb, hp = _mm_res_ln(v, s5_w_out[j].astype(bf16), jnp.zeros((1, d), f32), hf, g_mix, b_mix)
        elif mixer == 1:
            v = _mm_glu(hb, cv_w_pw1[j].astype(bf16), row(cv_b_pw1[j]), f32)
            cv = _conv_module(v, cv_w_dw[j], row(cv_b_dw[j]), row(cv_ln_g[j]), row(cv_ln_b[j]))
            hf, hb, hp = _mm_res_ln(cv, cv_w_pw2[j].astype(bf16), row(cv_b_pw2[j]), hf, g_mix, b_mix)
        else:
            hf, hb, hp = _pool_layer(hf, pl_w[j].astype(bf16), row(pl_scale[j]), g_mix, b_mix)
        hf, hb = _moe_layer(hf, hp, router_w, router_b, w_gate, w_up, w_down, i * N_EXPERTS,
                            row(ln_ffn_g[i]), row(ln_ffn_b[i]))
    return hf.reshape(bsz, seq, d)
```

```python
import math

import jax
import jax.numpy as jnp
from jax import lax
from jax.experimental import pallas as pl
from jax.experimental.pallas import tpu as pltpu

f32 = jnp.float32
bf16 = jnp.bfloat16

D_MODEL = 2048
DEPTH = 4
N_MIXERS = 3
S5_GROUP = 16
S5_STATE = 64
CONV_WIDTH = 31
POOL_WINDOWS = (2, 4, 8, 16)
POOL_CH = D_MODEL // len(POOL_WINDOWS)
N_EXPERTS = 16
EXPERTS_PER_GROUP = 4
D_EXPERT = D_MODEL // 2
DN_ALPHA = (2 * DEPTH) ** 0.25
LN_EPS = 1e-5

LANES = 128
SUBLANES = 8
VMEM_LIMIT = 56 * 1024 * 1024
S5_BLOCK = 8
S5_CHUNK_GROUPS = LANES // S5_GROUP
MOE_ROWS = 256
GATHER_UNROLL = 8
PACK_ROWS = D_MODEL // 2 // LANES
WIDE_ROWS = D_MODEL // LANES


def _params(*sem):
    return pltpu.CompilerParams(dimension_semantics=sem, vmem_limit_bytes=VMEM_LIMIT)


def _layer_norm(r, g, b):
    mu = jnp.mean(r, axis=-1, keepdims=True)
    xc = r - mu
    var = jnp.mean(xc * xc, axis=-1, keepdims=True)
    return xc * lax.rsqrt(var + LN_EPS) * g + b


def _dot(a, b):
    return jnp.dot(a.astype(bf16), b.astype(bf16), preferred_element_type=f32)


def _store_packed(out, hp_ref):
    tm, d = out.shape
    for c in range(PACK_ROWS):
        lo = out[:, c * LANES:(c + 1) * LANES].astype(bf16).astype(f32)
        hi = out[:, d // 2 + c * LANES:d // 2 + (c + 1) * LANES].astype(bf16).astype(f32)
        word = (lax.bitcast_convert_type(lo, jnp.uint32) >> 16) | lax.bitcast_convert_type(hi, jnp.uint32)
        hp_ref[pl.ds(c, tm, stride=PACK_ROWS), :] = word


def _load_packed(hp_ref, x_scr):
    tm, d = x_scr.shape
    for c in range(PACK_ROWS):
        word = hp_ref[pl.ds(c, tm, stride=PACK_ROWS), :]
        lo = lax.bitcast_convert_type(word << 16, f32)
        hi = lax.bitcast_convert_type(word & jnp.uint32(0xFFFF0000), f32)
        x_scr[:, c * LANES:(c + 1) * LANES] = lo.astype(bf16)
        x_scr[:, d // 2 + c * LANES:d // 2 + (c + 1) * LANES] = hi.astype(bf16)


def _mm_kernel(x_ref, w_ref, o_ref):
    o_ref[...] = _dot(x_ref[...], w_ref[...]).astype(o_ref.dtype)


def _mm(x, w, tm=512, tn=1024):
    m, k = x.shape
    n = w.shape[1]
    return pl.pallas_call(
        _mm_kernel,
        grid=(n // tn, m // tm),
        in_specs=[pl.BlockSpec((tm, k), lambda j, i: (i, 0)),
                  pl.BlockSpec((k, tn), lambda j, i: (0, j))],
        out_specs=pl.BlockSpec((tm, tn), lambda j, i: (i, j)),
        out_shape=jax.ShapeDtypeStruct((m, n), f32),
        compiler_params=_params("parallel", "parallel"),
        name="mm",
    )(x, w)


def _mm_glu_kernel(x_ref, wa_ref, wg_ref, ba_ref, bg_ref, o_ref):
    x = x_ref[...].astype(bf16)
    a = _dot(x, wa_ref[...]) + ba_ref[...]
    g = _dot(x, wg_ref[...]) + bg_ref[...]
    o_ref[...] = (a * jax.nn.sigmoid(g)).astype(o_ref.dtype)


def _mm_glu(x, w, b, out_dtype, tm=512, tn=512):
    m, k = x.shape
    n = w.shape[1] // 2
    nb = n // tn
    return pl.pallas_call(
        _mm_glu_kernel,
        grid=(nb, m // tm),
        in_specs=[pl.BlockSpec((tm, k), lambda j, i: (i, 0)),
                  pl.BlockSpec((k, tn), lambda j, i: (0, j)),
                  pl.BlockSpec((k, tn), lambda j, i: (0, j + nb)),
                  pl.BlockSpec((1, tn), lambda j, i: (0, j)),
                  pl.BlockSpec((1, tn), lambda j, i: (0, j + nb))],
        out_specs=pl.BlockSpec((tm, tn), lambda j, i: (i, j)),
        out_shape=jax.ShapeDtypeStruct((m, n), out_dtype),
        compiler_params=_params("parallel", "parallel"),
        name="mm_glu",
    )(x, w, w, b, b)


def _mm_res_ln_kernel(x_ref, w_ref, b_ref, h_ref, g_ref, beta_ref, of_ref, ob_ref, hp_ref):
    y = _dot(x_ref[...], w_ref[...]) + b_ref[...]
    out = _layer_norm(DN_ALPHA * h_ref[...] + y, g_ref[...], beta_ref[...])
    of_ref[...] = out
    ob_ref[...] = out.astype(bf16)
    _store_packed(out, hp_ref)


def _mm_res_ln(x, w, b, h, g, beta, tm=512):
    m, k = x.shape
    d = w.shape[1]
    row = lambda i: (i, 0)
    fixed = lambda i: (0, 0)
    return pl.pallas_call(
        _mm_res_ln_kernel,
        grid=(m // tm,),
        in_specs=[pl.BlockSpec((tm, k), row), pl.BlockSpec((k, d), fixed, pipeline_mode=pl.Buffered(1)),
                  pl.BlockSpec((1, d), fixed),
                  pl.BlockSpec((tm, d), row), pl.BlockSpec((1, d), fixed), pl.BlockSpec((1, d), fixed)],
        out_specs=[pl.BlockSpec((tm, d), row), pl.BlockSpec((tm, d), row),
                   pl.BlockSpec((tm * PACK_ROWS, LANES), row)],
        out_shape=[jax.ShapeDtypeStruct((m, d), f32), jax.ShapeDtypeStruct((m, d), bf16),
                   jax.ShapeDtypeStruct((m * PACK_ROWS, LANES), jnp.uint32)],
        compiler_params=_params("parallel"),
        name="mm_res_ln",
    )(x, w, b, h, g, beta)


def _cmul_add(xr, xi, ar, ai, sr, si):
    return xr + ar * sr - ai * si, xi + ar * si + ai * sr


def _expand_block_diag(compact_ref, rep_ref, w_scr, row_shift, col_shift):
    width = w_scr.shape[0]
    step = 2 * LANES
    for c0 in range(0, width, step):
        w = jnp.dot(compact_ref[0], rep_ref[:, c0:c0 + step], preferred_element_type=f32)
        row_g = (lax.broadcasted_iota(jnp.int32, w.shape, 0) >> row_shift) & (S5_CHUNK_GROUPS - 1)
        col_g = ((lax.broadcasted_iota(jnp.int32, w.shape, 1) + c0) >> col_shift) & (S5_CHUNK_GROUPS - 1)
        w_scr[:, c0:c0 + step] = jnp.where(row_g == col_g, w, 0.0).astype(bf16)


def _s5_kernel(u_ref, agg_ref, toe_ref, proj_ref, rep_state_ref, rep_chan_ref, dtab_ref, d_ref, z_ref,
               xb_scr, v_scr, wagg_scr, t_scr, wout_scr):
    m_rows = v_scr.shape[0]
    half = v_scr.shape[1] // 2
    ncol = half // LANES
    chan_shift = S5_GROUP.bit_length() - 1
    state_shift = S5_STATE.bit_length() - 1
    _expand_block_diag(agg_ref, rep_state_ref, wagg_scr, chan_shift, state_shift)
    _expand_block_diag(toe_ref, rep_chan_ref, t_scr, chan_shift, chan_shift)
    _expand_block_diag(proj_ref, rep_state_ref, wout_scr, chan_shift, state_shift)
    for i in range(S5_BLOCK):
        xb_scr[:, i * LANES:(i + 1) * LANES] = u_ref[pl.ds(i, m_rows, stride=S5_BLOCK), :].astype(bf16)
    xb = xb_scr[...]
    v_scr[...] = jnp.dot(xb, wagg_scr[...], preferred_element_type=f32)

    tab = dtab_ref[0]
    sub = lax.broadcasted_iota(jnp.int32, (SUBLANES, LANES), 0)

    def col(j, part):
        lo = part * half + j * LANES
        return slice(lo, lo + LANES)

    def bcast(row):
        return jnp.broadcast_to(row, (SUBLANES, LANES))

    def body(r, carry):
        r0 = pl.multiple_of(r * SUBLANES, SUBLANES)
        new = []
        for j in range(ncol):
            cr, ci = carry[2 * j], carry[2 * j + 1]
            xr = v_scr[pl.ds(r0, SUBLANES), col(j, 0)]
            xi = v_scr[pl.ds(r0, SUBLANES), col(j, 1)]
            for shift in (1, 2, 4):
                ar = bcast(tab[shift - 1:shift, col(j, 0)])
                ai = bcast(tab[shift - 1:shift, col(j, 1)])
                sr = jnp.where(sub >= shift, pltpu.roll(xr, shift, 0), 0.0)
                si = jnp.where(sub >= shift, pltpu.roll(xi, shift, 0), 0.0)
                xr, xi = _cmul_add(xr, xi, ar, ai, sr, si)
            xr, xi = _cmul_add(xr, xi, tab[:, col(j, 0)], tab[:, col(j, 1)], cr, ci)
            v_scr[pl.ds(r0, SUBLANES), col(j, 0)] = jnp.where(sub >= 1, pltpu.roll(xr, 1, 0), cr)
            v_scr[pl.ds(r0, SUBLANES), col(j, 1)] = jnp.where(sub >= 1, pltpu.roll(xi, 1, 0), ci)
            new.append(bcast(xr[SUBLANES - 1:SUBLANES, :]))
            new.append(bcast(xi[SUBLANES - 1:SUBLANES, :]))
        return tuple(new)

    zero = jnp.zeros((SUBLANES, LANES), f32)
    lax.fori_loop(0, m_rows // SUBLANES, body, (zero,) * (2 * ncol))

    sp = v_scr[...].astype(bf16)
    for i0 in range(0, S5_BLOCK, 2):
        cs = slice(i0 * LANES, (i0 + 2) * LANES)
        nt = (((1,), (1,)), ((), ()))
        y = (lax.dot_general(xb, t_scr[cs, :], nt, preferred_element_type=f32)
             + lax.dot_general(sp, wout_scr[cs, :], nt, preferred_element_type=f32))
        for i in (i0, i0 + 1):
            yi = y[:, (i - i0) * LANES:(i - i0 + 1) * LANES] + d_ref[...] * u_ref[pl.ds(i, m_rows, stride=S5_BLOCK), :]
            z_ref[pl.ds(i, m_rows, stride=S5_BLOCK), :] = 0.5 * yi * (1.0 + lax.erf(yi * (1.0 / math.sqrt(2.0))))


def _s5_core(u, agg, toe, proj, dtab, d_skip):
    n, d = u.shape
    nq = d // LANES
    m_rows = n // S5_BLOCK
    width = agg.shape[1]
    col = jnp.arange(width)[None, :]
    lane = jnp.arange(LANES)[:, None]
    half = width // 2
    rep_state = ((col // half == lane // S5_STATE) & (col % S5_STATE == lane % S5_STATE)).astype(bf16)
    rep_chan = ((col // LANES == lane // S5_GROUP) & (col % S5_GROUP == lane % S5_GROUP)).astype(bf16)
    wspec = pl.BlockSpec((1, width, LANES), lambda q: (q, 0, 0))
    rspec = pl.BlockSpec((LANES, width), lambda q: (0, 0))
    return pl.pallas_call(
        _s5_kernel,
        grid=(nq,),
        in_specs=[pl.BlockSpec((n, LANES), lambda q: (0, q)), wspec, wspec, wspec, rspec, rspec,
                  pl.BlockSpec((1, SUBLANES, width), lambda q: (q, 0, 0)),
                  pl.BlockSpec((1, LANES), lambda q: (0, q))],
        out_specs=pl.BlockSpec((n, LANES), lambda q: (0, q)),
        out_shape=jax.ShapeDtypeStruct((n, d), f32),
        scratch_shapes=[pltpu.VMEM((m_rows, width), bf16), pltpu.VMEM((m_rows, width), f32),
                        pltpu.VMEM((width, width), bf16), pltpu.VMEM((width, width), bf16),
                        pltpu.VMEM((width, width), bf16)],
        compiler_params=_params("parallel"),
        name="s5_core",
    )(u, agg, toe, proj, rep_state, rep_chan, dtab, d_skip)


def _s5_derive(b_re, b_im, c_re, c_im, a_re, a_im, log_step):
    highest = lax.Precision.HIGHEST
    g, p, c = b_re.shape
    nq = g // S5_CHUNK_GROUPS
    dt = jnp.exp(log_step)[:, None]
    mag = jnp.exp(a_re * dt)
    lb_re = mag * jnp.cos(a_im * dt)
    lb_im = mag * jnp.sin(a_im * dt)
    den = a_re * a_re + a_im * a_im
    n_re = lb_re - 1.0
    n_im = lb_im
    f_re = (n_re * a_re + n_im * a_im) / den
    f_im = (n_im * a_re - n_re * a_im) / den
    bb_re = f_re[..., None] * b_re - f_im[..., None] * b_im
    bb_im = f_re[..., None] * b_im + f_im[..., None] * b_re

    def powers(br, bi, count):
        rs, is_ = [br], [bi]
        for _ in range(count - 1):
            rs.append(rs[-1] * br - is_[-1] * bi)
            is_.append(rs[-2] * bi + is_[-1] * br)
        return rs, is_

    pr, pi = powers(lb_re, lb_im, S5_BLOCK)
    lam_r = [jnp.ones_like(lb_re)] + pr
    lam_i = [jnp.zeros_like(lb_im)] + pi
    width = S5_BLOCK * LANES

    def compact(w):
        w = w.reshape(S5_BLOCK, nq, S5_CHUNK_GROUPS * c, LANES)
        return w.transpose(1, 0, 2, 3).reshape(nq, width, LANES).astype(bf16)

    bt_re = bb_re.transpose(0, 2, 1)
    bt_im = bb_im.transpose(0, 2, 1)

    ar = jnp.stack([lam_r[S5_BLOCK - 1 - i] for i in range(S5_BLOCK)])[:, :, None, :]
    ai = jnp.stack([lam_i[S5_BLOCK - 1 - i] for i in range(S5_BLOCK)])[:, :, None, :]
    wagg = compact(jnp.concatenate([ar * bt_re - ai * bt_im, ar * bt_im + ai * bt_re], axis=-1))

    ar = jnp.stack(lam_r[1:S5_BLOCK + 1])[:, :, None, :]
    ai = jnp.stack(lam_i[1:S5_BLOCK + 1])[:, :, None, :]
    wout = compact(jnp.concatenate([c_re * ar - c_im * ai, -(c_re * ai + c_im * ar)], axis=-1))

    ar = jnp.stack(lam_r[:S5_BLOCK])[:, :, None, :]
    ai = jnp.stack(lam_i[:S5_BLOCK])[:, :, None, :]
    kj = (jnp.einsum('jgcp,gpd->jgcd', c_re * ar - c_im * ai, bb_re, precision=highest)
          - jnp.einsum('jgcp,gpd->jgcd', c_re * ai + c_im * ar, bb_im, precision=highest))
    none = jnp.zeros_like(kj[0])
    tmat = compact(jnp.stack([jnp.concatenate([kj[i - a] if i >= a else none for a in range(S5_BLOCK)], axis=-1)
                              for i in range(S5_BLOCK)]))

    dr, di = powers(lam_r[S5_BLOCK], lam_i[S5_BLOCK], SUBLANES)
    dtab = jnp.stack([jnp.stack(dr), jnp.stack(di)], axis=1)
    dtab = dtab.reshape(SUBLANES, 2, nq, S5_CHUNK_GROUPS * p).transpose(2, 0, 1, 3).reshape(nq, SUBLANES, width)
    return wagg, tmat, wout, dtab


def _conv_kernel(v_ref, halo_ref, w_ref, b_ref, g_ref, beta_ref, o_ref, ext_scr, rot_scr, acc_scr):
    tt = v_ref.shape[0]
    pad = halo_ref.shape[0]
    first = pl.program_id(0) == 0
    ext_scr[0:pad, :] = jnp.where(first, 0.0, halo_ref[...])
    ext_scr[pad:pad + tt, :] = v_ref[...]
    span = rot_scr.shape[1]
    for sh in range(1, SUBLANES):
        rot_scr[sh - 1] = ext_scr[sh:sh + span, :]
    rows, cols = 32, 256
    for c0 in range(0, v_ref.shape[1], cols):
        for r0 in range(0, tt, rows):
            acc = jnp.broadcast_to(b_ref[:, c0:c0 + cols], (rows, cols))
            for k in range(CONV_WIDTH):
                whole, sh = divmod(pad - (CONV_WIDTH - 1) + k, SUBLANES)
                off = whole * SUBLANES + r0
                if sh == 0:
                    win = ext_scr[off:off + rows, c0:c0 + cols]
                else:
                    win = rot_scr[sh - 1, off:off + rows, c0:c0 + cols]
                acc = acc + w_ref[k:k + 1, c0:c0 + cols] * win
            acc_scr[r0:r0 + rows, c0:c0 + cols] = acc
    y = _layer_norm(acc_scr[...], g_ref[...], beta_ref[...])
    o_ref[...] = (y * jax.nn.sigmoid(y)).astype(o_ref.dtype)


def _conv_module(v, w_dw, b_dw, g, beta, tt=128, pad=32):
    n, d = v.shape
    w_pad = jnp.zeros((pad, d), f32).at[:CONV_WIDTH].set(w_dw)
    ratio = tt // pad
    row = lambda i: (i, 0)
    fixed = lambda i: (0, 0)
    return pl.pallas_call(
        _conv_kernel,
        grid=(n // tt,),
        in_specs=[pl.BlockSpec((tt, d), row),
                  pl.BlockSpec((pad, d), lambda i: (jnp.maximum(i * ratio - 1, 0), 0)),
                  pl.BlockSpec((pad, d), fixed), pl.BlockSpec((1, d), fixed),
                  pl.BlockSpec((1, d), fixed), pl.BlockSpec((1, d), fixed)],
        out_specs=pl.BlockSpec((tt, d), row),
        out_shape=jax.ShapeDtypeStruct((n, d), bf16),
        scratch_shapes=[pltpu.VMEM((tt + pad, d), f32),
                        pltpu.VMEM((SUBLANES - 1, tt + pad - SUBLANES, d), f32),
                        pltpu.VMEM((tt, d), f32)],
        compiler_params=_params("parallel"),
        name="conv_module",
    )(v, v, w_pad, b_dw, g, beta)


def _pool_kernel(h_ref, halo_ref, w_ref, scale_ref, g_ref, beta_ref, of_ref, ob_ref, hp_ref, ext_scr):
    tt = h_ref.shape[0]
    pad = halo_ref.shape[0]
    i = pl.program_id(0)
    ext_scr[0:pad, :] = jnp.where(i == 0, 0.0, halo_ref[...])
    ext_scr[pad:pad + tt, :] = h_ref[...]
    t = i * tt + lax.broadcasted_iota(jnp.int32, (tt, 1), 0)
    ys = []
    for k, win in enumerate(POOL_WINDOWS):
        cs = slice(k * POOL_CH, (k + 1) * POOL_CH)
        x = h_ref[:, cs]
        acc = x
        for j in range(1, win):
            acc = acc + ext_scr[pad - j:pad - j + tt, cs]
        cnt = jnp.minimum(t + 1, win).astype(f32)
        ys.append(_dot(acc / cnt - x, w_ref[k]))
    y = jnp.concatenate(ys, axis=1) * scale_ref[...]
    out = _layer_norm(DN_ALPHA * h_ref[...] + y, g_ref[...], beta_ref[...])
    of_ref[...] = out
    ob_ref[...] = out.astype(bf16)
    _store_packed(out, hp_ref)


def _pool_layer(h, w_grp, scale, g, beta, tt=256, pad=16):
    n, d = h.shape
    ratio = tt // pad
    row = lambda i: (i, 0)
    fixed = lambda i: (0, 0)
    return pl.pallas_call(
        _pool_kernel,
        grid=(n // tt,),
        in_specs=[pl.BlockSpec((tt, d), row),
                  pl.BlockSpec((pad, d), lambda i: (jnp.maximum(i * ratio - 1, 0), 0)),
                  pl.BlockSpec(w_grp.shape, lambda i: (0, 0, 0)),
                  pl.BlockSpec((1, d), fixed), pl.BlockSpec((1, d), fixed), pl.BlockSpec((1, d), fixed)],
        out_specs=[pl.BlockSpec((tt, d), row), pl.BlockSpec((tt, d), row),
                   pl.BlockSpec((tt * PACK_ROWS, LANES), row)],
        out_shape=[jax.ShapeDtypeStruct((n, d), f32), jax.ShapeDtypeStruct((n, d), bf16),
                   jax.ShapeDtypeStruct((n * PACK_ROWS, LANES), jnp.uint32)],
        scratch_shapes=[pltpu.VMEM((tt + pad, d), f32)],
        compiler_params=_params("parallel"),
        name="pool_layer",
    )(h, h, w_grp, scale, g, beta)


R_E1, R_E2, R_G1, R_G2, R_RANK1, R_RANK2 = range(6)


def _router_kernel(h_ref, whi_ref, wlo_ref, b_ref, info_ref, info_t_ref, cnt_ref):
    tm = h_ref.shape[0]

    @pl.when(pl.program_id(0) == 0)
    def _():
        cnt_ref[...] = jnp.zeros_like(cnt_ref)

    h = h_ref[...]
    h_hi = h.astype(bf16)
    h_lo = (h - h_hi.astype(f32)).astype(bf16)
    logits = (jnp.dot(h_hi, whi_ref[...], preferred_element_type=f32)
              + jnp.dot(h_lo, whi_ref[...], preferred_element_type=f32)
              + jnp.dot(h_hi, wlo_ref[...], preferred_element_type=f32)) + b_ref[...]
    lane = lax.broadcasted_iota(jnp.int32, logits.shape, 1)
    real = lane < N_EXPERTS
    e = jnp.exp(logits - jnp.max(logits, axis=-1, keepdims=True))
    probs = e / jnp.sum(e, axis=-1, keepdims=True)

    a = probs
    b = pltpu.roll(probs, 1, 1)
    c = pltpu.roll(probs, 2, 1)
    d = pltpu.roll(probs, 3, 1)
    hi1, lo1 = jnp.maximum(a, b), jnp.minimum(a, b)
    hi2, lo2 = jnp.maximum(c, d), jnp.minimum(c, d)
    score = jnp.maximum(hi1, hi2) + jnp.maximum(jnp.minimum(hi1, hi2), jnp.maximum(lo1, lo2))
    best = None
    g_sel = None
    for grp in range(N_EXPERTS // EXPERTS_PER_GROUP):
        last = grp * EXPERTS_PER_GROUP + EXPERTS_PER_GROUP - 1
        s = jnp.max(jnp.where(lane == last, score, -1.0), axis=-1, keepdims=True)
        if grp == 0:
            best, g_sel = s, jnp.zeros_like(s, dtype=jnp.int32)
        else:
            better = s > best
            best = jnp.where(better, s, best)
            g_sel = jnp.where(better, grp, g_sel)

    in_grp = real & ((lane // EXPERTS_PER_GROUP) == g_sel)
    masked = jnp.where(in_grp, probs, -1.0)
    lane_f = lane.astype(f32)
    p1 = jnp.max(masked, axis=-1, keepdims=True)
    e1 = jnp.min(jnp.where(masked == p1, lane_f, float(LANES)), axis=-1, keepdims=True)
    masked2 = jnp.where(lane_f == e1, -2.0, masked)
    p2 = jnp.max(masked2, axis=-1, keepdims=True)
    e2 = jnp.min(jnp.where(masked2 == p2, lane_f, float(LANES)), axis=-1, keepdims=True)
    tot = p1 + p2

    oh1 = (lane_f == e1).astype(f32)
    oh2 = (lane_f == e2).astype(f32)
    both = oh1 + oh2
    ri = lax.broadcasted_iota(jnp.int32, (tm, tm), 0)
    ci = lax.broadcasted_iota(jnp.int32, (tm, tm), 1)
    tri = (ci < ri).astype(bf16)
    before = cnt_ref[0:1, :] + jnp.dot(tri, both.astype(bf16), preferred_element_type=f32)
    rank1 = jnp.sum(before * oh1, axis=-1, keepdims=True)
    rank2 = jnp.sum(before * oh2, axis=-1, keepdims=True)
    cnt_ref[...] = cnt_ref[...] + jnp.sum(both, axis=0, keepdims=True)

    info = jnp.zeros(logits.shape, f32)
    for slot, val in ((R_E1, e1), (R_E2, e2), (R_G1, p1 / tot), (R_G2, p2 / tot),
                      (R_RANK1, rank1), (R_RANK2, rank2)):
        info = jnp.where(lane == slot, val, info)
    info_ref[...] = info
    info_t_ref[...] = info.T[:SUBLANES, :]


def _router(h, router_w, router_b, tm=512):
    n, d = h.shape
    w = jnp.zeros((d, LANES), f32).at[:, :N_EXPERTS].set(router_w)
    w_hi = w.astype(bf16)
    w_lo = (w - w_hi.astype(f32)).astype(bf16)
    b = jnp.full((1, LANES), -1e30, f32).at[0, :N_EXPERTS].set(router_b)
    return pl.pallas_call(
        _router_kernel,
        grid=(n // tm,),
        in_specs=[pl.BlockSpec((tm, d), lambda i: (i, 0)), pl.BlockSpec((d, LANES), lambda i: (0, 0)),
                  pl.BlockSpec((d, LANES), lambda i: (0, 0)), pl.BlockSpec((1, LANES), lambda i: (0, 0))],
        out_specs=[pl.BlockSpec((tm, LANES), lambda i: (i, 0)), pl.BlockSpec((SUBLANES, tm), lambda i: (0, i)),
                   pl.BlockSpec((SUBLANES, LANES), lambda i: (0, 0))],
        out_shape=[jax.ShapeDtypeStruct((n, LANES), f32), jax.ShapeDtypeStruct((SUBLANES, n), f32),
                   jax.ShapeDtypeStruct((SUBLANES, LANES), f32)],
        compiler_params=_params("arbitrary"),
        name="router",
    )(h, w_hi, w_lo, b)


def _token_copy(src_ref, dst_ref, sem, src_tok, dst_tok, rows):
    src = pl.multiple_of(src_tok * rows, SUBLANES)
    dst = pl.multiple_of(dst_tok * rows, SUBLANES)
    return pltpu.make_async_copy(src_ref.at[pl.ds(src, rows)], dst_ref.at[pl.ds(dst, rows)], sem)


def _expert_kernel(blk_e_ref, blk_valid_ref, blk_first_ref, blk_next_ref, row_tok_ref,
                   hp_ref, wg_hbm, wu_hbm, wd_hbm, o_ref,
                   xg_scr, x_scr, stage_g, stage_u, stage_d, wg_scr, wu_scr, wd_scr, sems, wsems):
    i = pl.program_id(0)
    tm = x_scr.shape[0]
    nblk = pl.num_programs(0)

    def weight_copies(e):
        return (pltpu.make_async_copy(wg_hbm.at[e], stage_g, wsems.at[0]),
                pltpu.make_async_copy(wu_hbm.at[e], stage_u, wsems.at[1]),
                pltpu.make_async_copy(wd_hbm.at[e], stage_d, wsems.at[2]))

    @pl.when(i == 0)
    def _():
        for cp in weight_copies(blk_e_ref[0]):
            cp.start()

    def token_copy(slot, j, tok):
        return _token_copy(hp_ref, xg_scr.at[slot], sems.at[slot], tok, j, PACK_ROWS)

    def wait_gather(slot):
        def body(j0, carry):
            for k in range(GATHER_UNROLL):
                token_copy(slot, 0, 0).wait()
            return carry
        lax.fori_loop(0, tm // GATHER_UNROLL, body, 0)

    @pl.when(i == 0)
    def _():
        def body(j0, carry):
            for k in range(GATHER_UNROLL):
                j = j0 * GATHER_UNROLL + k
                token_copy(0, j, row_tok_ref[j]).start()
            return carry
        lax.fori_loop(0, tm // GATHER_UNROLL, body, 0)

    @pl.when(blk_first_ref[i] != 0)
    def _():
        for cp in weight_copies(blk_e_ref[i]):
            cp.wait()
        chunk = 256
        for stage, dst in ((stage_g, wg_scr), (stage_u, wu_scr), (stage_d, wd_scr)):
            for r in range(0, stage.shape[0], chunk):
                dst[r:r + chunk, :] = stage[r:r + chunk, :].astype(bf16)
        nxt = blk_next_ref[i]

        @pl.when(nxt >= 0)
        def _():
            for cp in weight_copies(nxt):
                cp.start()

    this_slot = i % 2
    next_slot = (i + 1) % 2

    @pl.when(blk_valid_ref[i] != 0)
    def _():
        wait_gather(this_slot)
        _load_packed(xg_scr.at[this_slot], x_scr)
        nxt_blk = jnp.minimum(i + 1, nblk - 1)
        for j in range(tm):
            token_copy(next_slot, j, row_tok_ref[nxt_blk * tm + j]).start()
        x = x_scr[...]
        gate = _dot(x, wg_scr[...])
        hid = gate * jax.nn.sigmoid(gate) * _dot(x, wu_scr[...])
        y = _dot(hid, wd_scr[...])
        for c in range(WIDE_ROWS):
            o_ref[pl.ds(c, tm, stride=WIDE_ROWS), :] = y[:, c * LANES:(c + 1) * LANES]

        @pl.when(i == nblk - 1)
        def _():
            wait_gather(next_slot)

    @pl.when(blk_valid_ref[i] == 0)
    def _():
        o_ref[...] = jnp.zeros_like(o_ref)

        @pl.when(blk_valid_ref[jnp.maximum(i - 1, 0)] != 0)
        def _():
            wait_gather(this_slot)


def _experts(hp, row_tok, blk_e, blk_valid, blk_first, blk_next, w_gate, w_up, w_down):
    cap = row_tok.shape[0]
    _, d, de = w_gate.shape
    row = lambda i, *_: (i, 0)
    any_spec = pl.BlockSpec(memory_space=pl.ANY)
    return pl.pallas_call(
        _expert_kernel,
        grid_spec=pltpu.PrefetchScalarGridSpec(
            num_scalar_prefetch=5, grid=(cap // MOE_ROWS,),
            in_specs=[any_spec, any_spec, any_spec, any_spec],
            out_specs=pl.BlockSpec((MOE_ROWS * WIDE_ROWS, LANES), row),
            scratch_shapes=[pltpu.VMEM((2, MOE_ROWS * PACK_ROWS, LANES), jnp.uint32),
                            pltpu.VMEM((MOE_ROWS, d), bf16),
                            pltpu.VMEM((d, de), f32), pltpu.VMEM((d, de), f32), pltpu.VMEM((de, d), f32),
                            pltpu.VMEM((d, de), bf16), pltpu.VMEM((d, de), bf16), pltpu.VMEM((de, d), bf16),
                            pltpu.SemaphoreType.DMA((2,)), pltpu.SemaphoreType.DMA((3,))]),
        out_shape=jax.ShapeDtypeStruct((cap * WIDE_ROWS, LANES), f32),
        compiler_params=_params("arbitrary"),
        name="experts",
    )(blk_e, blk_valid, blk_first, blk_next, row_tok, hp, w_gate, w_up, w_down)


def _load_wide(y_ref, tm):
    return jnp.concatenate([y_ref[pl.ds(c, tm, stride=WIDE_ROWS), :] for c in range(WIDE_ROWS)], axis=1)


def _moe_ln_kernel(dest_ref, h_ref, y_hbm, info_ref, g_ref, beta_ref, of_ref, ob_ref, y1_scr, y2_scr, sems):
    i = pl.program_id(0)
    ntile = pl.num_programs(0)
    tm = h_ref.shape[0]
    n_tok = ntile * tm

    def token_copy(buf, which, j, row):
        scr = (y1_scr, y2_scr)[which]
        return _token_copy(y_hbm, scr.at[buf], sems.at[buf], row, j, WIDE_ROWS)

    def wait_gather(buf):
        def body(j0, carry):
            for _ in range(2 * GATHER_UNROLL):
                token_copy(buf, 0, 0, 0).wait()
            return carry
        lax.fori_loop(0, tm // GATHER_UNROLL, body, 0)

    def start_gather(tile, buf):
        def body(j0, carry):
            for which in range(2):
                for k in range(GATHER_UNROLL):
                    j = j0 * GATHER_UNROLL + k
                    token_copy(buf, which, j, dest_ref[which * n_tok + tile * tm + j]).start()
            return carry
        lax.fori_loop(0, tm // GATHER_UNROLL, body, 0)

    this_buf = i % 2

    @pl.when(i == 0)
    def _():
        start_gather(0, 0)

    @pl.when(i + 1 < ntile)
    def _():
        start_gather(i + 1, (i + 1) % 2)

    wait_gather(this_buf)
    info = info_ref[...]
    g1 = info[:, R_G1:R_G1 + 1]
    g2 = info[:, R_G2:R_G2 + 1]
    f = g1 * _load_wide(y1_scr.at[this_buf], tm) + g2 * _load_wide(y2_scr.at[this_buf], tm)
    out = _layer_norm(DN_ALPHA * h_ref[...] + f, g_ref[...], beta_ref[...])
    of_ref[...] = out
    ob_ref[...] = out.astype(bf16)


def _moe_ln(h, y, dest, info, g, beta, tm=256):
    n, d = h.shape
    row = lambda i, *_: (i, 0)
    fixed = lambda i, *_: (0, 0)
    return pl.pallas_call(
        _moe_ln_kernel,
        grid_spec=pltpu.PrefetchScalarGridSpec(
            num_scalar_prefetch=1, grid=(n // tm,),
            in_specs=[pl.BlockSpec((tm, d), row), pl.BlockSpec(memory_space=pl.ANY),
                      pl.BlockSpec((tm, LANES), row), pl.BlockSpec((1, d), fixed), pl.BlockSpec((1, d), fixed)],
            out_specs=[pl.BlockSpec((tm, d), row), pl.BlockSpec((tm, d), row)],
            scratch_shapes=[pltpu.VMEM((2, tm * WIDE_ROWS, LANES), f32), pltpu.VMEM((2, tm * WIDE_ROWS, LANES), f32),
                            pltpu.SemaphoreType.DMA((2,))]),
        out_shape=[jax.ShapeDtypeStruct((n, d), f32), jax.ShapeDtypeStruct((n, d), bf16)],
        compiler_params=_params("arbitrary"),
        name="moe_ln",
    )(dest, h, y, info, g, beta)


def _moe_layer(hf, hp, router_w, router_b, w_gate, w_up, w_down, expert_base, g, beta):
    n, d = hf.shape
    info, info_t, cnt = _router(hf, router_w, router_b)
    experts = info_t[R_E1:R_E2 + 1].astype(jnp.int32)
    rank = info_t[R_RANK1:R_RANK2 + 1].astype(jnp.int32)
    counts = cnt[0, :N_EXPERTS].astype(jnp.int32)
    pcounts = ((counts + MOE_ROWS - 1) // MOE_ROWS) * MOE_ROWS
    pends = jnp.cumsum(pcounts)
    pstarts = pends - pcounts
    dest = rank
    for e in range(N_EXPERTS):
        dest = dest + jnp.where(experts == e, pstarts[e], 0)
    cap = 2 * n + N_EXPERTS * MOE_ROWS
    dest = dest.reshape(-1)
    tok = jnp.tile(jnp.arange(n, dtype=jnp.int32), 2)
    row_tok = (jnp.arange(cap, dtype=jnp.int32) % n).at[dest].set(tok, unique_indices=True)
    blk_start = jnp.arange(cap // MOE_ROWS, dtype=jnp.int32) * MOE_ROWS
    blk_e = jnp.sum((blk_start[:, None] >= pends[None, :]).astype(jnp.int32), axis=1)
    blk_e = jnp.minimum(blk_e, N_EXPERTS - 1)
    blk_valid = (blk_start < pends[-1]).astype(jnp.int32)
    last_e = jnp.max(jnp.where(counts > 0, jnp.arange(N_EXPERTS, dtype=jnp.int32), 0))
    blk_e = jnp.where(blk_valid != 0, blk_e, last_e)
    blk_first = blk_valid * jnp.concatenate([jnp.ones((1,), jnp.int32), (blk_e[1:] != blk_e[:-1]).astype(jnp.int32)])
    ids = jnp.arange(N_EXPERTS, dtype=jnp.int32)
    later = (ids[None, :] > ids[:, None]) & (counts[None, :] > 0)
    next_e = jnp.min(jnp.where(later, ids[None, :], N_EXPERTS), axis=1)
    next_e = jnp.where(next_e < N_EXPERTS, next_e + expert_base, -1).astype(jnp.int32)
    blk_next = jnp.sum(jnp.where(blk_e[:, None] == ids[None, :], next_e[None, :], 0), axis=1)

    y = _experts(hp, row_tok, blk_e + expert_base, blk_valid, blk_first, blk_next, w_gate, w_up, w_down)
    return _moe_ln(hf, y, dest, info, g, beta)


def kernel(x, s5_w_in, s5_b_re, s5_b_im, s5_c_re, s5_c_im, s5_a_re, s5_a_im, s5_log_step, s5_d, s5_w_glu, s5_w_out, cv_w_pw1, cv_b_pw1, cv_w_dw, cv_b_dw, cv_ln_g, cv_ln_b, cv_w_pw2, cv_b_pw2, pl_w, pl_scale, router_w, router_b, moe_w_gate, moe_w_up, moe_w_down, ln_mix_g, ln_mix_b, ln_ffn_g, ln_ffn_b):
    bsz, seq, d = x.shape
    hf = x.reshape(bsz * seq, d)
    hb = hf.astype(bf16)
    row = lambda v: v.reshape(1, -1)
    w_gate = moe_w_gate.reshape((-1,) + moe_w_gate.shape[2:])
    w_up = moe_w_up.reshape((-1,) + moe_w_up.shape[2:])
    w_down = moe_w_down.reshape((-1,) + moe_w_down.shape[2:])

    for i in range(DEPTH):
        mixer, j = i % N_MIXERS, i // N_MIXERS
        g_mix, b_mix = row(ln_mix_g[i]), row(ln_mix_b[i])
        if mixer == 0:
            u = _mm(hb, s5_w_in[j].astype(bf16))
            wagg, tmat, wout, dtab = _s5_derive(s5_b_re[j], s5_b_im[j], s5_c_re[j], s5_c_im[j],
                                                s5_a_re[j], s5_a_im[j], s5_log_step[j])
            z = _s5_core(u, wagg, tmat, wout, dtab, row(s5_d[j]))
            v = _mm_glu(z, s5_w_glu[j].astype(bf16), jnp.zeros((1, 2 * d), f32), bf16)
            hf, hb, hp = _mm_res_ln(v, s5_w_out[j].astype(bf16), jnp.zeros((1, d), f32), hf, g_mix, b_mix)
        elif mixer == 1:
            v = _mm_glu(hb, cv_w_pw1[j].astype(bf16), row(cv_b_pw1[j]), f32)
            cv = _conv_module(v, cv_w_dw[j], row(cv_b_dw[j]), row(cv_ln_g[j]), row(cv_ln_b[j]))
            hf, hb, hp = _mm_res_ln(cv, cv_w_pw2[j].astype(bf16), row(cv_b_pw2[j]), hf, g_mix, b_mix)
        else:
            hf, hb, hp = _pool_layer(hf, pl_w[j].astype(bf16), row(pl_scale[j]), g_mix, b_mix)
        hf, hb = _moe_layer(hf, hp, router_w, router_b, w_gate, w_up, w_down, i * N_EXPERTS,
                            row(ln_ffn_g[i]), row(ln_ffn_b[i]))
    return hf.reshape(bsz, seq, d)
```

```python
import math

import jax
import jax.numpy as jnp
from jax import lax
from jax.experimental import pallas as pl
from jax.experimental.pallas import tpu as pltpu

f32 = jnp.float32
bf16 = jnp.bfloat16

D_MODEL = 2048
DEPTH = 4
N_MIXERS = 3
S5_GROUP = 16
S5_STATE = 64
CONV_WIDTH = 31
POOL_WINDOWS = (2, 4, 8, 16)
POOL_CH = D_MODEL // len(POOL_WINDOWS)
N_EXPERTS = 16
EXPERTS_PER_GROUP = 4
D_EXPERT = D_MODEL // 2
DN_ALPHA = (2 * DEPTH) ** 0.25
LN_EPS = 1e-5

LANES = 128
SUBLANES = 8
VMEM_LIMIT = 56 * 1024 * 1024
S5_BLOCK = 8
S5_CHUNK_GROUPS = LANES // S5_GROUP
MOE_ROWS = 256
GATHER_UNROLL = 8
PACK_ROWS = D_MODEL // 2 // LANES
WIDE_ROWS = D_MODEL // LANES


def _params(*sem):
    return pltpu.CompilerParams(dimension_semantics=sem, vmem_limit_bytes=VMEM_LIMIT)


def _layer_norm(r, g, b):
    mu = jnp.mean(r, axis=-1, keepdims=True)
    xc = r - mu
    var = jnp.mean(xc * xc, axis=-1, keepdims=True)
    return xc * lax.rsqrt(var + LN_EPS) * g + b


def _dot(a, b):
    return jnp.dot(a.astype(bf16), b.astype(bf16), preferred_element_type=f32)


def _store_packed(out, hp_ref):
    tm, d = out.shape
    for c in range(PACK_ROWS):
        lo = out[:, c * LANES:(c + 1) * LANES].astype(bf16).astype(f32)
        hi = out[:, d // 2 + c * LANES:d // 2 + (c + 1) * LANES].astype(bf16).astype(f32)
        word = (lax.bitcast_convert_type(lo, jnp.uint32) >> 16) | lax.bitcast_convert_type(hi, jnp.uint32)
        hp_ref[pl.ds(c, tm, stride=PACK_ROWS), :] = word


def _load_packed(hp_ref, x_scr):
    tm, d = x_scr.shape
    for c in range(PACK_ROWS):
        word = hp_ref[pl.ds(c, tm, stride=PACK_ROWS), :]
        lo = lax.bitcast_convert_type(word << 16, f32)
        hi = lax.bitcast_convert_type(word & jnp.uint32(0xFFFF0000), f32)
        x_scr[:, c * LANES:(c + 1) * LANES] = lo.astype(bf16)
        x_scr[:, d // 2 + c * LANES:d // 2 + (c + 1) * LANES] = hi.astype(bf16)


def _mm_kernel(x_ref, w_ref, o_ref):
    o_ref[...] = _dot(x_ref[...], w_ref[...]).astype(o_ref.dtype)


def _mm(x, w, tm=1024, tn=1024):
    m, k = x.shape
    n = w.shape[1]
    return pl.pallas_call(
        _mm_kernel,
        grid=(n // tn, m // tm),
        in_specs=[pl.BlockSpec((tm, k), lambda j, i: (i, 0)),
                  pl.BlockSpec((k, tn), lambda j, i: (0, j))],
        out_specs=pl.BlockSpec((tm, tn), lambda j, i: (i, j)),
        out_shape=jax.ShapeDtypeStruct((m, n), f32),
        compiler_params=_params("parallel", "parallel"),
        name="mm",
    )(x, w)


def _mm_glu_kernel(x_ref, wa_ref, wg_ref, ba_ref, bg_ref, o_ref):
    x = x_ref[...].astype(bf16)
    a = _dot(x, wa_ref[...]) + ba_ref[...]
    g = _dot(x, wg_ref[...]) + bg_ref[...]
    o_ref[...] = (a * jax.nn.sigmoid(g)).astype(o_ref.dtype)


def _mm_glu(x, w, b, out_dtype, tm=1024, tn=512):
    m, k = x.shape
    n = w.shape[1] // 2
    nb = n // tn
    return pl.pallas_call(
        _mm_glu_kernel,
        grid=(nb, m // tm),
        in_specs=[pl.BlockSpec((tm, k), lambda j, i: (i, 0)),
                  pl.BlockSpec((k, tn), lambda j, i: (0, j)),
                  pl.BlockSpec((k, tn), lambda j, i: (0, j + nb)),
                  pl.BlockSpec((1, tn), lambda j, i: (0, j)),
                  pl.BlockSpec((1, tn), lambda j, i: (0, j + nb))],
        out_specs=pl.BlockSpec((tm, tn), lambda j, i: (i, j)),
        out_shape=jax.ShapeDtypeStruct((m, n), out_dtype),
        compiler_params=_params("parallel", "parallel"),
        name="mm_glu",
    )(x, w, w, b, b)


def _mm_res_ln_kernel(x_ref, w_ref, b_ref, h_ref, g_ref, beta_ref, of_ref, ob_ref, hp_ref):
    y = _dot(x_ref[...], w_ref[...]) + b_ref[...]
    out = _layer_norm(DN_ALPHA * h_ref[...] + y, g_ref[...], beta_ref[...])
    of_ref[...] = out
    ob_ref[...] = out.astype(bf16)
    _store_packed(out, hp_ref)


def _mm_res_ln(x, w, b, h, g, beta, tm=512):
    m, k = x.shape
    d = w.shape[1]
    row = lambda i: (i, 0)
    fixed = lambda i: (0, 0)
    return pl.pallas_call(
        _mm_res_ln_kernel,
        grid=(m // tm,),
        in_specs=[pl.BlockSpec((tm, k), row), pl.BlockSpec((k, d), fixed, pipeline_mode=pl.Buffered(1)),
                  pl.BlockSpec((1, d), fixed),
                  pl.BlockSpec((tm, d), row), pl.BlockSpec((1, d), fixed), pl.BlockSpec((1, d), fixed)],
        out_specs=[pl.BlockSpec((tm, d), row), pl.BlockSpec((tm, d), row),
                   pl.BlockSpec((tm * PACK_ROWS, LANES), row)],
        out_shape=[jax.ShapeDtypeStruct((m, d), f32), jax.ShapeDtypeStruct((m, d), bf16),
                   jax.ShapeDtypeStruct((m * PACK_ROWS, LANES), jnp.uint32)],
        compiler_params=_params("parallel"),
        name="mm_res_ln",
    )(x, w, b, h, g, beta)


def _cmul_add(xr, xi, ar, ai, sr, si):
    return xr + ar * sr - ai * si, xi + ar * si + ai * sr


def _expand_block_diag(compact_ref, rep_ref, w_scr, row_shift, col_shift):
    width = w_scr.shape[0]
    step = 2 * LANES
    for c0 in range(0, width, step):
        w = jnp.dot(compact_ref[0], rep_ref[:, c0:c0 + step], preferred_element_type=f32)
        row_g = (lax.broadcasted_iota(jnp.int32, w.shape, 0) >> row_shift) & (S5_CHUNK_GROUPS - 1)
        col_g = ((lax.broadcasted_iota(jnp.int32, w.shape, 1) + c0) >> col_shift) & (S5_CHUNK_GROUPS - 1)
        w_scr[:, c0:c0 + step] = jnp.where(row_g == col_g, w, 0.0).astype(bf16)


def _s5_kernel(u_ref, agg_ref, toe_ref, proj_ref, rep_state_ref, rep_chan_ref, dtab_ref, d_ref, z_ref,
               xb_scr, v_scr, wagg_scr, t_scr, wout_scr):
    m_rows = v_scr.shape[0]
    half = v_scr.shape[1] // 2
    ncol = half // LANES
    chan_shift = S5_GROUP.bit_length() - 1
    state_shift = S5_STATE.bit_length() - 1
    _expand_block_diag(agg_ref, rep_state_ref, wagg_scr, chan_shift, state_shift)
    _expand_block_diag(toe_ref, rep_chan_ref, t_scr, chan_shift, chan_shift)
    _expand_block_diag(proj_ref, rep_state_ref, wout_scr, chan_shift, state_shift)
    for i in range(S5_BLOCK):
        xb_scr[:, i * LANES:(i + 1) * LANES] = u_ref[pl.ds(i, m_rows, stride=S5_BLOCK), :].astype(bf16)
    xb = xb_scr[...]
    v_scr[...] = jnp.dot(xb, wagg_scr[...], preferred_element_type=f32)

    tab = dtab_ref[0]
    sub = lax.broadcasted_iota(jnp.int32, (SUBLANES, LANES), 0)

    def col(j, part):
        lo = part * half + j * LANES
        return slice(lo, lo + LANES)

    def bcast(row):
        return jnp.broadcast_to(row, (SUBLANES, LANES))

    def body(r, carry):
        r0 = pl.multiple_of(r * SUBLANES, SUBLANES)
        new = []
        for j in range(ncol):
            cr, ci = carry[2 * j], carry[2 * j + 1]
            xr = v_scr[pl.ds(r0, SUBLANES), col(j, 0)]
            xi = v_scr[pl.ds(r0, SUBLANES), col(j, 1)]
            for shift in (1, 2, 4):
                ar = bcast(tab[shift - 1:shift, col(j, 0)])
                ai = bcast(tab[shift - 1:shift, col(j, 1)])
                sr = jnp.where(sub >= shift, pltpu.roll(xr, shift, 0), 0.0)
                si = jnp.where(sub >= shift, pltpu.roll(xi, shift, 0), 0.0)
                xr, xi = _cmul_add(xr, xi, ar, ai, sr, si)
            xr, xi = _cmul_add(xr, xi, tab[:, col(j, 0)], tab[:, col(j, 1)], cr, ci)
            v_scr[pl.ds(r0, SUBLANES), col(j, 0)] = jnp.where(sub >= 1, pltpu.roll(xr, 1, 0), cr)
            v_scr[pl.ds(r0, SUBLANES), col(j, 1)] = jnp.where(sub >= 1, pltpu.roll(xi, 1, 0), ci)
            new.append(bcast(xr[SUBLANES - 1:SUBLANES, :]))
            new.append(bcast(xi[SUBLANES - 1:SUBLANES, :]))
        return tuple(new)

    zero = jnp.zeros((SUBLANES, LANES), f32)
    lax.fori_loop(0, m_rows // SUBLANES, body, (zero,) * (2 * ncol))

    sp = v_scr[...].astype(bf16)
    for i0 in range(0, S5_BLOCK, 2):
        cs = slice(i0 * LANES, (i0 + 2) * LANES)
        nt = (((1,), (1,)), ((), ()))
        y = (lax.dot_general(xb, t_scr[cs, :], nt, preferred_element_type=f32)
             + lax.dot_general(sp, wout_scr[cs, :], nt, preferred_element_type=f32))
        for i in (i0, i0 + 1):
            yi = y[:, (i - i0) * LANES:(i - i0 + 1) * LANES] + d_ref[...] * u_ref[pl.ds(i, m_rows, stride=S5_BLOCK), :]
            z_ref[pl.ds(i, m_rows, stride=S5_BLOCK), :] = 0.5 * yi * (1.0 + lax.erf(yi * (1.0 / math.sqrt(2.0))))


def _s5_core(u, agg, toe, proj, dtab, d_skip):
    n, d = u.shape
    nq = d // LANES
    m_rows = n // S5_BLOCK
    width = agg.shape[1]
    col = jnp.arange(width)[None, :]
    lane = jnp.arange(LANES)[:, None]
    half = width // 2
    rep_state = ((col // half == lane // S5_STATE) & (col % S5_STATE == lane % S5_STATE)).astype(bf16)
    rep_chan = ((col // LANES == lane // S5_GROUP) & (col % S5_GROUP == lane % S5_GROUP)).astype(bf16)
    wspec = pl.BlockSpec((1, width, LANES), lambda q: (q, 0, 0))
    rspec = pl.BlockSpec((LANES, width), lambda q: (0, 0))
    return pl.pallas_call(
        _s5_kernel,
        grid=(nq,),
        in_specs=[pl.BlockSpec((n, LANES), lambda q: (0, q)), wspec, wspec, wspec, rspec, rspec,
                  pl.BlockSpec((1, SUBLANES, width), lambda q: (q, 0, 0)),
                  pl.BlockSpec((1, LANES), lambda q: (0, q))],
        out_specs=pl.BlockSpec((n, LANES), lambda q: (0, q)),
        out_shape=jax.ShapeDtypeStruct((n, d), f32),
        scratch_shapes=[pltpu.VMEM((m_rows, width), bf16), pltpu.VMEM((m_rows, width), f32),
                        pltpu.VMEM((width, width), bf16), pltpu.VMEM((width, width), bf16),
                        pltpu.VMEM((width, width), bf16)],
        compiler_params=_params("parallel"),
        name="s5_core",
    )(u, agg, toe, proj, rep_state, rep_chan, dtab, d_skip)


def _s5_derive(b_re, b_im, c_re, c_im, a_re, a_im, log_step):
    highest = lax.Precision.HIGHEST
    g, p, c = b_re.shape
    nq = g // S5_CHUNK_GROUPS
    dt = jnp.exp(log_step)[:, None]
    mag = jnp.exp(a_re * dt)
    lb_re = mag * jnp.cos(a_im * dt)
    lb_im = mag * jnp.sin(a_im * dt)
    den = a_re * a_re + a_im * a_im
    n_re = lb_re - 1.0
    n_im = lb_im
    f_re = (n_re * a_re + n_im * a_im) / den
    f_im = (n_im * a_re - n_re * a_im) / den
    bb_re = f_re[..., None] * b_re - f_im[..., None] * b_im
    bb_im = f_re[..., None] * b_im + f_im[..., None] * b_re

    def powers(br, bi, count):
        rs, is_ = [br], [bi]
        for _ in range(count - 1):
            rs.append(rs[-1] * br - is_[-1] * bi)
            is_.append(rs[-2] * bi + is_[-1] * br)
        return rs, is_

    pr, pi = powers(lb_re, lb_im, S5_BLOCK)
    lam_r = [jnp.ones_like(lb_re)] + pr
    lam_i = [jnp.zeros_like(lb_im)] + pi
    width = S5_BLOCK * LANES

    def compact(w):
        w = w.reshape(S5_BLOCK, nq, S5_CHUNK_GROUPS * c, LANES)
        return w.transpose(1, 0, 2, 3).reshape(nq, width, LANES).astype(bf16)

    bt_re = bb_re.transpose(0, 2, 1)
    bt_im = bb_im.transpose(0, 2, 1)

    ar = jnp.stack([lam_r[S5_BLOCK - 1 - i] for i in range(S5_BLOCK)])[:, :, None, :]
    ai = jnp.stack([lam_i[S5_BLOCK - 1 - i] for i in range(S5_BLOCK)])[:, :, None, :]
    wagg = compact(jnp.concatenate([ar * bt_re - ai * bt_im, ar * bt_im + ai * bt_re], axis=-1))

    ar = jnp.stack(lam_r[1:S5_BLOCK + 1])[:, :, None, :]
    ai = jnp.stack(lam_i[1:S5_BLOCK + 1])[:, :, None, :]
    wout = compact(jnp.concatenate([c_re * ar - c_im * ai, -(c_re * ai + c_im * ar)], axis=-1))

    ar = jnp.stack(lam_r[:S5_BLOCK])[:, :, None, :]
    ai = jnp.stack(lam_i[:S5_BLOCK])[:, :, None, :]
    kj = (jnp.einsum('jgcp,gpd->jgcd', c_re * ar - c_im * ai, bb_re, precision=highest)
          - jnp.einsum('jgcp,gpd->jgcd', c_re * ai + c_im * ar, bb_im, precision=highest))
    none = jnp.zeros_like(kj[0])
    tmat = compact(jnp.stack([jnp.concatenate([kj[i - a] if i >= a else none for a in range(S5_BLOCK)], axis=-1)
                              for i in range(S5_BLOCK)]))

    dr, di = powers(lam_r[S5_BLOCK], lam_i[S5_BLOCK], SUBLANES)
    dtab = jnp.stack([jnp.stack(dr), jnp.stack(di)], axis=1)
    dtab = dtab.reshape(SUBLANES, 2, nq, S5_CHUNK_GROUPS * p).transpose(2, 0, 1, 3).reshape(nq, SUBLANES, width)
    return wagg, tmat, wout, dtab


def _conv_kernel(v_ref, halo_ref, w_ref, b_ref, g_ref, beta_ref, o_ref, ext_scr, rot_scr, acc_scr):
    tt = v_ref.shape[0]
    pad = halo_ref.shape[0]
    first = pl.program_id(0) == 0
    ext_scr[0:pad, :] = jnp.where(first, 0.0, halo_ref[...])
    ext_scr[pad:pad + tt, :] = v_ref[...]
    span = rot_scr.shape[1]
    for sh in range(1, SUBLANES):
        rot_scr[sh - 1] = ext_scr[sh:sh + span, :]
    rows, cols = 64, 128
    for c0 in range(0, v_ref.shape[1], cols):
        for r0 in range(0, tt, rows):
            acc = jnp.broadcast_to(b_ref[:, c0:c0 + cols], (rows, cols))
            for k in range(CONV_WIDTH):
                whole, sh = divmod(pad - (CONV_WIDTH - 1) + k, SUBLANES)
                off = whole * SUBLANES + r0
                if sh == 0:
                    win = ext_scr[off:off + rows, c0:c0 + cols]
                else:
                    win = rot_scr[sh - 1, off:off + rows, c0:c0 + cols]
                acc = acc + w_ref[k:k + 1, c0:c0 + cols] * win
            acc_scr[r0:r0 + rows, c0:c0 + cols] = acc
    y = _layer_norm(acc_scr[...], g_ref[...], beta_ref[...])
    o_ref[...] = (y * jax.nn.sigmoid(y)).astype(o_ref.dtype)


def _conv_module(v, w_dw, b_dw, g, beta, tt=128, pad=32):
    n, d = v.shape
    w_pad = jnp.zeros((pad, d), f32).at[:CONV_WIDTH].set(w_dw)
    ratio = tt // pad
    row = lambda i: (i, 0)
    fixed = lambda i: (0, 0)
    return pl.pallas_call(
        _conv_kernel,
        grid=(n // tt,),
        in_specs=[pl.BlockSpec((tt, d), row),
                  pl.BlockSpec((pad, d), lambda i: (jnp.maximum(i * ratio - 1, 0), 0)),
                  pl.BlockSpec((pad, d), fixed), pl.BlockSpec((1, d), fixed),
                  pl.BlockSpec((1, d), fixed), pl.BlockSpec((1, d), fixed)],
        out_specs=pl.BlockSpec((tt, d), row),
        out_shape=jax.ShapeDtypeStruct((n, d), bf16),
        scratch_shapes=[pltpu.VMEM((tt + pad, d), f32),
                        pltpu.VMEM((SUBLANES - 1, tt + pad - SUBLANES, d), f32),
                        pltpu.VMEM((tt, d), f32)],
        compiler_params=_params("parallel"),
        name="conv_module",
    )(v, v, w_pad, b_dw, g, beta)


def _pool_kernel(h_ref, halo_ref, w_ref, scale_ref, g_ref, beta_ref, of_ref, ob_ref, hp_ref, ext_scr):
    tt = h_ref.shape[0]
    pad = halo_ref.shape[0]
    i = pl.program_id(0)
    ext_scr[0:pad, :] = jnp.where(i == 0, 0.0, halo_ref[...])
    ext_scr[pad:pad + tt, :] = h_ref[...]
    t = i * tt + lax.broadcasted_iota(jnp.int32, (tt, 1), 0)
    ys = []
    for k, win in enumerate(POOL_WINDOWS):
        cs = slice(k * POOL_CH, (k + 1) * POOL_CH)
        x = h_ref[:, cs]
        acc = x
        for j in range(1, win):
            acc = acc + ext_scr[pad - j:pad - j + tt, cs]
        cnt = jnp.minimum(t + 1, win).astype(f32)
        ys.append(_dot(acc / cnt - x, w_ref[k]))
    y = jnp.concatenate(ys, axis=1) * scale_ref[...]
    out = _layer_norm(DN_ALPHA * h_ref[...] + y, g_ref[...], beta_ref[...])
    of_ref[...] = out
    ob_ref[...] = out.astype(bf16)
    _store_packed(out, hp_ref)


def _pool_layer(h, w_grp, scale, g, beta, tt=256, pad=16):
    n, d = h.shape
    ratio = tt // pad
    row = lambda i: (i, 0)
    fixed = lambda i: (0, 0)
    return pl.pallas_call(
        _pool_kernel,
        grid=(n // tt,),
        in_specs=[pl.BlockSpec((tt, d), row),
                  pl.BlockSpec((pad, d), lambda i: (jnp.maximum(i * ratio - 1, 0), 0)),
                  pl.BlockSpec(w_grp.shape, lambda i: (0, 0, 0)),
                  pl.BlockSpec((1, d), fixed), pl.BlockSpec((1, d), fixed), pl.BlockSpec((1, d), fixed)],
        out_specs=[pl.BlockSpec((tt, d), row), pl.BlockSpec((tt, d), row),
                   pl.BlockSpec((tt * PACK_ROWS, LANES), row)],
        out_shape=[jax.ShapeDtypeStruct((n, d), f32), jax.ShapeDtypeStruct((n, d), bf16),
                   jax.ShapeDtypeStruct((n * PACK_ROWS, LANES), jnp.uint32)],
        scratch_shapes=[pltpu.VMEM((tt + pad, d), f32)],
        compiler_params=_params("parallel"),
        name="pool_layer",
    )(h, h, w_grp, scale, g, beta)


R_E1, R_E2, R_G1, R_G2, R_RANK1, R_RANK2 = range(6)


def _router_kernel(h_ref, whi_ref, wlo_ref, b_ref, info_ref, info_t_ref, cnt_ref):
    tm = h_ref.shape[0]

    @pl.when(pl.program_id(0) == 0)
    def _():
        cnt_ref[...] = jnp.zeros_like(cnt_ref)

    h = h_ref[...]
    h_hi = h.astype(bf16)
    h_lo = (h - h_hi.astype(f32)).astype(bf16)
    logits = (jnp.dot(h_hi, whi_ref[...], preferred_element_type=f32)
              + jnp.dot(h_lo, whi_ref[...], preferred_element_type=f32)
              + jnp.dot(h_hi, wlo_ref[...], preferred_element_type=f32)) + b_ref[...]
    lane = lax.broadcasted_iota(jnp.int32, logits.shape, 1)
    real = lane < N_EXPERTS
    e = jnp.exp(logits - jnp.max(logits, axis=-1, keepdims=True))
    probs = e / jnp.sum(e, axis=-1, keepdims=True)

    a = probs
    b = pltpu.roll(probs, 1, 1)
    c = pltpu.roll(probs, 2, 1)
    d = pltpu.roll(probs, 3, 1)
    hi1, lo1 = jnp.maximum(a, b), jnp.minimum(a, b)
    hi2, lo2 = jnp.maximum(c, d), jnp.minimum(c, d)
    score = jnp.maximum(hi1, hi2) + jnp.maximum(jnp.minimum(hi1, hi2), jnp.maximum(lo1, lo2))
    best = None
    g_sel = None
    for grp in range(N_EXPERTS // EXPERTS_PER_GROUP):
        last = grp * EXPERTS_PER_GROUP + EXPERTS_PER_GROUP - 1
        s = jnp.max(jnp.where(lane == last, score, -1.0), axis=-1, keepdims=True)
        if grp == 0:
            best, g_sel = s, jnp.zeros_like(s, dtype=jnp.int32)
        else:
            better = s > best
            best = jnp.where(better, s, best)
            g_sel = jnp.where(better, grp, g_sel)

    in_grp = real & ((lane // EXPERTS_PER_GROUP) == g_sel)
    masked = jnp.where(in_grp, probs, -1.0)
    lane_f = lane.astype(f32)
    p1 = jnp.max(masked, axis=-1, keepdims=True)
    e1 = jnp.min(jnp.where(masked == p1, lane_f, float(LANES)), axis=-1, keepdims=True)
    masked2 = jnp.where(lane_f == e1, -2.0, masked)
    p2 = jnp.max(masked2, axis=-1, keepdims=True)
    e2 = jnp.min(jnp.where(masked2 == p2, lane_f, float(LANES)), axis=-1, keepdims=True)
    tot = p1 + p2

    oh1 = (lane_f == e1).astype(f32)
    oh2 = (lane_f == e2).astype(f32)
    both = oh1 + oh2
    ri = lax.broadcasted_iota(jnp.int32, (tm, tm), 0)
    ci = lax.broadcasted_iota(jnp.int32, (tm, tm), 1)
    tri = (ci < ri).astype(bf16)
    before = cnt_ref[0:1, :] + jnp.dot(tri, both.astype(bf16), preferred_element_type=f32)
    rank1 = jnp.sum(before * oh1, axis=-1, keepdims=True)
    rank2 = jnp.sum(before * oh2, axis=-1, keepdims=True)
    cnt_ref[...] = cnt_ref[...] + jnp.sum(both, axis=0, keepdims=True)

    info = jnp.zeros(logits.shape, f32)
    for slot, val in ((R_E1, e1), (R_E2, e2), (R_G1, p1 / tot), (R_G2, p2 / tot),
                      (R_RANK1, rank1), (R_RANK2, rank2)):
        info = jnp.where(lane == slot, val, info)
    info_ref[...] = info
    info_t_ref[...] = info.T[:SUBLANES, :]


def _router(h, router_w, router_b, tm=512):
    n, d = h.shape
    w = jnp.zeros((d, LANES), f32).at[:, :N_EXPERTS].set(router_w)
    w_hi = w.astype(bf16)
    w_lo = (w - w_hi.astype(f32)).astype(bf16)
    b = jnp.full((1, LANES), -1e30, f32).at[0, :N_EXPERTS].set(router_b)
    return pl.pallas_call(
        _router_kernel,
        grid=(n // tm,),
        in_specs=[pl.BlockSpec((tm, d), lambda i: (i, 0)), pl.BlockSpec((d, LANES), lambda i: (0, 0)),
                  pl.BlockSpec((d, LANES), lambda i: (0, 0)), pl.BlockSpec((1, LANES), lambda i: (0, 0))],
        out_specs=[pl.BlockSpec((tm, LANES), lambda i: (i, 0)), pl.BlockSpec((SUBLANES, tm), lambda i: (0, i)),
                   pl.BlockSpec((SUBLANES, LANES), lambda i: (0, 0))],
        out_shape=[jax.ShapeDtypeStruct((n, LANES), f32), jax.ShapeDtypeStruct((SUBLANES, n), f32),
                   jax.ShapeDtypeStruct((SUBLANES, LANES), f32)],
        compiler_params=_params("arbitrary"),
        name="router",
    )(h, w_hi, w_lo, b)


def _token_copy(src_ref, dst_ref, sem, src_tok, dst_tok, rows):
    src = pl.multiple_of(src_tok * rows, SUBLANES)
    dst = pl.multiple_of(dst_tok * rows, SUBLANES)
    return pltpu.make_async_copy(src_ref.at[pl.ds(src, rows)], dst_ref.at[pl.ds(dst, rows)], sem)


def _expert_kernel(blk_e_ref, blk_valid_ref, blk_first_ref, blk_next_ref, blk_wmask_ref, row_tok_ref,
                   hp_ref, wg_hbm, wu_hbm, wd_hbm, o_ref,
                   xg_scr, x_scr, stage_g, stage_u, stage_d, wg_scr, wu_scr, wd_scr, sems, wsems):
    i = pl.program_id(0)
    tm = x_scr.shape[0]
    nblk = pl.num_programs(0)

    def weight_copies(e):
        return (pltpu.make_async_copy(wg_hbm.at[e], stage_g, wsems.at[0]),
                pltpu.make_async_copy(wu_hbm.at[e], stage_u, wsems.at[1]),
                pltpu.make_async_copy(wd_hbm.at[e], stage_d, wsems.at[2]))

    @pl.when(i == 0)
    def _():
        for cp in weight_copies(blk_e_ref[0]):
            cp.start()

    def token_copy(slot, j, tok):
        return _token_copy(hp_ref, xg_scr.at[slot], sems.at[slot], tok, j, PACK_ROWS)

    def wait_gather(slot):
        def body(j0, carry):
            for k in range(GATHER_UNROLL):
                token_copy(slot, 0, 0).wait()
            return carry
        lax.fori_loop(0, tm // GATHER_UNROLL, body, 0)

    @pl.when(i == 0)
    def _():
        def body(j0, carry):
            for k in range(GATHER_UNROLL):
                j = j0 * GATHER_UNROLL + k
                token_copy(0, j, row_tok_ref[j]).start()
            return carry
        lax.fori_loop(0, tm // GATHER_UNROLL, body, 0)

    @pl.when(blk_first_ref[i] != 0)
    def _():
        for cp in weight_copies(blk_e_ref[i]):
            cp.wait()
        chunk = 256
        for stage, dst in ((stage_g, wg_scr), (stage_u, wu_scr), (stage_d, wd_scr)):
            for r in range(0, stage.shape[0], chunk):
                dst[r:r + chunk, :] = stage[r:r + chunk, :].astype(bf16)

    this_slot = i % 2
    next_slot = (i + 1) % 2

    @pl.when(blk_valid_ref[i] != 0)
    def _():
        wait_gather(this_slot)
        _load_packed(xg_scr.at[this_slot], x_scr)
        nxt_blk = jnp.minimum(i + 1, nblk - 1)
        for j in range(tm):
            token_copy(next_slot, j, row_tok_ref[nxt_blk * tm + j]).start()
        x = x_scr[...]
        gate = _dot(x, wg_scr[...])
        hid = gate * jax.nn.sigmoid(gate) * _dot(x, wu_scr[...])
        y = _dot(hid, wd_scr[...])
        for c in range(WIDE_ROWS):
            o_ref[pl.ds(c, tm, stride=WIDE_ROWS), :] = y[:, c * LANES:(c + 1) * LANES]

        mask = blk_wmask_ref[i]
        for k, cp in enumerate(weight_copies(jnp.maximum(blk_next_ref[i], 0))):
            @pl.when(((mask >> k) & 1) != 0)
            def _(cp=cp):
                cp.start()

        @pl.when(i == nblk - 1)
        def _():
            wait_gather(next_slot)

    @pl.when(blk_valid_ref[i] == 0)
    def _():
        o_ref[...] = jnp.zeros_like(o_ref)

        @pl.when(blk_valid_ref[jnp.maximum(i - 1, 0)] != 0)
        def _():
            wait_gather(this_slot)


def _experts(hp, row_tok, blk_e, blk_valid, blk_first, blk_next, blk_wmask, w_gate, w_up, w_down):
    cap = row_tok.shape[0]
    _, d, de = w_gate.shape
    row = lambda i, *_: (i, 0)
    any_spec = pl.BlockSpec(memory_space=pl.ANY)
    return pl.pallas_call(
        _expert_kernel,
        grid_spec=pltpu.PrefetchScalarGridSpec(
            num_scalar_prefetch=6, grid=(cap // MOE_ROWS,),
            in_specs=[any_spec, any_spec, any_spec, any_spec],
            out_specs=pl.BlockSpec((MOE_ROWS * WIDE_ROWS, LANES), row),
            scratch_shapes=[pltpu.VMEM((2, MOE_ROWS * PACK_ROWS, LANES), jnp.uint32),
                            pltpu.VMEM((MOE_ROWS, d), bf16),
                            pltpu.VMEM((d, de), f32), pltpu.VMEM((d, de), f32), pltpu.VMEM((de, d), f32),
                            pltpu.VMEM((d, de), bf16), pltpu.VMEM((d, de), bf16), pltpu.VMEM((de, d), bf16),
                            pltpu.SemaphoreType.DMA((2,)), pltpu.SemaphoreType.DMA((3,))]),
        out_shape=jax.ShapeDtypeStruct((cap * WIDE_ROWS, LANES), f32),
        compiler_params=_params("arbitrary"),
        name="experts",
    )(blk_e, blk_valid, blk_first, blk_next, blk_wmask, row_tok, hp, w_gate, w_up, w_down)


def _load_wide(y_ref, tm):
    return jnp.concatenate([y_ref[pl.ds(c, tm, stride=WIDE_ROWS), :] for c in range(WIDE_ROWS)], axis=1)


def _moe_ln_kernel(dest_ref, h_ref, y_hbm, info_ref, g_ref, beta_ref, of_ref, ob_ref, y1_scr, y2_scr, sems):
    i = pl.program_id(0)
    ntile = pl.num_programs(0)
    tm = h_ref.shape[0]
    n_tok = ntile * tm

    def token_copy(buf, which, j, row):
        scr = (y1_scr, y2_scr)[which]
        return _token_copy(y_hbm, scr.at[buf], sems.at[buf], row, j, WIDE_ROWS)

    def wait_gather(buf):
        def body(j0, carry):
            for _ in range(2 * GATHER_UNROLL):
                token_copy(buf, 0, 0, 0).wait()
            return carry
        lax.fori_loop(0, tm // GATHER_UNROLL, body, 0)

    def start_gather(tile, buf):
        def body(j0, carry):
            for which in range(2):
                for k in range(GATHER_UNROLL):
                    j = j0 * GATHER_UNROLL + k
                    token_copy(buf, which, j, dest_ref[which * n_tok + tile * tm + j]).start()
            return carry
        lax.fori_loop(0, tm // GATHER_UNROLL, body, 0)

    this_buf = i % 2

    @pl.when(i == 0)
    def _():
        start_gather(0, 0)

    @pl.when(i + 1 < ntile)
    def _():
        start_gather(i + 1, (i + 1) % 2)

    wait_gather(this_buf)
    info = info_ref[...]
    g1 = info[:, R_G1:R_G1 + 1]
    g2 = info[:, R_G2:R_G2 + 1]
    f = g1 * _load_wide(y1_scr.at[this_buf], tm) + g2 * _load_wide(y2_scr.at[this_buf], tm)
    out = _layer_norm(DN_ALPHA * h_ref[...] + f, g_ref[...], beta_ref[...])
    of_ref[...] = out
    ob_ref[...] = out.astype(bf16)


def _moe_ln(h, y, dest, info, g, beta, tm=256):
    n, d = h.shape
    row = lambda i, *_: (i, 0)
    fixed = lambda i, *_: (0, 0)
    return pl.pallas_call(
        _moe_ln_kernel,
        grid_spec=pltpu.PrefetchScalarGridSpec(
            num_scalar_prefetch=1, grid=(n // tm,),
            in_specs=[pl.BlockSpec((tm, d), row), pl.BlockSpec(memory_space=pl.ANY),
                      pl.BlockSpec((tm, LANES), row), pl.BlockSpec((1, d), fixed), pl.BlockSpec((1, d), fixed)],
            out_specs=[pl.BlockSpec((tm, d), row), pl.BlockSpec((tm, d), row)],
            scratch_shapes=[pltpu.VMEM((2, tm * WIDE_ROWS, LANES), f32), pltpu.VMEM((2, tm * WIDE_ROWS, LANES), f32),
                            pltpu.SemaphoreType.DMA((2,))]),
        out_shape=[jax.ShapeDtypeStruct((n, d), f32), jax.ShapeDtypeStruct((n, d), bf16)],
        compiler_params=_params("arbitrary"),
        name="moe_ln",
    )(dest, h, y, info, g, beta)


def _moe_layer(hf, hp, router_w, router_b, w_gate, w_up, w_down, expert_base, g, beta):
    n, d = hf.shape
    info, info_t, cnt = _router(hf, router_w, router_b)
    experts = info_t[R_E1:R_E2 + 1].astype(jnp.int32)
    rank = info_t[R_RANK1:R_RANK2 + 1].astype(jnp.int32)
    counts = cnt[0, :N_EXPERTS].astype(jnp.int32)
    pcounts = ((counts + MOE_ROWS - 1) // MOE_ROWS) * MOE_ROWS
    pends = jnp.cumsum(pcounts)
    pstarts = pends - pcounts
    dest = rank
    for e in range(N_EXPERTS):
        dest = dest + jnp.where(experts == e, pstarts[e], 0)
    cap = 2 * n + N_EXPERTS * MOE_ROWS
    dest = dest.reshape(-1)
    tok = jnp.tile(jnp.arange(n, dtype=jnp.int32), 2)
    row_tok = (jnp.arange(cap, dtype=jnp.int32) % n).at[dest].set(tok, unique_indices=True)
    blk_start = jnp.arange(cap // MOE_ROWS, dtype=jnp.int32) * MOE_ROWS
    blk_e = jnp.sum((blk_start[:, None] >= pends[None, :]).astype(jnp.int32), axis=1)
    blk_e = jnp.minimum(blk_e, N_EXPERTS - 1)
    blk_valid = (blk_start < pends[-1]).astype(jnp.int32)
    last_e = jnp.max(jnp.where(counts > 0, jnp.arange(N_EXPERTS, dtype=jnp.int32), 0))
    blk_e = jnp.where(blk_valid != 0, blk_e, last_e)
    blk_first = blk_valid * jnp.concatenate([jnp.ones((1,), jnp.int32), (blk_e[1:] != blk_e[:-1]).astype(jnp.int32)])
    ids = jnp.arange(N_EXPERTS, dtype=jnp.int32)
    later = (ids[None, :] > ids[:, None]) & (counts[None, :] > 0)
    next_e = jnp.min(jnp.where(later, ids[None, :], N_EXPERTS), axis=1)
    next_e = jnp.where(next_e < N_EXPERTS, next_e + expert_base, -1).astype(jnp.int32)
    mine = blk_e[:, None] == ids[None, :]
    blk_next = jnp.sum(jnp.where(mine, next_e[None, :], 0), axis=1)
    run_pos = blk_start // MOE_ROWS - jnp.sum(jnp.where(mine, pstarts[None, :] // MOE_ROWS, 0), axis=1)
    run_len = jnp.sum(jnp.where(mine, pcounts[None, :] // MOE_ROWS, 0), axis=1)
    mats = jnp.arange(3, dtype=jnp.int32)
    start_at = jnp.minimum(mats[None, :], run_len[:, None] - 1)
    starts = (run_pos[:, None] == start_at) & (blk_next[:, None] >= 0) & (blk_valid[:, None] != 0)
    blk_wmask = jnp.sum(starts.astype(jnp.int32) << mats[None, :], axis=1)

    y = _experts(hp, row_tok, blk_e + expert_base, blk_valid, blk_first, blk_next, blk_wmask,
                 w_gate, w_up, w_down)
    return _moe_ln(hf, y, dest, info, g, beta)


def kernel(x, s5_w_in, s5_b_re, s5_b_im, s5_c_re, s5_c_im, s5_a_re, s5_a_im, s5_log_step, s5_d, s5_w_glu, s5_w_out, cv_w_pw1, cv_b_pw1, cv_w_dw, cv_b_dw, cv_ln_g, cv_ln_b, cv_w_pw2, cv_b_pw2, pl_w, pl_scale, router_w, router_b, moe_w_gate, moe_w_up, moe_w_down, ln_mix_g, ln_mix_b, ln_ffn_g, ln_ffn_b):
    bsz, seq, d = x.shape
    hf = x.reshape(bsz * seq, d)
    hb = hf.astype(bf16)
    row = lambda v: v.reshape(1, -1)
    w_gate = moe_w_gate.reshape((-1,) + moe_w_gate.shape[2:])
    w_up = moe_w_up.reshape((-1,) + moe_w_up.shape[2:])
    w_down = moe_w_down.reshape((-1,) + moe_w_down.shape[2:])

    for i in range(DEPTH):
        mixer, j = i % N_MIXERS, i // N_MIXERS
        g_mix, b_mix = row(ln_mix_g[i]), row(ln_mix_b[i])
        if mixer == 0:
            u = _mm(hb, s5_w_in[j].astype(bf16))
            wagg, tmat, wout, dtab = _s5_derive(s5_b_re[j], s5_b_im[j], s5_c_re[j], s5_c_im[j],
                                                s5_a_re[j], s5_a_im[j], s5_log_step[j])
            z = _s5_core(u, wagg, tmat, wout, dtab, row(s5_d[j]))
            v = _mm_glu(z, s5_w_glu[j].astype(bf16), jnp.zeros((1, 2 * d), f32), bf16)
            hf, hb, hp = _mm_res_ln(v, s5_w_out[j].astype(bf16), jnp.zeros((1, d), f32), hf, g_mix, b_mix)
        elif mixer == 1:
            v = _mm_glu(hb, cv_w_pw1[j].astype(bf16), row(cv_b_pw1[j]), f32)
            cv = _conv_module(v, cv_w_dw[j], row(cv_b_dw[j]), row(cv_ln_g[j]), row(cv_ln_b[j]))
            hf, hb, hp = _mm_res_ln(cv, cv_w_pw2[j].astype(bf16), row(cv_b_pw2[j]), hf, g_mix, b_mix)
        else:
            hf, hb, hp = _pool_layer(hf, pl_w[j].astype(bf16), row(pl_scale[j]), g_mix, b_mix)
        hf, hb = _moe_layer(hf, hp, router_w, router_b, w_gate, w_up, w_down, i * N_EXPERTS,
                            row(ln_ffn_g[i]), row(ln_ffn_b[i]))
    return hf.reshape(bsz, seq, d)
```

```python
import math

import jax
import jax.numpy as jnp
from jax import lax
from jax.experimental import pallas as pl
from jax.experimental.pallas import tpu as pltpu

f32 = jnp.float32
bf16 = jnp.bfloat16

D_MODEL = 2048
DEPTH = 4
N_MIXERS = 3
S5_GROUP = 16
S5_STATE = 64
CONV_WIDTH = 31
POOL_WINDOWS = (2, 4, 8, 16)
POOL_CH = D_MODEL // len(POOL_WINDOWS)
N_EXPERTS = 16
EXPERTS_PER_GROUP = 4
D_EXPERT = D_MODEL // 2
DN_ALPHA = (2 * DEPTH) ** 0.25
LN_EPS = 1e-5

LANES = 128
SUBLANES = 8
VMEM_LIMIT = 56 * 1024 * 1024
S5_BLOCK = 8
S5_CHUNK_GROUPS = LANES // S5_GROUP
MOE_ROWS = 256
GATHER_UNROLL = 8
PACK_ROWS = D_MODEL // 2 // LANES


def _params(*sem):
    return pltpu.CompilerParams(dimension_semantics=sem, vmem_limit_bytes=VMEM_LIMIT)


def _layer_norm(r, g, b):
    mu = jnp.mean(r, axis=-1, keepdims=True)
    xc = r - mu
    var = jnp.mean(xc * xc, axis=-1, keepdims=True)
    return xc * lax.rsqrt(var + LN_EPS) * g + b


def _dot(a, b):
    return jnp.dot(a.astype(bf16), b.astype(bf16), preferred_element_type=f32)


def _store_packed(out, hp_ref):
    tm, d = out.shape
    for c in range(PACK_ROWS):
        lo = out[:, c * LANES:(c + 1) * LANES].astype(bf16).astype(f32)
        hi = out[:, d // 2 + c * LANES:d // 2 + (c + 1) * LANES].astype(bf16).astype(f32)
        word = (lax.bitcast_convert_type(lo, jnp.uint32) >> 16) | lax.bitcast_convert_type(hi, jnp.uint32)
        hp_ref[pl.ds(c, tm, stride=PACK_ROWS), :] = word


def _load_packed(hp_ref, x_scr):
    tm, d = x_scr.shape
    for c in range(PACK_ROWS):
        word = hp_ref[pl.ds(c, tm, stride=PACK_ROWS), :]
        lo = lax.bitcast_convert_type(word << 16, f32)
        hi = lax.bitcast_convert_type(word & jnp.uint32(0xFFFF0000), f32)
        x_scr[:, c * LANES:(c + 1) * LANES] = lo.astype(bf16)
        x_scr[:, d // 2 + c * LANES:d // 2 + (c + 1) * LANES] = hi.astype(bf16)


def _mm_kernel(x_ref, w_ref, o_ref):
    o_ref[...] = _dot(x_ref[...], w_ref[...]).astype(o_ref.dtype)


def _mm(x, w, tm=1024, tn=1024):
    m, k = x.shape
    n = w.shape[1]
    return pl.pallas_call(
        _mm_kernel,
        grid=(n // tn, m // tm),
        in_specs=[pl.BlockSpec((tm, k), lambda j, i: (i, 0)),
                  pl.BlockSpec((k, tn), lambda j, i: (0, j))],
        out_specs=pl.BlockSpec((tm, tn), lambda j, i: (i, j)),
        out_shape=jax.ShapeDtypeStruct((m, n), f32),
        compiler_params=_params("parallel", "parallel"),
        name="mm",
    )(x, w)


def _mm_glu_kernel(x_ref, wa_ref, wg_ref, ba_ref, bg_ref, o_ref):
    x = x_ref[...].astype(bf16)
    a = _dot(x, wa_ref[...]) + ba_ref[...]
    g = _dot(x, wg_ref[...]) + bg_ref[...]
    o_ref[...] = (a * jax.nn.sigmoid(g)).astype(o_ref.dtype)


def _mm_glu(x, w, b, out_dtype, tm=1024, tn=512):
    m, k = x.shape
    n = w.shape[1] // 2
    nb = n // tn
    return pl.pallas_call(
        _mm_glu_kernel,
        grid=(nb, m // tm),
        in_specs=[pl.BlockSpec((tm, k), lambda j, i: (i, 0)),
                  pl.BlockSpec((k, tn), lambda j, i: (0, j)),
                  pl.BlockSpec((k, tn), lambda j, i: (0, j + nb)),
                  pl.BlockSpec((1, tn), lambda j, i: (0, j)),
                  pl.BlockSpec((1, tn), lambda j, i: (0, j + nb))],
        out_specs=pl.BlockSpec((tm, tn), lambda j, i: (i, j)),
        out_shape=jax.ShapeDtypeStruct((m, n), out_dtype),
        compiler_params=_params("parallel", "parallel"),
        name="mm_glu",
    )(x, w, w, b, b)


def _mm_res_ln_kernel(x_ref, w_ref, b_ref, h_ref, g_ref, beta_ref, of_ref, ob_ref, hp_ref):
    y = _dot(x_ref[...], w_ref[...]) + b_ref[...]
    out = _layer_norm(DN_ALPHA * h_ref[...] + y, g_ref[...], beta_ref[...])
    of_ref[...] = out
    ob_ref[...] = out.astype(bf16)
    _store_packed(out, hp_ref)


def _mm_res_ln(x, w, b, h, g, beta, tm=512):
    m, k = x.shape
    d = w.shape[1]
    row = lambda i: (i, 0)
    fixed = lambda i: (0, 0)
    return pl.pallas_call(
        _mm_res_ln_kernel,
        grid=(m // tm,),
        in_specs=[pl.BlockSpec((tm, k), row), pl.BlockSpec((k, d), fixed, pipeline_mode=pl.Buffered(1)),
                  pl.BlockSpec((1, d), fixed),
                  pl.BlockSpec((tm, d), row), pl.BlockSpec((1, d), fixed), pl.BlockSpec((1, d), fixed)],
        out_specs=[pl.BlockSpec((tm, d), row), pl.BlockSpec((tm, d), row),
                   pl.BlockSpec((tm * PACK_ROWS, LANES), row)],
        out_shape=[jax.ShapeDtypeStruct((m, d), f32), jax.ShapeDtypeStruct((m, d), bf16),
                   jax.ShapeDtypeStruct((m * PACK_ROWS, LANES), jnp.uint32)],
        compiler_params=_params("parallel"),
        name="mm_res_ln",
    )(x, w, b, h, g, beta)


def _cmul_add(xr, xi, ar, ai, sr, si):
    return xr + ar * sr - ai * si, xi + ar * si + ai * sr


def _expand_block_diag(compact_ref, rep_ref, w_scr, row_shift, col_shift):
    width = w_scr.shape[0]
    step = 2 * LANES
    for c0 in range(0, width, step):
        w = jnp.dot(compact_ref[0], rep_ref[:, c0:c0 + step], preferred_element_type=f32)
        row_g = (lax.broadcasted_iota(jnp.int32, w.shape, 0) >> row_shift) & (S5_CHUNK_GROUPS - 1)
        col_g = ((lax.broadcasted_iota(jnp.int32, w.shape, 1) + c0) >> col_shift) & (S5_CHUNK_GROUPS - 1)
        w_scr[:, c0:c0 + step] = jnp.where(row_g == col_g, w, 0.0).astype(bf16)


def _s5_kernel(u_ref, agg_ref, toe_ref, proj_ref, rep_state_ref, rep_chan_ref, dtab_ref, d_ref, z_ref,
               xb_scr, v_scr, wagg_scr, t_scr, wout_scr):
    m_rows = v_scr.shape[0]
    half = v_scr.shape[1] // 2
    ncol = half // LANES
    chan_shift = S5_GROUP.bit_length() - 1
    state_shift = S5_STATE.bit_length() - 1
    _expand_block_diag(agg_ref, rep_state_ref, wagg_scr, chan_shift, state_shift)
    _expand_block_diag(toe_ref, rep_chan_ref, t_scr, chan_shift, chan_shift)
    _expand_block_diag(proj_ref, rep_state_ref, wout_scr, chan_shift, state_shift)
    for i in range(S5_BLOCK):
        xb_scr[:, i * LANES:(i + 1) * LANES] = u_ref[pl.ds(i, m_rows, stride=S5_BLOCK), :].astype(bf16)
    xb = xb_scr[...]
    v_scr[...] = jnp.dot(xb, wagg_scr[...], preferred_element_type=f32)

    tab = dtab_ref[0]
    sub = lax.broadcasted_iota(jnp.int32, (SUBLANES, LANES), 0)

    def col(j, part):
        lo = part * half + j * LANES
        return slice(lo, lo + LANES)

    def bcast(row):
        return jnp.broadcast_to(row, (SUBLANES, LANES))

    def body(r, carry):
        r0 = pl.multiple_of(r * SUBLANES, SUBLANES)
        new = []
        for j in range(ncol):
            cr, ci = carry[2 * j], carry[2 * j + 1]
            xr = v_scr[pl.ds(r0, SUBLANES), col(j, 0)]
            xi = v_scr[pl.ds(r0, SUBLANES), col(j, 1)]
            for shift in (1, 2, 4):
                ar = bcast(tab[shift - 1:shift, col(j, 0)])
                ai = bcast(tab[shift - 1:shift, col(j, 1)])
                sr = jnp.where(sub >= shift, pltpu.roll(xr, shift, 0), 0.0)
                si = jnp.where(sub >= shift, pltpu.roll(xi, shift, 0), 0.0)
                xr, xi = _cmul_add(xr, xi, ar, ai, sr, si)
            xr, xi = _cmul_add(xr, xi, tab[:, col(j, 0)], tab[:, col(j, 1)], cr, ci)
            v_scr[pl.ds(r0, SUBLANES), col(j, 0)] = jnp.where(sub >= 1, pltpu.roll(xr, 1, 0), cr)
            v_scr[pl.ds(r0, SUBLANES), col(j, 1)] = jnp.where(sub >= 1, pltpu.roll(xi, 1, 0), ci)
            new.append(bcast(xr[SUBLANES - 1:SUBLANES, :]))
            new.append(bcast(xi[SUBLANES - 1:SUBLANES, :]))
        return tuple(new)

    zero = jnp.zeros((SUBLANES, LANES), f32)
    lax.fori_loop(0, m_rows // SUBLANES, body, (zero,) * (2 * ncol))

    sp = v_scr[...].astype(bf16)
    for i0 in range(0, S5_BLOCK, 2):
        cs = slice(i0 * LANES, (i0 + 2) * LANES)
        nt = (((1,), (1,)), ((), ()))
        y = (lax.dot_general(xb, t_scr[cs, :], nt, preferred_element_type=f32)
             + lax.dot_general(sp, wout_scr[cs, :], nt, preferred_element_type=f32))
        for i in (i0, i0 + 1):
            yi = y[:, (i - i0) * LANES:(i - i0 + 1) * LANES] + d_ref[...] * u_ref[pl.ds(i, m_rows, stride=S5_BLOCK), :]
            z_ref[pl.ds(i, m_rows, stride=S5_BLOCK), :] = 0.5 * yi * (1.0 + lax.erf(yi * (1.0 / math.sqrt(2.0))))


def _s5_core(u, agg, toe, proj, dtab, d_skip):
    n, d = u.shape
    nq = d // LANES
    m_rows = n // S5_BLOCK
    width = agg.shape[1]
    col = jnp.arange(width)[None, :]
    lane = jnp.arange(LANES)[:, None]
    half = width // 2
    rep_state = ((col // half == lane // S5_STATE) & (col % S5_STATE == lane % S5_STATE)).astype(bf16)
    rep_chan = ((col // LANES == lane // S5_GROUP) & (col % S5_GROUP == lane % S5_GROUP)).astype(bf16)
    wspec = pl.BlockSpec((1, width, LANES), lambda q: (q, 0, 0))
    rspec = pl.BlockSpec((LANES, width), lambda q: (0, 0))
    return pl.pallas_call(
        _s5_kernel,
        grid=(nq,),
        in_specs=[pl.BlockSpec((n, LANES), lambda q: (0, q)), wspec, wspec, wspec, rspec, rspec,
                  pl.BlockSpec((1, SUBLANES, width), lambda q: (q, 0, 0)),
                  pl.BlockSpec((1, LANES), lambda q: (0, q))],
        out_specs=pl.BlockSpec((n, LANES), lambda q: (0, q)),
        out_shape=jax.ShapeDtypeStruct((n, d), f32),
        scratch_shapes=[pltpu.VMEM((m_rows, width), bf16), pltpu.VMEM((m_rows, width), f32),
                        pltpu.VMEM((width, width), bf16), pltpu.VMEM((width, width), bf16),
                        pltpu.VMEM((width, width), bf16)],
        compiler_params=_params("parallel"),
        name="s5_core",
    )(u, agg, toe, proj, rep_state, rep_chan, dtab, d_skip)


def _s5_derive(b_re, b_im, c_re, c_im, a_re, a_im, log_step):
    highest = lax.Precision.HIGHEST
    g, p, c = b_re.shape
    nq = g // S5_CHUNK_GROUPS
    dt = jnp.exp(log_step)[:, None]
    mag = jnp.exp(a_re * dt)
    lb_re = mag * jnp.cos(a_im * dt)
    lb_im = mag * jnp.sin(a_im * dt)
    den = a_re * a_re + a_im * a_im
    n_re = lb_re - 1.0
    n_im = lb_im
    f_re = (n_re * a_re + n_im * a_im) / den
    f_im = (n_im * a_re - n_re * a_im) / den
    bb_re = f_re[..., None] * b_re - f_im[..., None] * b_im
    bb_im = f_re[..., None] * b_im + f_im[..., None] * b_re

    def powers(br, bi, count):
        rs, is_ = [br], [bi]
        for _ in range(count - 1):
            rs.append(rs[-1] * br - is_[-1] * bi)
            is_.append(rs[-2] * bi + is_[-1] * br)
        return rs, is_

    pr, pi = powers(lb_re, lb_im, S5_BLOCK)
    lam_r = [jnp.ones_like(lb_re)] + pr
    lam_i = [jnp.zeros_like(lb_im)] + pi
    width = S5_BLOCK * LANES

    def compact(w):
        w = w.reshape(S5_BLOCK, nq, S5_CHUNK_GROUPS * c, LANES)
        return w.transpose(1, 0, 2, 3).reshape(nq, width, LANES).astype(bf16)

    bt_re = bb_re.transpose(0, 2, 1)
    bt_im = bb_im.transpose(0, 2, 1)

    ar = jnp.stack([lam_r[S5_BLOCK - 1 - i] for i in range(S5_BLOCK)])[:, :, None, :]
    ai = jnp.stack([lam_i[S5_BLOCK - 1 - i] for i in range(S5_BLOCK)])[:, :, None, :]
    wagg = compact(jnp.concatenate([ar * bt_re - ai * bt_im, ar * bt_im + ai * bt_re], axis=-1))

    ar = jnp.stack(lam_r[1:S5_BLOCK + 1])[:, :, None, :]
    ai = jnp.stack(lam_i[1:S5_BLOCK + 1])[:, :, None, :]
    wout = compact(jnp.concatenate([c_re * ar - c_im * ai, -(c_re * ai + c_im * ar)], axis=-1))

    ar = jnp.stack(lam_r[:S5_BLOCK])[:, :, None, :]
    ai = jnp.stack(lam_i[:S5_BLOCK])[:, :, None, :]
    kj = (jnp.einsum('jgcp,gpd->jgcd', c_re * ar - c_im * ai, bb_re, precision=highest)
          - jnp.einsum('jgcp,gpd->jgcd', c_re * ai + c_im * ar, bb_im, precision=highest))
    none = jnp.zeros_like(kj[0])
    tmat = compact(jnp.stack([jnp.concatenate([kj[i - a] if i >= a else none for a in range(S5_BLOCK)], axis=-1)
                              for i in range(S5_BLOCK)]))

    dr, di = powers(lam_r[S5_BLOCK], lam_i[S5_BLOCK], SUBLANES)
    dtab = jnp.stack([jnp.stack(dr), jnp.stack(di)], axis=1)
    dtab = dtab.reshape(SUBLANES, 2, nq, S5_CHUNK_GROUPS * p).transpose(2, 0, 1, 3).reshape(nq, SUBLANES, width)
    return wagg, tmat, wout, dtab


def _conv_kernel(v_ref, halo_ref, w_ref, b_ref, g_ref, beta_ref, o_ref, ext_scr, rot_scr, acc_scr):
    tt = v_ref.shape[0]
    pad = halo_ref.shape[0]
    first = pl.program_id(0) == 0
    ext_scr[0:pad, :] = jnp.where(first, 0.0, halo_ref[...])
    ext_scr[pad:pad + tt, :] = v_ref[...]
    span = rot_scr.shape[1]
    for sh in range(1, SUBLANES):
        rot_scr[sh - 1] = ext_scr[sh:sh + span, :]
    rows, cols = 64, 128
    for c0 in range(0, v_ref.shape[1], cols):
        for r0 in range(0, tt, rows):
            acc = jnp.broadcast_to(b_ref[:, c0:c0 + cols], (rows, cols))
            for k in range(CONV_WIDTH):
                whole, sh = divmod(pad - (CONV_WIDTH - 1) + k, SUBLANES)
                off = whole * SUBLANES + r0
                if sh == 0:
                    win = ext_scr[off:off + rows, c0:c0 + cols]
                else:
                    win = rot_scr[sh - 1, off:off + rows, c0:c0 + cols]
                acc = acc + w_ref[k:k + 1, c0:c0 + cols] * win
            acc_scr[r0:r0 + rows, c0:c0 + cols] = acc
    y = _layer_norm(acc_scr[...], g_ref[...], beta_ref[...])
    o_ref[...] = (y * jax.nn.sigmoid(y)).astype(o_ref.dtype)


def _conv_module(v, w_dw, b_dw, g, beta, tt=128, pad=32):
    n, d = v.shape
    w_pad = jnp.zeros((pad, d), f32).at[:CONV_WIDTH].set(w_dw)
    ratio = tt // pad
    row = lambda i: (i, 0)
    fixed = lambda i: (0, 0)
    return pl.pallas_call(
        _conv_kernel,
        grid=(n // tt,),
        in_specs=[pl.BlockSpec((tt, d), row),
                  pl.BlockSpec((pad, d), lambda i: (jnp.maximum(i * ratio - 1, 0), 0)),
                  pl.BlockSpec((pad, d), fixed), pl.BlockSpec((1, d), fixed),
                  pl.BlockSpec((1, d), fixed), pl.BlockSpec((1, d), fixed)],
        out_specs=pl.BlockSpec((tt, d), row),
        out_shape=jax.ShapeDtypeStruct((n, d), bf16),
        scratch_shapes=[pltpu.VMEM((tt + pad, d), f32),
                        pltpu.VMEM((SUBLANES - 1, tt + pad - SUBLANES, d), f32),
                        pltpu.VMEM((tt, d), f32)],
        compiler_params=_params("parallel"),
        name="conv_module",
    )(v, v, w_pad, b_dw, g, beta)


def _pool_kernel(h_ref, halo_ref, w_ref, scale_ref, g_ref, beta_ref, of_ref, ob_ref, hp_ref, ext_scr):
    tt = h_ref.shape[0]
    pad = halo_ref.shape[0]
    i = pl.program_id(0)
    ext_scr[0:pad, :] = jnp.where(i == 0, 0.0, halo_ref[...])
    ext_scr[pad:pad + tt, :] = h_ref[...]
    t = i * tt + lax.broadcasted_iota(jnp.int32, (tt, 1), 0)
    ys = []
    for k, win in enumerate(POOL_WINDOWS):
        cs = slice(k * POOL_CH, (k + 1) * POOL_CH)
        x = h_ref[:, cs]
        acc = x
        for j in range(1, win):
            acc = acc + ext_scr[pad - j:pad - j + tt, cs]
        cnt = jnp.minimum(t + 1, win).astype(f32)
        ys.append(_dot(acc / cnt - x, w_ref[k]))
    y = jnp.concatenate(ys, axis=1) * scale_ref[...]
    out = _layer_norm(DN_ALPHA * h_ref[...] + y, g_ref[...], beta_ref[...])
    of_ref[...] = out
    ob_ref[...] = out.astype(bf16)
    _store_packed(out, hp_ref)


def _pool_layer(h, w_grp, scale, g, beta, tt=256, pad=16):
    n, d = h.shape
    ratio = tt // pad
    row = lambda i: (i, 0)
    fixed = lambda i: (0, 0)
    return pl.pallas_call(
        _pool_kernel,
        grid=(n // tt,),
        in_specs=[pl.BlockSpec((tt, d), row),
                  pl.BlockSpec((pad, d), lambda i: (jnp.maximum(i * ratio - 1, 0), 0)),
                  pl.BlockSpec(w_grp.shape, lambda i: (0, 0, 0)),
                  pl.BlockSpec((1, d), fixed), pl.BlockSpec((1, d), fixed), pl.BlockSpec((1, d), fixed)],
        out_specs=[pl.BlockSpec((tt, d), row), pl.BlockSpec((tt, d), row),
                   pl.BlockSpec((tt * PACK_ROWS, LANES), row)],
        out_shape=[jax.ShapeDtypeStruct((n, d), f32), jax.ShapeDtypeStruct((n, d), bf16),
                   jax.ShapeDtypeStruct((n * PACK_ROWS, LANES), jnp.uint32)],
        scratch_shapes=[pltpu.VMEM((tt + pad, d), f32)],
        compiler_params=_params("parallel"),
        name="pool_layer",
    )(h, h, w_grp, scale, g, beta)


R_E1, R_E2, R_G1, R_G2, R_RANK1, R_RANK2 = range(6)


def _router_kernel(h_ref, whi_ref, wlo_ref, b_ref, info_ref, info_t_ref, cnt_ref):
    tm = h_ref.shape[0]

    @pl.when(pl.program_id(0) == 0)
    def _():
        cnt_ref[...] = jnp.zeros_like(cnt_ref)

    h = h_ref[...]
    h_hi = h.astype(bf16)
    h_lo = (h - h_hi.astype(f32)).astype(bf16)
    logits = (jnp.dot(h_hi, whi_ref[...], preferred_element_type=f32)
              + jnp.dot(h_lo, whi_ref[...], preferred_element_type=f32)
              + jnp.dot(h_hi, wlo_ref[...], preferred_element_type=f32)) + b_ref[...]
    lane = lax.broadcasted_iota(jnp.int32, logits.shape, 1)
    real = lane < N_EXPERTS
    e = jnp.exp(logits - jnp.max(logits, axis=-1, keepdims=True))
    probs = e / jnp.sum(e, axis=-1, keepdims=True)

    a = probs
    b = pltpu.roll(probs, 1, 1)
    c = pltpu.roll(probs, 2, 1)
    d = pltpu.roll(probs, 3, 1)
    hi1, lo1 = jnp.maximum(a, b), jnp.minimum(a, b)
    hi2, lo2 = jnp.maximum(c, d), jnp.minimum(c, d)
    score = jnp.maximum(hi1, hi2) + jnp.maximum(jnp.minimum(hi1, hi2), jnp.maximum(lo1, lo2))
    best = None
    g_sel = None
    for grp in range(N_EXPERTS // EXPERTS_PER_GROUP):
        last = grp * EXPERTS_PER_GROUP + EXPERTS_PER_GROUP - 1
        s = jnp.max(jnp.where(lane == last, score, -1.0), axis=-1, keepdims=True)
        if grp == 0:
            best, g_sel = s, jnp.zeros_like(s, dtype=jnp.int32)
        else:
            better = s > best
            best = jnp.where(better, s, best)
            g_sel = jnp.where(better, grp, g_sel)

    in_grp = real & ((lane // EXPERTS_PER_GROUP) == g_sel)
    masked = jnp.where(in_grp, probs, -1.0)
    lane_f = lane.astype(f32)
    p1 = jnp.max(masked, axis=-1, keepdims=True)
    e1 = jnp.min(jnp.where(masked == p1, lane_f, float(LANES)), axis=-1, keepdims=True)
    masked2 = jnp.where(lane_f == e1, -2.0, masked)
    p2 = jnp.max(masked2, axis=-1, keepdims=True)
    e2 = jnp.min(jnp.where(masked2 == p2, lane_f, float(LANES)), axis=-1, keepdims=True)
    tot = p1 + p2

    oh1 = (lane_f == e1).astype(f32)
    oh2 = (lane_f == e2).astype(f32)
    both = oh1 + oh2
    ri = lax.broadcasted_iota(jnp.int32, (tm, tm), 0)
    ci = lax.broadcasted_iota(jnp.int32, (tm, tm), 1)
    tri = (ci < ri).astype(bf16)
    before = cnt_ref[0:1, :] + jnp.dot(tri, both.astype(bf16), preferred_element_type=f32)
    rank1 = jnp.sum(before * oh1, axis=-1, keepdims=True)
    rank2 = jnp.sum(before * oh2, axis=-1, keepdims=True)
    cnt_ref[...] = cnt_ref[...] + jnp.sum(both, axis=0, keepdims=True)

    info = jnp.zeros(logits.shape, f32)
    for slot, val in ((R_E1, e1), (R_E2, e2), (R_G1, p1 / tot), (R_G2, p2 / tot),
                      (R_RANK1, rank1), (R_RANK2, rank2)):
        info = jnp.where(lane == slot, val, info)
    info_ref[...] = info
    info_t_ref[...] = info.T[:SUBLANES, :]


def _router(h, router_w, router_b, tm=512):
    n, d = h.shape
    w = jnp.zeros((d, LANES), f32).at[:, :N_EXPERTS].set(router_w)
    w_hi = w.astype(bf16)
    w_lo = (w - w_hi.astype(f32)).astype(bf16)
    b = jnp.full((1, LANES), -1e30, f32).at[0, :N_EXPERTS].set(router_b)
    return pl.pallas_call(
        _router_kernel,
        grid=(n // tm,),
        in_specs=[pl.BlockSpec((tm, d), lambda i: (i, 0)), pl.BlockSpec((d, LANES), lambda i: (0, 0)),
                  pl.BlockSpec((d, LANES), lambda i: (0, 0)), pl.BlockSpec((1, LANES), lambda i: (0, 0))],
        out_specs=[pl.BlockSpec((tm, LANES), lambda i: (i, 0)), pl.BlockSpec((SUBLANES, tm), lambda i: (0, i)),
                   pl.BlockSpec((SUBLANES, LANES), lambda i: (0, 0))],
        out_shape=[jax.ShapeDtypeStruct((n, LANES), f32), jax.ShapeDtypeStruct((SUBLANES, n), f32),
                   jax.ShapeDtypeStruct((SUBLANES, LANES), f32)],
        compiler_params=_params("arbitrary"),
        name="router",
    )(h, w_hi, w_lo, b)


def _token_copy(src_ref, dst_ref, sem, src_tok, dst_tok, rows):
    src = pl.multiple_of(src_tok * rows, SUBLANES)
    dst = pl.multiple_of(dst_tok * rows, SUBLANES)
    return pltpu.make_async_copy(src_ref.at[pl.ds(src, rows)], dst_ref.at[pl.ds(dst, rows)], sem)


def _expert_kernel(blk_e_ref, blk_valid_ref, blk_first_ref, blk_next_ref, blk_wmask_ref, row_tok_ref,
                   hp_ref, wg_hbm, wu_hbm, wd_hbm, o_ref,
                   xg_scr, x_scr, stage_g, stage_u, stage_d, wg_scr, wu_scr, wd_scr, sems, wsems):
    i = pl.program_id(0)
    tm = x_scr.shape[0]
    nblk = pl.num_programs(0)

    def weight_copies(e):
        return (pltpu.make_async_copy(wg_hbm.at[e], stage_g, wsems.at[0]),
                pltpu.make_async_copy(wu_hbm.at[e], stage_u, wsems.at[1]),
                pltpu.make_async_copy(wd_hbm.at[e], stage_d, wsems.at[2]))

    @pl.when(i == 0)
    def _():
        for cp in weight_copies(blk_e_ref[0]):
            cp.start()

    def token_copy(slot, j, tok):
        return _token_copy(hp_ref, xg_scr.at[slot], sems.at[slot], tok, j, PACK_ROWS)

    def wait_gather(slot):
        def body(j0, carry):
            for k in range(GATHER_UNROLL):
                token_copy(slot, 0, 0).wait()
            return carry
        lax.fori_loop(0, tm // GATHER_UNROLL, body, 0)

    @pl.when(i == 0)
    def _():
        def body(j0, carry):
            for k in range(GATHER_UNROLL):
                j = j0 * GATHER_UNROLL + k
                token_copy(0, j, row_tok_ref[j]).start()
            return carry
        lax.fori_loop(0, tm // GATHER_UNROLL, body, 0)

    @pl.when(blk_first_ref[i] != 0)
    def _():
        for cp in weight_copies(blk_e_ref[i]):
            cp.wait()
        chunk = 256
        for stage, dst in ((stage_g, wg_scr), (stage_u, wu_scr), (stage_d, wd_scr)):
            for r in range(0, stage.shape[0], chunk):
                dst[r:r + chunk, :] = stage[r:r + chunk, :].astype(bf16)

    this_slot = i % 2
    next_slot = (i + 1) % 2

    @pl.when(blk_valid_ref[i] != 0)
    def _():
        wait_gather(this_slot)
        _load_packed(xg_scr.at[this_slot], x_scr)
        nxt_blk = jnp.minimum(i + 1, nblk - 1)
        for j in range(tm):
            token_copy(next_slot, j, row_tok_ref[nxt_blk * tm + j]).start()
        x = x_scr[...]
        gate = _dot(x, wg_scr[...])
        hid = gate * jax.nn.sigmoid(gate) * _dot(x, wu_scr[...])
        y = _dot(hid, wd_scr[...])
        _store_packed(y, o_ref)

        mask = blk_wmask_ref[i]
        for k, cp in enumerate(weight_copies(jnp.maximum(blk_next_ref[i], 0))):
            @pl.when(((mask >> k) & 1) != 0)
            def _(cp=cp):
                cp.start()

        @pl.when(i == nblk - 1)
        def _():
            wait_gather(next_slot)

    @pl.when(blk_valid_ref[i] == 0)
    def _():
        o_ref[...] = jnp.zeros_like(o_ref)

        @pl.when(blk_valid_ref[jnp.maximum(i - 1, 0)] != 0)
        def _():
            wait_gather(this_slot)


def _experts(hp, row_tok, blk_e, blk_valid, blk_first, blk_next, blk_wmask, w_gate, w_up, w_down):
    cap = row_tok.shape[0]
    _, d, de = w_gate.shape
    row = lambda i, *_: (i, 0)
    any_spec = pl.BlockSpec(memory_space=pl.ANY)
    return pl.pallas_call(
        _expert_kernel,
        grid_spec=pltpu.PrefetchScalarGridSpec(
            num_scalar_prefetch=6, grid=(cap // MOE_ROWS,),
            in_specs=[any_spec, any_spec, any_spec, any_spec],
            out_specs=pl.BlockSpec((MOE_ROWS * PACK_ROWS, LANES), row),
            scratch_shapes=[pltpu.VMEM((2, MOE_ROWS * PACK_ROWS, LANES), jnp.uint32),
                            pltpu.VMEM((MOE_ROWS, d), bf16),
                            pltpu.VMEM((d, de), f32), pltpu.VMEM((d, de), f32), pltpu.VMEM((de, d), f32),
                            pltpu.VMEM((d, de), bf16), pltpu.VMEM((d, de), bf16), pltpu.VMEM((de, d), bf16),
                            pltpu.SemaphoreType.DMA((2,)), pltpu.SemaphoreType.DMA((3,))]),
        out_shape=jax.ShapeDtypeStruct((cap * PACK_ROWS, LANES), jnp.uint32),
        compiler_params=_params("arbitrary"),
        name="experts",
    )(blk_e, blk_valid, blk_first, blk_next, blk_wmask, row_tok, hp, w_gate, w_up, w_down)


def _load_packed_f32(hp_ref, tm):
    words = [hp_ref[pl.ds(c, tm, stride=PACK_ROWS), :] for c in range(PACK_ROWS)]
    lo = [lax.bitcast_convert_type(w << 16, f32) for w in words]
    hi = [lax.bitcast_convert_type(w & jnp.uint32(0xFFFF0000), f32) for w in words]
    return jnp.concatenate(lo + hi, axis=1)


def _moe_ln_kernel(dest_ref, h_ref, y_hbm, info_ref, g_ref, beta_ref, of_ref, ob_ref, y1_scr, y2_scr, sems):
    i = pl.program_id(0)
    ntile = pl.num_programs(0)
    tm = h_ref.shape[0]
    n_tok = ntile * tm

    def token_copy(buf, which, j, row):
        scr = (y1_scr, y2_scr)[which]
        return _token_copy(y_hbm, scr.at[buf], sems.at[buf], row, j, PACK_ROWS)

    def wait_gather(buf):
        def body(j0, carry):
            for _ in range(2 * GATHER_UNROLL):
                token_copy(buf, 0, 0, 0).wait()
            return carry
        lax.fori_loop(0, tm // GATHER_UNROLL, body, 0)

    def start_gather(tile, buf):
        def body(j0, carry):
            for which in range(2):
                for k in range(GATHER_UNROLL):
                    j = j0 * GATHER_UNROLL + k
                    token_copy(buf, which, j, dest_ref[which * n_tok + tile * tm + j]).start()
            return carry
        lax.fori_loop(0, tm // GATHER_UNROLL, body, 0)

    this_buf = i % 2

    @pl.when(i == 0)
    def _():
        start_gather(0, 0)

    @pl.when(i + 1 < ntile)
    def _():
        start_gather(i + 1, (i + 1) % 2)

    wait_gather(this_buf)
    info = info_ref[...]
    g1 = info[:, R_G1:R_G1 + 1]
    g2 = info[:, R_G2:R_G2 + 1]
    f = g1 * _load_packed_f32(y1_scr.at[this_buf], tm) + g2 * _load_packed_f32(y2_scr.at[this_buf], tm)
    out = _layer_norm(DN_ALPHA * h_ref[...] + f, g_ref[...], beta_ref[...])
    of_ref[...] = out
    ob_ref[...] = out.astype(bf16)


def _moe_ln(h, y, dest, info, g, beta, tm=256):
    n, d = h.shape
    row = lambda i, *_: (i, 0)
    fixed = lambda i, *_: (0, 0)
    return pl.pallas_call(
        _moe_ln_kernel,
        grid_spec=pltpu.PrefetchScalarGridSpec(
            num_scalar_prefetch=1, grid=(n // tm,),
            in_specs=[pl.BlockSpec((tm, d), row), pl.BlockSpec(memory_space=pl.ANY),
                      pl.BlockSpec((tm, LANES), row), pl.BlockSpec((1, d), fixed), pl.BlockSpec((1, d), fixed)],
            out_specs=[pl.BlockSpec((tm, d), row), pl.BlockSpec((tm, d), row)],
            scratch_shapes=[pltpu.VMEM((2, tm * PACK_ROWS, LANES), jnp.uint32),
                            pltpu.VMEM((2, tm * PACK_ROWS, LANES), jnp.uint32), pltpu.SemaphoreType.DMA((2,))]),
        out_shape=[jax.ShapeDtypeStruct((n, d), f32), jax.ShapeDtypeStruct((n, d), bf16)],
        compiler_params=_params("arbitrary"),
        name="moe_ln",
    )(dest, h, y, info, g, beta)


def _moe_layer(hf, hp, router_w, router_b, w_gate, w_up, w_down, expert_base, g, beta):
    n, d = hf.shape
    info, info_t, cnt = _router(hf, router_w, router_b)
    experts = info_t[R_E1:R_E2 + 1].astype(jnp.int32)
    rank = info_t[R_RANK1:R_RANK2 + 1].astype(jnp.int32)
    counts = cnt[0, :N_EXPERTS].astype(jnp.int32)
    pcounts = ((counts + MOE_ROWS - 1) // MOE_ROWS) * MOE_ROWS
    pends = jnp.cumsum(pcounts)
    pstarts = pends - pcounts
    dest = rank
    for e in range(N_EXPERTS):
        dest = dest + jnp.where(experts == e, pstarts[e], 0)
    cap = 2 * n + N_EXPERTS * MOE_ROWS
    dest = dest.reshape(-1)
    tok = jnp.tile(jnp.arange(n, dtype=jnp.int32), 2)
    row_tok = (jnp.arange(cap, dtype=jnp.int32) % n).at[dest].set(tok, unique_indices=True)
    blk_start = jnp.arange(cap // MOE_ROWS, dtype=jnp.int32) * MOE_ROWS
    blk_e = jnp.sum((blk_start[:, None] >= pends[None, :]).astype(jnp.int32), axis=1)
    blk_e = jnp.minimum(blk_e, N_EXPERTS - 1)
    blk_valid = (blk_start < pends[-1]).astype(jnp.int32)
    last_e = jnp.max(jnp.where(counts > 0, jnp.arange(N_EXPERTS, dtype=jnp.int32), 0))
    blk_e = jnp.where(blk_valid != 0, blk_e, last_e)
    blk_first = blk_valid * jnp.concatenate([jnp.ones((1,), jnp.int32), (blk_e[1:] != blk_e[:-1]).astype(jnp.int32)])
    ids = jnp.arange(N_EXPERTS, dtype=jnp.int32)
    later = (ids[None, :] > ids[:, None]) & (counts[None, :] > 0)
    next_e = jnp.min(jnp.where(later, ids[None, :], N_EXPERTS), axis=1)
    next_e = jnp.where(next_e < N_EXPERTS, next_e + expert_base, -1).astype(jnp.int32)
    mine = blk_e[:, None] == ids[None, :]
    blk_next = jnp.sum(jnp.where(mine, next_e[None, :], 0), axis=1)
    run_pos = blk_start // MOE_ROWS - jnp.sum(jnp.where(mine, pstarts[None, :] // MOE_ROWS, 0), axis=1)
    run_len = jnp.sum(jnp.where(mine, pcounts[None, :] // MOE_ROWS, 0), axis=1)
    mats = jnp.arange(3, dtype=jnp.int32)
    start_at = jnp.minimum(mats[None, :], run_len[:, None] - 1)
    starts = (run_pos[:, None] == start_at) & (blk_next[:, None] >= 0) & (blk_valid[:, None] != 0)
    blk_wmask = jnp.sum(starts.astype(jnp.int32) << mats[None, :], axis=1)

    y = _experts(hp, row_tok, blk_e + expert_base, blk_valid, blk_first, blk_next, blk_wmask,
                 w_gate, w_up, w_down)
    return _moe_ln(hf, y, dest, info, g, beta)


def kernel(x, s5_w_in, s5_b_re, s5_b_im, s5_c_re, s5_c_im, s5_a_re, s5_a_im, s5_log_step, s5_d, s5_w_glu, s5_w_out, cv_w_pw1, cv_b_pw1, cv_w_dw, cv_b_dw, cv_ln_g, cv_ln_b, cv_w_pw2, cv_b_pw2, pl_w, pl_scale, router_w, router_b, moe_w_gate, moe_w_up, moe_w_down, ln_mix_g, ln_mix_b, ln_ffn_g, ln_ffn_b):
    bsz, seq, d = x.shape
    hf = x.reshape(bsz * seq, d)
    hb = hf.astype(bf16)
    row = lambda v: v.reshape(1, -1)
    w_gate = moe_w_gate.reshape((-1,) + moe_w_gate.shape[2:])
    w_up = moe_w_up.reshape((-1,) + moe_w_up.shape[2:])
    w_down = moe_w_down.reshape((-1,) + moe_w_down.shape[2:])

    for i in range(DEPTH):
        mixer, j = i % N_MIXERS, i // N_MIXERS
        g_mix, b_mix = row(ln_mix_g[i]), row(ln_mix_b[i])
        if mixer == 0:
            u = _mm(hb, s5_w_in[j].astype(bf16))
            wagg, tmat, wout, dtab = _s5_derive(s5_b_re[j], s5_b_im[j], s5_c_re[j], s5_c_im[j],
                                                s5_a_re[j], s5_a_im[j], s5_log_step[j])
            z = _s5_core(u, wagg, tmat, wout, dtab, row(s5_d[j]))
            v = _mm_glu(z, s5_w_glu[j].astype(bf16), jnp.zeros((1, 2 * d), f32), bf16)
            hf, hb, hp = _mm_res_ln(v, s5_w_out[j].astype(bf16), jnp.zeros((1, d), f32), hf, g_mix, b_mix)
        elif mixer == 1:
            v = _mm_glu(hb, cv_w_pw1[j].astype(bf16), row(cv_b_pw1[j]), f32)
            cv = _conv_module(v, cv_w_dw[j], row(cv_b_dw[j]), row(cv_ln_g[j]), row(cv_ln_b[j]))
            hf, hb, hp = _mm_res_ln(cv, cv_w_pw2[j].astype(bf16), row(cv_b_pw2[j]), hf, g_mix, b_mix)
        else:
            hf, hb, hp = _pool_layer(hf, pl_w[j].astype(bf16), row(pl_scale[j]), g_mix, b_mix)
        hf, hb = _moe_layer(hf, hp, router_w, router_b, w_gate, w_up, w_down, i * N_EXPERTS,
                            row(ln_ffn_g[i]), row(ln_ffn_b[i]))
    return hf.reshape(bsz, seq, d)
```

```python
import math

import jax
import jax.numpy as jnp
from jax import lax
from jax.experimental import pallas as pl
from jax.experimental.pallas import tpu as pltpu

f32 = jnp.float32
bf16 = jnp.bfloat16

D_MODEL = 2048
DEPTH = 4
N_MIXERS = 3
S5_GROUP = 16
S5_STATE = 64
CONV_WIDTH = 31
POOL_WINDOWS = (2, 4, 8, 16)
POOL_CH = D_MODEL // len(POOL_WINDOWS)
N_EXPERTS = 16
EXPERTS_PER_GROUP = 4
D_EXPERT = D_MODEL // 2
DN_ALPHA = (2 * DEPTH) ** 0.25
LN_EPS = 1e-5

LANES = 128
SUBLANES = 8
VMEM_LIMIT = 56 * 1024 * 1024
S5_BLOCK = 8
S5_CHUNK_GROUPS = LANES // S5_GROUP
MOE_ROWS = 256
GATHER_UNROLL = 8
PACK_ROWS = D_MODEL // 2 // LANES


def _params(*sem):
    return pltpu.CompilerParams(dimension_semantics=sem, vmem_limit_bytes=VMEM_LIMIT)


def _layer_norm(r, g, b):
    mu = jnp.mean(r, axis=-1, keepdims=True)
    xc = r - mu
    var = jnp.mean(xc * xc, axis=-1, keepdims=True)
    return xc * lax.rsqrt(var + LN_EPS) * g + b


def _dot(a, b):
    return jnp.dot(a.astype(bf16), b.astype(bf16), preferred_element_type=f32)


def _store_packed(out, hp_ref):
    tm, d = out.shape
    for c in range(PACK_ROWS):
        lo = out[:, c * LANES:(c + 1) * LANES].astype(bf16).astype(f32)
        hi = out[:, d // 2 + c * LANES:d // 2 + (c + 1) * LANES].astype(bf16).astype(f32)
        word = (lax.bitcast_convert_type(lo, jnp.uint32) >> 16) | lax.bitcast_convert_type(hi, jnp.uint32)
        hp_ref[pl.ds(c, tm, stride=PACK_ROWS), :] = word


def _load_packed(hp_ref, x_scr):
    tm, d = x_scr.shape
    for c in range(PACK_ROWS):
        word = hp_ref[pl.ds(c, tm, stride=PACK_ROWS), :]
        lo = lax.bitcast_convert_type(word << 16, f32)
        hi = lax.bitcast_convert_type(word & jnp.uint32(0xFFFF0000), f32)
        x_scr[:, c * LANES:(c + 1) * LANES] = lo.astype(bf16)
        x_scr[:, d // 2 + c * LANES:d // 2 + (c + 1) * LANES] = hi.astype(bf16)


def _mm_kernel(x_ref, w_ref, o_ref):
    o_ref[...] = _dot(x_ref[...], w_ref[...]).astype(o_ref.dtype)


def _mm(x, w, tm=1024, tn=1024):
    m, k = x.shape
    n = w.shape[1]
    return pl.pallas_call(
        _mm_kernel,
        grid=(n // tn, m // tm),
        in_specs=[pl.BlockSpec((tm, k), lambda j, i: (i, 0)),
                  pl.BlockSpec((k, tn), lambda j, i: (0, j))],
        out_specs=pl.BlockSpec((tm, tn), lambda j, i: (i, j)),
        out_shape=jax.ShapeDtypeStruct((m, n), f32),
        compiler_params=_params("parallel", "parallel"),
        name="mm",
    )(x, w)


def _mm_glu_kernel(x_ref, wa_ref, wg_ref, ba_ref, bg_ref, o_ref):
    x = x_ref[...].astype(bf16)
    a = _dot(x, wa_ref[...]) + ba_ref[...]
    g = _dot(x, wg_ref[...]) + bg_ref[...]
    o_ref[...] = (a * jax.nn.sigmoid(g)).astype(o_ref.dtype)


def _mm_glu(x, w, b, out_dtype, tm=1024, tn=512):
    m, k = x.shape
    n = w.shape[1] // 2
    nb = n // tn
    return pl.pallas_call(
        _mm_glu_kernel,
        grid=(nb, m // tm),
        in_specs=[pl.BlockSpec((tm, k), lambda j, i: (i, 0)),
                  pl.BlockSpec((k, tn), lambda j, i: (0, j)),
                  pl.BlockSpec((k, tn), lambda j, i: (0, j + nb)),
                  pl.BlockSpec((1, tn), lambda j, i: (0, j)),
                  pl.BlockSpec((1, tn), lambda j, i: (0, j + nb))],
        out_specs=pl.BlockSpec((tm, tn), lambda j, i: (i, j)),
        out_shape=jax.ShapeDtypeStruct((m, n), out_dtype),
        compiler_params=_params("parallel", "parallel"),
        name="mm_glu",
    )(x, w, w, b, b)


def _mm_res_ln_kernel(x_ref, w_ref, b_ref, h_ref, g_ref, beta_ref, of_ref, ob_ref, hp_ref):
    y = _dot(x_ref[...], w_ref[...]) + b_ref[...]
    out = _layer_norm(DN_ALPHA * h_ref[...] + y, g_ref[...], beta_ref[...])
    of_ref[...] = out
    ob_ref[...] = out.astype(bf16)
    _store_packed(out, hp_ref)


def _mm_res_ln(x, w, b, h, g, beta, tm=512):
    m, k = x.shape
    d = w.shape[1]
    row = lambda i: (i, 0)
    fixed = lambda i: (0, 0)
    return pl.pallas_call(
        _mm_res_ln_kernel,
        grid=(m // tm,),
        in_specs=[pl.BlockSpec((tm, k), row), pl.BlockSpec((k, d), fixed, pipeline_mode=pl.Buffered(1)),
                  pl.BlockSpec((1, d), fixed),
                  pl.BlockSpec((tm, d), row), pl.BlockSpec((1, d), fixed), pl.BlockSpec((1, d), fixed)],
        out_specs=[pl.BlockSpec((tm, d), row), pl.BlockSpec((tm, d), row),
                   pl.BlockSpec((tm * PACK_ROWS, LANES), row)],
        out_shape=[jax.ShapeDtypeStruct((m, d), f32), jax.ShapeDtypeStruct((m, d), bf16),
                   jax.ShapeDtypeStruct((m * PACK_ROWS, LANES), jnp.uint32)],
        compiler_params=_params("parallel"),
        name="mm_res_ln",
    )(x, w, b, h, g, beta)


def _cmul_add(xr, xi, ar, ai, sr, si):
    return xr + ar * sr - ai * si, xi + ar * si + ai * sr


def _expand_block_diag(compact_ref, rep_ref, w_scr, row_shift, col_shift):
    width = w_scr.shape[0]
    step = 2 * LANES
    for c0 in range(0, width, step):
        w = jnp.dot(compact_ref[0], rep_ref[:, c0:c0 + step], preferred_element_type=f32)
        row_g = (lax.broadcasted_iota(jnp.int32, w.shape, 0) >> row_shift) & (S5_CHUNK_GROUPS - 1)
        col_g = ((lax.broadcasted_iota(jnp.int32, w.shape, 1) + c0) >> col_shift) & (S5_CHUNK_GROUPS - 1)
        w_scr[:, c0:c0 + step] = jnp.where(row_g == col_g, w, 0.0).astype(bf16)


def _s5_kernel(u_ref, agg_ref, toe_ref, proj_ref, rep_state_ref, rep_chan_ref, dtab_ref, d_ref, z_ref,
               xb_scr, v_scr, wagg_scr, t_scr, wout_scr):
    m_rows = v_scr.shape[0]
    half = v_scr.shape[1] // 2
    ncol = half // LANES
    chan_shift = S5_GROUP.bit_length() - 1
    state_shift = S5_STATE.bit_length() - 1
    _expand_block_diag(agg_ref, rep_state_ref, wagg_scr, chan_shift, state_shift)
    _expand_block_diag(toe_ref, rep_chan_ref, t_scr, chan_shift, chan_shift)
    _expand_block_diag(proj_ref, rep_state_ref, wout_scr, chan_shift, state_shift)
    for i in range(S5_BLOCK):
        xb_scr[:, i * LANES:(i + 1) * LANES] = u_ref[pl.ds(i, m_rows, stride=S5_BLOCK), :].astype(bf16)
    xb = xb_scr[...]
    v_scr[...] = jnp.dot(xb, wagg_scr[...], preferred_element_type=f32)

    tab = dtab_ref[0]
    sub = lax.broadcasted_iota(jnp.int32, (SUBLANES, LANES), 0)

    def col(j, part):
        lo = part * half + j * LANES
        return slice(lo, lo + LANES)

    def bcast(row):
        return jnp.broadcast_to(row, (SUBLANES, LANES))

    def body(r, carry):
        r0 = pl.multiple_of(r * SUBLANES, SUBLANES)
        new = []
        for j in range(ncol):
            cr, ci = carry[2 * j], carry[2 * j + 1]
            xr = v_scr[pl.ds(r0, SUBLANES), col(j, 0)]
            xi = v_scr[pl.ds(r0, SUBLANES), col(j, 1)]
            for shift in (1, 2, 4):
                ar = bcast(tab[shift - 1:shift, col(j, 0)])
                ai = bcast(tab[shift - 1:shift, col(j, 1)])
                sr = jnp.where(sub >= shift, pltpu.roll(xr, shift, 0), 0.0)
                si = jnp.where(sub >= shift, pltpu.roll(xi, shift, 0), 0.0)
                xr, xi = _cmul_add(xr, xi, ar, ai, sr, si)
            xr, xi = _cmul_add(xr, xi, tab[:, col(j, 0)], tab[:, col(j, 1)], cr, ci)
            v_scr[pl.ds(r0, SUBLANES), col(j, 0)] = jnp.where(sub >= 1, pltpu.roll(xr, 1, 0), cr)
            v_scr[pl.ds(r0, SUBLANES), col(j, 1)] = jnp.where(sub >= 1, pltpu.roll(xi, 1, 0), ci)
            new.append(bcast(xr[SUBLANES - 1:SUBLANES, :]))
            new.append(bcast(xi[SUBLANES - 1:SUBLANES, :]))
        return tuple(new)

    zero = jnp.zeros((SUBLANES, LANES), f32)
    lax.fori_loop(0, m_rows // SUBLANES, body, (zero,) * (2 * ncol))

    sp = v_scr[...].astype(bf16)
    for i0 in range(0, S5_BLOCK, 2):
        cs = slice(i0 * LANES, (i0 + 2) * LANES)
        nt = (((1,), (1,)), ((), ()))
        live = (i0 + 2) * LANES
        y = (lax.dot_general(xb[:, :live], t_scr[cs, :live], nt, preferred_element_type=f32)
             + lax.dot_general(sp, wout_scr[cs, :], nt, preferred_element_type=f32))
        for i in (i0, i0 + 1):
            yi = y[:, (i - i0) * LANES:(i - i0 + 1) * LANES] + d_ref[...] * u_ref[pl.ds(i, m_rows, stride=S5_BLOCK), :]
            z_ref[pl.ds(i, m_rows, stride=S5_BLOCK), :] = 0.5 * yi * (1.0 + lax.erf(yi * (1.0 / math.sqrt(2.0))))


def _s5_core(u, agg, toe, proj, dtab, d_skip):
    n, d = u.shape
    nq = d // LANES
    m_rows = n // S5_BLOCK
    width = agg.shape[1]
    col = jnp.arange(width)[None, :]
    lane = jnp.arange(LANES)[:, None]
    half = width // 2
    rep_state = ((col // half == lane // S5_STATE) & (col % S5_STATE == lane % S5_STATE)).astype(bf16)
    rep_chan = ((col // LANES == lane // S5_GROUP) & (col % S5_GROUP == lane % S5_GROUP)).astype(bf16)
    wspec = pl.BlockSpec((1, width, LANES), lambda q: (q, 0, 0))
    rspec = pl.BlockSpec((LANES, width), lambda q: (0, 0))
    return pl.pallas_call(
        _s5_kernel,
        grid=(nq,),
        in_specs=[pl.BlockSpec((n, LANES), lambda q: (0, q)), wspec, wspec, wspec, rspec, rspec,
                  pl.BlockSpec((1, SUBLANES, width), lambda q: (q, 0, 0)),
                  pl.BlockSpec((1, LANES), lambda q: (0, q))],
        out_specs=pl.BlockSpec((n, LANES), lambda q: (0, q)),
        out_shape=jax.ShapeDtypeStruct((n, d), f32),
        scratch_shapes=[pltpu.VMEM((m_rows, width), bf16), pltpu.VMEM((m_rows, width), f32),
                        pltpu.VMEM((width, width), bf16), pltpu.VMEM((width, width), bf16),
                        pltpu.VMEM((width, width), bf16)],
        compiler_params=_params("parallel"),
        name="s5_core",
    )(u, agg, toe, proj, rep_state, rep_chan, dtab, d_skip)


def _s5_derive(b_re, b_im, c_re, c_im, a_re, a_im, log_step):
    highest = lax.Precision.HIGHEST
    g, p, c = b_re.shape
    nq = g // S5_CHUNK_GROUPS
    dt = jnp.exp(log_step)[:, None]
    mag = jnp.exp(a_re * dt)
    lb_re = mag * jnp.cos(a_im * dt)
    lb_im = mag * jnp.sin(a_im * dt)
    den = a_re * a_re + a_im * a_im
    n_re = lb_re - 1.0
    n_im = lb_im
    f_re = (n_re * a_re + n_im * a_im) / den
    f_im = (n_im * a_re - n_re * a_im) / den
    bb_re = f_re[..., None] * b_re - f_im[..., None] * b_im
    bb_im = f_re[..., None] * b_im + f_im[..., None] * b_re

    def powers(br, bi, count):
        rs, is_ = [br], [bi]
        for _ in range(count - 1):
            rs.append(rs[-1] * br - is_[-1] * bi)
            is_.append(rs[-2] * bi + is_[-1] * br)
        return rs, is_

    pr, pi = powers(lb_re, lb_im, S5_BLOCK)
    lam_r = [jnp.ones_like(lb_re)] + pr
    lam_i = [jnp.zeros_like(lb_im)] + pi
    width = S5_BLOCK * LANES

    def compact(w):
        w = w.reshape(S5_BLOCK, nq, S5_CHUNK_GROUPS * c, LANES)
        return w.transpose(1, 0, 2, 3).reshape(nq, width, LANES).astype(bf16)

    bt_re = bb_re.transpose(0, 2, 1)
    bt_im = bb_im.transpose(0, 2, 1)

    ar = jnp.stack([lam_r[S5_BLOCK - 1 - i] for i in range(S5_BLOCK)])[:, :, None, :]
    ai = jnp.stack([lam_i[S5_BLOCK - 1 - i] for i in range(S5_BLOCK)])[:, :, None, :]
    wagg = compact(jnp.concatenate([ar * bt_re - ai * bt_im, ar * bt_im + ai * bt_re], axis=-1))

    ar = jnp.stack(lam_r[1:S5_BLOCK + 1])[:, :, None, :]
    ai = jnp.stack(lam_i[1:S5_BLOCK + 1])[:, :, None, :]
    wout = compact(jnp.concatenate([c_re * ar - c_im * ai, -(c_re * ai + c_im * ar)], axis=-1))

    ar = jnp.stack(lam_r[:S5_BLOCK])[:, :, None, :]
    ai = jnp.stack(lam_i[:S5_BLOCK])[:, :, None, :]
    kj = (jnp.einsum('jgcp,gpd->jgcd', c_re * ar - c_im * ai, bb_re, precision=highest)
          - jnp.einsum('jgcp,gpd->jgcd', c_re * ai + c_im * ar, bb_im, precision=highest))
    none = jnp.zeros_like(kj[0])
    tmat = compact(jnp.stack([jnp.concatenate([kj[i - a] if i >= a else none for a in range(S5_BLOCK)], axis=-1)
                              for i in range(S5_BLOCK)]))

    dr, di = powers(lam_r[S5_BLOCK], lam_i[S5_BLOCK], SUBLANES)
    dtab = jnp.stack([jnp.stack(dr), jnp.stack(di)], axis=1)
    dtab = dtab.reshape(SUBLANES, 2, nq, S5_CHUNK_GROUPS * p).transpose(2, 0, 1, 3).reshape(nq, SUBLANES, width)
    return wagg, tmat, wout, dtab


def _conv_kernel(v_ref, halo_ref, w_ref, b_ref, g_ref, beta_ref, o_ref, ext_scr, rot_scr, acc_scr):
    tt = v_ref.shape[0]
    pad = halo_ref.shape[0]
    first = pl.program_id(0) == 0
    ext_scr[0:pad, :] = jnp.where(first, 0.0, halo_ref[...])
    ext_scr[pad:pad + tt, :] = v_ref[...]
    span = rot_scr.shape[1]
    for sh in range(1, SUBLANES):
        rot_scr[sh - 1] = ext_scr[sh:sh + span, :]
    rows, cols = 64, 128
    for c0 in range(0, v_ref.shape[1], cols):
        for r0 in range(0, tt, rows):
            acc = jnp.broadcast_to(b_ref[:, c0:c0 + cols], (rows, cols))
            for k in range(CONV_WIDTH):
                whole, sh = divmod(pad - (CONV_WIDTH - 1) + k, SUBLANES)
                off = whole * SUBLANES + r0
                if sh == 0:
                    win = ext_scr[off:off + rows, c0:c0 + cols]
                else:
                    win = rot_scr[sh - 1, off:off + rows, c0:c0 + cols]
                acc = acc + w_ref[k:k + 1, c0:c0 + cols] * win
            acc_scr[r0:r0 + rows, c0:c0 + cols] = acc
    y = _layer_norm(acc_scr[...], g_ref[...], beta_ref[...])
    o_ref[...] = (y * jax.nn.sigmoid(y)).astype(o_ref.dtype)


def _conv_module(v, w_dw, b_dw, g, beta, tt=128, pad=32):
    n, d = v.shape
    w_pad = jnp.zeros((pad, d), f32).at[:CONV_WIDTH].set(w_dw)
    ratio = tt // pad
    row = lambda i: (i, 0)
    fixed = lambda i: (0, 0)
    return pl.pallas_call(
        _conv_kernel,
        grid=(n // tt,),
        in_specs=[pl.BlockSpec((tt, d), row),
                  pl.BlockSpec((pad, d), lambda i: (jnp.maximum(i * ratio - 1, 0), 0)),
                  pl.BlockSpec((pad, d), fixed), pl.BlockSpec((1, d), fixed),
                  pl.BlockSpec((1, d), fixed), pl.BlockSpec((1, d), fixed)],
        out_specs=pl.BlockSpec((tt, d), row),
        out_shape=jax.ShapeDtypeStruct((n, d), bf16),
        scratch_shapes=[pltpu.VMEM((tt + pad, d), f32),
                        pltpu.VMEM((SUBLANES - 1, tt + pad - SUBLANES, d), f32),
                        pltpu.VMEM((tt, d), f32)],
        compiler_params=_params("parallel"),
        name="conv_module",
    )(v, v, w_pad, b_dw, g, beta)


def _pool_kernel(h_ref, halo_ref, w_ref, scale_ref, g_ref, beta_ref, of_ref, ob_ref, hp_ref, ext_scr):
    tt = h_ref.shape[0]
    pad = halo_ref.shape[0]
    i = pl.program_id(0)
    ext_scr[0:pad, :] = jnp.where(i == 0, 0.0, halo_ref[...])
    ext_scr[pad:pad + tt, :] = h_ref[...]
    t = i * tt + lax.broadcasted_iota(jnp.int32, (tt, 1), 0)
    ys = []
    for k, win in enumerate(POOL_WINDOWS):
        cs = slice(k * POOL_CH, (k + 1) * POOL_CH)
        x = h_ref[:, cs]
        acc = x
        for j in range(1, win):
            acc = acc + ext_scr[pad - j:pad - j + tt, cs]
        cnt = jnp.minimum(t + 1, win).astype(f32)
        ys.append(_dot(acc / cnt - x, w_ref[k]))
    y = jnp.concatenate(ys, axis=1) * scale_ref[...]
    out = _layer_norm(DN_ALPHA * h_ref[...] + y, g_ref[...], beta_ref[...])
    of_ref[...] = out
    ob_ref[...] = out.astype(bf16)
    _store_packed(out, hp_ref)


def _pool_layer(h, w_grp, scale, g, beta, tt=256, pad=16):
    n, d = h.shape
    ratio = tt // pad
    row = lambda i: (i, 0)
    fixed = lambda i: (0, 0)
    return pl.pallas_call(
        _pool_kernel,
        grid=(n // tt,),
        in_specs=[pl.BlockSpec((tt, d), row),
                  pl.BlockSpec((pad, d), lambda i: (jnp.maximum(i * ratio - 1, 0), 0)),
                  pl.BlockSpec(w_grp.shape, lambda i: (0, 0, 0)),
                  pl.BlockSpec((1, d), fixed), pl.BlockSpec((1, d), fixed), pl.BlockSpec((1, d), fixed)],
        out_specs=[pl.BlockSpec((tt, d), row), pl.BlockSpec((tt, d), row),
                   pl.BlockSpec((tt * PACK_ROWS, LANES), row)],
        out_shape=[jax.ShapeDtypeStruct((n, d), f32), jax.ShapeDtypeStruct((n, d), bf16),
                   jax.ShapeDtypeStruct((n * PACK_ROWS, LANES), jnp.uint32)],
        scratch_shapes=[pltpu.VMEM((tt + pad, d), f32)],
        compiler_params=_params("parallel"),
        name="pool_layer",
    )(h, h, w_grp, scale, g, beta)


R_E1, R_E2, R_G1, R_G2, R_RANK1, R_RANK2 = range(6)


def _router_kernel(h_ref, whi_ref, wlo_ref, b_ref, info_ref, info_t_ref, cnt_ref):
    tm = h_ref.shape[0]

    @pl.when(pl.program_id(0) == 0)
    def _():
        cnt_ref[...] = jnp.zeros_like(cnt_ref)

    h = h_ref[...]
    h_hi = h.astype(bf16)
    h_lo = (h - h_hi.astype(f32)).astype(bf16)
    logits = (jnp.dot(h_hi, whi_ref[...], preferred_element_type=f32)
              + jnp.dot(h_lo, whi_ref[...], preferred_element_type=f32)
              + jnp.dot(h_hi, wlo_ref[...], preferred_element_type=f32)) + b_ref[...]
    lane = lax.broadcasted_iota(jnp.int32, logits.shape, 1)
    real = lane < N_EXPERTS
    e = jnp.exp(logits - jnp.max(logits, axis=-1, keepdims=True))
    probs = e / jnp.sum(e, axis=-1, keepdims=True)

    a = probs
    b = pltpu.roll(probs, 1, 1)
    c = pltpu.roll(probs, 2, 1)
    d = pltpu.roll(probs, 3, 1)
    hi1, lo1 = jnp.maximum(a, b), jnp.minimum(a, b)
    hi2, lo2 = jnp.maximum(c, d), jnp.minimum(c, d)
    score = jnp.maximum(hi1, hi2) + jnp.maximum(jnp.minimum(hi1, hi2), jnp.maximum(lo1, lo2))
    best = None
    g_sel = None
    for grp in range(N_EXPERTS // EXPERTS_PER_GROUP):
        last = grp * EXPERTS_PER_GROUP + EXPERTS_PER_GROUP - 1
        s = jnp.max(jnp.where(lane == last, score, -1.0), axis=-1, keepdims=True)
        if grp == 0:
            best, g_sel = s, jnp.zeros_like(s, dtype=jnp.int32)
        else:
            better = s > best
            best = jnp.where(better, s, best)
            g_sel = jnp.where(better, grp, g_sel)

    in_grp = real & ((lane // EXPERTS_PER_GROUP) == g_sel)
    masked = jnp.where(in_grp, probs, -1.0)
    lane_f = lane.astype(f32)
    p1 = jnp.max(masked, axis=-1, keepdims=True)
    e1 = jnp.min(jnp.where(masked == p1, lane_f, float(LANES)), axis=-1, keepdims=True)
    masked2 = jnp.where(lane_f == e1, -2.0, masked)
    p2 = jnp.max(masked2, axis=-1, keepdims=True)
    e2 = jnp.min(jnp.where(masked2 == p2, lane_f, float(LANES)), axis=-1, keepdims=True)
    tot = p1 + p2

    oh1 = (lane_f == e1).astype(f32)
    oh2 = (lane_f == e2).astype(f32)
    both = oh1 + oh2
    ri = lax.broadcasted_iota(jnp.int32, (tm, tm), 0)
    ci = lax.broadcasted_iota(jnp.int32, (tm, tm), 1)
    tri = (ci < ri).astype(bf16)
    before = cnt_ref[0:1, :] + jnp.dot(tri, both.astype(bf16), preferred_element_type=f32)
    rank1 = jnp.sum(before * oh1, axis=-1, keepdims=True)
    rank2 = jnp.sum(before * oh2, axis=-1, keepdims=True)
    cnt_ref[...] = cnt_ref[...] + jnp.sum(both, axis=0, keepdims=True)

    info = jnp.zeros(logits.shape, f32)
    for slot, val in ((R_E1, e1), (R_E2, e2), (R_G1, p1 / tot), (R_G2, p2 / tot),
                      (R_RANK1, rank1), (R_RANK2, rank2)):
        info = jnp.where(lane == slot, val, info)
    info_ref[...] = info
    info_t_ref[...] = info.T[:SUBLANES, :]


def _router(h, router_w, router_b, tm=512):
    n, d = h.shape
    w = jnp.zeros((d, LANES), f32).at[:, :N_EXPERTS].set(router_w)
    w_hi = w.astype(bf16)
    w_lo = (w - w_hi.astype(f32)).astype(bf16)
    b = jnp.full((1, LANES), -1e30, f32).at[0, :N_EXPERTS].set(router_b)
    return pl.pallas_call(
        _router_kernel,
        grid=(n // tm,),
        in_specs=[pl.BlockSpec((tm, d), lambda i: (i, 0)), pl.BlockSpec((d, LANES), lambda i: (0, 0)),
                  pl.BlockSpec((d, LANES), lambda i: (0, 0)), pl.BlockSpec((1, LANES), lambda i: (0, 0))],
        out_specs=[pl.BlockSpec((tm, LANES), lambda i: (i, 0)), pl.BlockSpec((SUBLANES, tm), lambda i: (0, i)),
                   pl.BlockSpec((SUBLANES, LANES), lambda i: (0, 0))],
        out_shape=[jax.ShapeDtypeStruct((n, LANES), f32), jax.ShapeDtypeStruct((SUBLANES, n), f32),
                   jax.ShapeDtypeStruct((SUBLANES, LANES), f32)],
        compiler_params=_params("arbitrary"),
        name="router",
    )(h, w_hi, w_lo, b)


def _token_copy(src_ref, dst_ref, sem, src_tok, dst_tok, rows):
    src = pl.multiple_of(src_tok * rows, SUBLANES)
    dst = pl.multiple_of(dst_tok * rows, SUBLANES)
    return pltpu.make_async_copy(src_ref.at[pl.ds(src, rows)], dst_ref.at[pl.ds(dst, rows)], sem)


def _expert_kernel(blk_e_ref, blk_valid_ref, blk_first_ref, blk_next_ref, blk_wmask_ref, row_tok_ref,
                   hp_ref, wg_hbm, wu_hbm, wd_hbm, o_ref,
                   xg_scr, x_scr, stage_g, stage_u, stage_d, wg_scr, wu_scr, wd_scr, sems, wsems):
    i = pl.program_id(0)
    tm = x_scr.shape[0]
    nblk = pl.num_programs(0)

    def weight_copies(e):
        return (pltpu.make_async_copy(wg_hbm.at[e], stage_g, wsems.at[0]),
                pltpu.make_async_copy(wu_hbm.at[e], stage_u, wsems.at[1]),
                pltpu.make_async_copy(wd_hbm.at[e], stage_d, wsems.at[2]))

    @pl.when(i == 0)
    def _():
        for cp in weight_copies(blk_e_ref[0]):
            cp.start()

    def token_copy(slot, j, tok):
        return _token_copy(hp_ref, xg_scr.at[slot], sems.at[slot], tok, j, PACK_ROWS)

    def wait_gather(slot):
        def body(j0, carry):
            for k in range(GATHER_UNROLL):
                token_copy(slot, 0, 0).wait()
            return carry
        lax.fori_loop(0, tm // GATHER_UNROLL, body, 0)

    @pl.when(i == 0)
    def _():
        def body(j0, carry):
            for k in range(GATHER_UNROLL):
                j = j0 * GATHER_UNROLL + k
                token_copy(0, j, row_tok_ref[j]).start()
            return carry
        lax.fori_loop(0, tm // GATHER_UNROLL, body, 0)

    @pl.when(blk_first_ref[i] != 0)
    def _():
        for cp in weight_copies(blk_e_ref[i]):
            cp.wait()
        chunk = 256
        for stage, dst in ((stage_g, wg_scr), (stage_u, wu_scr), (stage_d, wd_scr)):
            for r in range(0, stage.shape[0], chunk):
                dst[r:r + chunk, :] = stage[r:r + chunk, :].astype(bf16)

    this_slot = i % 2
    next_slot = (i + 1) % 2

    @pl.when(blk_valid_ref[i] != 0)
    def _():
        wait_gather(this_slot)
        _load_packed(xg_scr.at[this_slot], x_scr)
        nxt_blk = jnp.minimum(i + 1, nblk - 1)
        for j in range(tm):
            token_copy(next_slot, j, row_tok_ref[nxt_blk * tm + j]).start()
        x = x_scr[...]
        gate = _dot(x, wg_scr[...])
        hid = gate * jax.nn.sigmoid(gate) * _dot(x, wu_scr[...])
        y = _dot(hid, wd_scr[...])
        _store_packed(y, o_ref)

        mask = blk_wmask_ref[i]
        for k, cp in enumerate(weight_copies(jnp.maximum(blk_next_ref[i], 0))):
            @pl.when(((mask >> k) & 1) != 0)
            def _(cp=cp):
                cp.start()

        @pl.when(i == nblk - 1)
        def _():
            wait_gather(next_slot)

    @pl.when(blk_valid_ref[i] == 0)
    def _():
        o_ref[...] = jnp.zeros_like(o_ref)

        @pl.when(blk_valid_ref[jnp.maximum(i - 1, 0)] != 0)
        def _():
            wait_gather(this_slot)


def _experts(hp, row_tok, blk_e, blk_valid, blk_first, blk_next, blk_wmask, w_gate, w_up, w_down):
    cap = row_tok.shape[0]
    _, d, de = w_gate.shape
    row = lambda i, *_: (i, 0)
    any_spec = pl.BlockSpec(memory_space=pl.ANY)
    return pl.pallas_call(
        _expert_kernel,
        grid_spec=pltpu.PrefetchScalarGridSpec(
            num_scalar_prefetch=6, grid=(cap // MOE_ROWS,),
            in_specs=[any_spec, any_spec, any_spec, any_spec],
            out_specs=pl.BlockSpec((MOE_ROWS * PACK_ROWS, LANES), row),
            scratch_shapes=[pltpu.VMEM((2, MOE_ROWS * PACK_ROWS, LANES), jnp.uint32),
                            pltpu.VMEM((MOE_ROWS, d), bf16),
                            pltpu.VMEM((d, de), f32), pltpu.VMEM((d, de), f32), pltpu.VMEM((de, d), f32),
                            pltpu.VMEM((d, de), bf16), pltpu.VMEM((d, de), bf16), pltpu.VMEM((de, d), bf16),
                            pltpu.SemaphoreType.DMA((2,)), pltpu.SemaphoreType.DMA((3,))]),
        out_shape=jax.ShapeDtypeStruct((cap * PACK_ROWS, LANES), jnp.uint32),
        compiler_params=_params("arbitrary"),
        name="experts",
    )(blk_e, blk_valid, blk_first, blk_next, blk_wmask, row_tok, hp, w_gate, w_up, w_down)


def _load_packed_f32(hp_ref, tm):
    words = [hp_ref[pl.ds(c, tm, stride=PACK_ROWS), :] for c in range(PACK_ROWS)]
    lo = [lax.bitcast_convert_type(w << 16, f32) for w in words]
    hi = [lax.bitcast_convert_type(w & jnp.uint32(0xFFFF0000), f32) for w in words]
    return jnp.concatenate(lo + hi, axis=1)


def _moe_ln_kernel(dest_ref, h_ref, y_hbm, info_ref, g_ref, beta_ref, of_ref, ob_ref, y1_scr, y2_scr, sems):
    i = pl.program_id(0)
    ntile = pl.num_programs(0)
    tm = h_ref.shape[0]
    n_tok = ntile * tm

    def token_copy(buf, which, j, row):
        scr = (y1_scr, y2_scr)[which]
        return _token_copy(y_hbm, scr.at[buf], sems.at[buf], row, j, PACK_ROWS)

    def wait_gather(buf):
        def body(j0, carry):
            for _ in range(2 * GATHER_UNROLL):
                token_copy(buf, 0, 0, 0).wait()
            return carry
        lax.fori_loop(0, tm // GATHER_UNROLL, body, 0)

    def start_gather(tile, buf):
        def body(j0, carry):
            for which in range(2):
                for k in range(GATHER_UNROLL):
                    j = j0 * GATHER_UNROLL + k
                    token_copy(buf, which, j, dest_ref[which * n_tok + tile * tm + j]).start()
            return carry
        lax.fori_loop(0, tm // GATHER_UNROLL, body, 0)

    this_buf = i % 2

    @pl.when(i == 0)
    def _():
        start_gather(0, 0)

    @pl.when(i + 1 < ntile)
    def _():
        start_gather(i + 1, (i + 1) % 2)

    wait_gather(this_buf)
    info = info_ref[...]
    g1 = info[:, R_G1:R_G1 + 1]
    g2 = info[:, R_G2:R_G2 + 1]
    f = g1 * _load_packed_f32(y1_scr.at[this_buf], tm) + g2 * _load_packed_f32(y2_scr.at[this_buf], tm)
    out = _layer_norm(DN_ALPHA * h_ref[...] + f, g_ref[...], beta_ref[...])
    of_ref[...] = out
    ob_ref[...] = out.astype(bf16)


def _moe_ln(h, y, dest, info, g, beta, tm=256):
    n, d = h.shape
    row = lambda i, *_: (i, 0)
    fixed = lambda i, *_: (0, 0)
    return pl.pallas_call(
        _moe_ln_kernel,
        grid_spec=pltpu.PrefetchScalarGridSpec(
            num_scalar_prefetch=1, grid=(n // tm,),
            in_specs=[pl.BlockSpec((tm, d), row), pl.BlockSpec(memory_space=pl.ANY),
                      pl.BlockSpec((tm, LANES), row), pl.BlockSpec((1, d), fixed), pl.BlockSpec((1, d), fixed)],
            out_specs=[pl.BlockSpec((tm, d), row), pl.BlockSpec((tm, d), row)],
            scratch_shapes=[pltpu.VMEM((2, tm * PACK_ROWS, LANES), jnp.uint32),
                            pltpu.VMEM((2, tm * PACK_ROWS, LANES), jnp.uint32), pltpu.SemaphoreType.DMA((2,))]),
        out_shape=[jax.ShapeDtypeStruct((n, d), f32), jax.ShapeDtypeStruct((n, d), bf16)],
        compiler_params=_params("arbitrary"),
        name="moe_ln",
    )(dest, h, y, info, g, beta)


def _moe_layer(hf, hp, router_w, router_b, w_gate, w_up, w_down, expert_base, g, beta):
    n, d = hf.shape
    info, info_t, cnt = _router(hf, router_w, router_b)
    experts = info_t[R_E1:R_E2 + 1].astype(jnp.int32)
    rank = info_t[R_RANK1:R_RANK2 + 1].astype(jnp.int32)
    counts = cnt[0, :N_EXPERTS].astype(jnp.int32)
    pcounts = ((counts + MOE_ROWS - 1) // MOE_ROWS) * MOE_ROWS
    pends = jnp.cumsum(pcounts)
    pstarts = pends - pcounts
    dest = rank
    for e in range(N_EXPERTS):
        dest = dest + jnp.where(experts == e, pstarts[e], 0)
    cap = 2 * n + N_EXPERTS * MOE_ROWS
    dest = dest.reshape(-1)
    tok = jnp.tile(jnp.arange(n, dtype=jnp.int32), 2)
    row_tok = (jnp.arange(cap, dtype=jnp.int32) % n).at[dest].set(tok, unique_indices=True)
    blk_start = jnp.arange(cap // MOE_ROWS, dtype=jnp.int32) * MOE_ROWS
    blk_e = jnp.sum((blk_start[:, None] >= pends[None, :]).astype(jnp.int32), axis=1)
    blk_e = jnp.minimum(blk_e, N_EXPERTS - 1)
    blk_valid = (blk_start < pends[-1]).astype(jnp.int32)
    last_e = jnp.max(jnp.where(counts > 0, jnp.arange(N_EXPERTS, dtype=jnp.int32), 0))
    blk_e = jnp.where(blk_valid != 0, blk_e, last_e)
    blk_first = blk_valid * jnp.concatenate([jnp.ones((1,), jnp.int32), (blk_e[1:] != blk_e[:-1]).astype(jnp.int32)])
    ids = jnp.arange(N_EXPERTS, dtype=jnp.int32)
    later = (ids[None, :] > ids[:, None]) & (counts[None, :] > 0)
    next_e = jnp.min(jnp.where(later, ids[None, :], N_EXPERTS), axis=1)
    next_e = jnp.where(next_e < N_EXPERTS, next_e + expert_base, -1).astype(jnp.int32)
    mine = blk_e[:, None] == ids[None, :]
    blk_next = jnp.sum(jnp.where(mine, next_e[None, :], 0), axis=1)
    run_pos = blk_start // MOE_ROWS - jnp.sum(jnp.where(mine, pstarts[None, :] // MOE_ROWS, 0), axis=1)
    run_len = jnp.sum(jnp.where(mine, pcounts[None, :] // MOE_ROWS, 0), axis=1)
    mats = jnp.arange(3, dtype=jnp.int32)
    start_at = jnp.minimum(mats[None, :], run_len[:, None] - 1)
    starts = (run_pos[:, None] == start_at) & (blk_next[:, None] >= 0) & (blk_valid[:, None] != 0)
    blk_wmask = jnp.sum(starts.astype(jnp.int32) << mats[None, :], axis=1)

    y = _experts(hp, row_tok, blk_e + expert_base, blk_valid, blk_first, blk_next, blk_wmask,
                 w_gate, w_up, w_down)
    return _moe_ln(hf, y, dest, info, g, beta)


def kernel(x, s5_w_in, s5_b_re, s5_b_im, s5_c_re, s5_c_im, s5_a_re, s5_a_im, s5_log_step, s5_d, s5_w_glu, s5_w_out, cv_w_pw1, cv_b_pw1, cv_w_dw, cv_b_dw, cv_ln_g, cv_ln_b, cv_w_pw2, cv_b_pw2, pl_w, pl_scale, router_w, router_b, moe_w_gate, moe_w_up, moe_w_down, ln_mix_g, ln_mix_b, ln_ffn_g, ln_ffn_b):
    bsz, seq, d = x.shape
    hf = x.reshape(bsz * seq, d)
    hb = hf.astype(bf16)
    row = lambda v: v.reshape(1, -1)
    w_gate = moe_w_gate.reshape((-1,) + moe_w_gate.shape[2:])
    w_up = moe_w_up.reshape((-1,) + moe_w_up.shape[2:])
    w_down = moe_w_down.reshape((-1,) + moe_w_down.shape[2:])

    for i in range(DEPTH):
        mixer, j = i % N_MIXERS, i // N_MIXERS
        g_mix, b_mix = row(ln_mix_g[i]), row(ln_mix_b[i])
        if mixer == 0:
            u = _mm(hb, s5_w_in[j].astype(bf16))
            wagg, tmat, wout, dtab = _s5_derive(s5_b_re[j], s5_b_im[j], s5_c_re[j], s5_c_im[j],
                                                s5_a_re[j], s5_a_im[j], s5_log_step[j])
            z = _s5_core(u, wagg, tmat, wout, dtab, row(s5_d[j]))
            v = _mm_glu(z, s5_w_glu[j].astype(bf16), jnp.zeros((1, 2 * d), f32), bf16)
            hf, hb, hp = _mm_res_ln(v, s5_w_out[j].astype(bf16), jnp.zeros((1, d), f32), hf, g_mix, b_mix)
        elif mixer == 1:
            v = _mm_glu(hb, cv_w_pw1[j].astype(bf16), row(cv_b_pw1[j]), f32)
            cv = _conv_module(v, cv_w_dw[j], row(cv_b_dw[j]), row(cv_ln_g[j]), row(cv_ln_b[j]))
            hf, hb, hp = _mm_res_ln(cv, cv_w_pw2[j].astype(bf16), row(cv_b_pw2[j]), hf, g_mix, b_mix)
        else:
            hf, hb, hp = _pool_layer(hf, pl_w[j].astype(bf16), row(pl_scale[j]), g_mix, b_mix)
        hf, hb = _moe_layer(hf, hp, router_w, router_b, w_gate, w_up, w_down, i * N_EXPERTS,
                            row(ln_ffn_g[i]), row(ln_ffn_b[i]))
    return hf.reshape(bsz, seq, d)
```

```python
import math

import jax
import jax.numpy as jnp
from jax import lax
from jax.experimental import pallas as pl
from jax.experimental.pallas import tpu as pltpu

f32 = jnp.float32
bf16 = jnp.bfloat16

D_MODEL = 2048
DEPTH = 4
N_MIXERS = 3
S5_GROUP = 16
S5_STATE = 64
CONV_WIDTH = 31
POOL_WINDOWS = (2, 4, 8, 16)
POOL_CH = D_MODEL // len(POOL_WINDOWS)
N_EXPERTS = 16
EXPERTS_PER_GROUP = 4
D_EXPERT = D_MODEL // 2
DN_ALPHA = (2 * DEPTH) ** 0.25
LN_EPS = 1e-5

LANES = 128
SUBLANES = 8
VMEM_LIMIT = 56 * 1024 * 1024
S5_BLOCK = 8
S5_CHUNK_GROUPS = LANES // S5_GROUP
MOE_ROWS = 256
GATHER_UNROLL = 8
PACK_ROWS = D_MODEL // 2 // LANES


def _params(*sem):
    return pltpu.CompilerParams(dimension_semantics=sem, vmem_limit_bytes=VMEM_LIMIT)


def _layer_norm(r, g, b):
    mu = jnp.mean(r, axis=-1, keepdims=True)
    xc = r - mu
    var = jnp.mean(xc * xc, axis=-1, keepdims=True)
    return xc * lax.rsqrt(var + LN_EPS) * g + b


def _dot(a, b):
    return jnp.dot(a.astype(bf16), b.astype(bf16), preferred_element_type=f32)


def _store_packed(out, hp_ref):
    tm, d = out.shape
    for c in range(PACK_ROWS):
        lo = out[:, c * LANES:(c + 1) * LANES].astype(bf16).astype(f32)
        hi = out[:, d // 2 + c * LANES:d // 2 + (c + 1) * LANES].astype(bf16).astype(f32)
        word = (lax.bitcast_convert_type(lo, jnp.uint32) >> 16) | lax.bitcast_convert_type(hi, jnp.uint32)
        hp_ref[pl.ds(c, tm, stride=PACK_ROWS), :] = word


def _load_packed(hp_ref, x_scr):
    tm, d = x_scr.shape
    for c in range(PACK_ROWS):
        word = hp_ref[pl.ds(c, tm, stride=PACK_ROWS), :]
        lo = lax.bitcast_convert_type(word << 16, f32)
        hi = lax.bitcast_convert_type(word & jnp.uint32(0xFFFF0000), f32)
        x_scr[:, c * LANES:(c + 1) * LANES] = lo.astype(bf16)
        x_scr[:, d // 2 + c * LANES:d // 2 + (c + 1) * LANES] = hi.astype(bf16)


def _mm_kernel(x_ref, w_ref, o_ref):
    o_ref[...] = _dot(x_ref[...], w_ref[...]).astype(o_ref.dtype)


def _mm(x, w, tm=1024, tn=1024):
    m, k = x.shape
    n = w.shape[1]
    return pl.pallas_call(
        _mm_kernel,
        grid=(n // tn, m // tm),
        in_specs=[pl.BlockSpec((tm, k), lambda j, i: (i, 0)),
                  pl.BlockSpec((k, tn), lambda j, i: (0, j))],
        out_specs=pl.BlockSpec((tm, tn), lambda j, i: (i, j)),
        out_shape=jax.ShapeDtypeStruct((m, n), f32),
        compiler_params=_params("parallel", "parallel"),
        name="mm",
    )(x, w)


def _mm_glu_kernel(x_ref, wa_ref, wg_ref, ba_ref, bg_ref, o_ref):
    x = x_ref[...].astype(bf16)
    a = _dot(x, wa_ref[...]) + ba_ref[...]
    g = _dot(x, wg_ref[...]) + bg_ref[...]
    o_ref[...] = (a * jax.nn.sigmoid(g)).astype(o_ref.dtype)


def _mm_glu(x, w, b, out_dtype, tm=1024, tn=512):
    m, k = x.shape
    n = w.shape[1] // 2
    nb = n // tn
    return pl.pallas_call(
        _mm_glu_kernel,
        grid=(nb, m // tm),
        in_specs=[pl.BlockSpec((tm, k), lambda j, i: (i, 0)),
                  pl.BlockSpec((k, tn), lambda j, i: (0, j)),
                  pl.BlockSpec((k, tn), lambda j, i: (0, j + nb)),
                  pl.BlockSpec((1, tn), lambda j, i: (0, j)),
                  pl.BlockSpec((1, tn), lambda j, i: (0, j + nb))],
        out_specs=pl.BlockSpec((tm, tn), lambda j, i: (i, j)),
        out_shape=jax.ShapeDtypeStruct((m, n), out_dtype),
        compiler_params=_params("parallel", "parallel"),
        name="mm_glu",
    )(x, w, w, b, b)


def _mm_res_ln_kernel(x_ref, w_ref, b_ref, h_ref, g_ref, beta_ref, of_ref, ob_ref, hp_ref):
    y = _dot(x_ref[...], w_ref[...]) + b_ref[...]
    out = _layer_norm(DN_ALPHA * h_ref[...] + y, g_ref[...], beta_ref[...])
    of_ref[...] = out
    ob_ref[...] = out.astype(bf16)
    _store_packed(out, hp_ref)


def _mm_res_ln(x, w, b, h, g, beta, tm=512):
    m, k = x.shape
    d = w.shape[1]
    row = lambda i: (i, 0)
    fixed = lambda i: (0, 0)
    return pl.pallas_call(
        _mm_res_ln_kernel,
        grid=(m // tm,),
        in_specs=[pl.BlockSpec((tm, k), row), pl.BlockSpec((k, d), fixed, pipeline_mode=pl.Buffered(1)),
                  pl.BlockSpec((1, d), fixed),
                  pl.BlockSpec((tm, d), row), pl.BlockSpec((1, d), fixed), pl.BlockSpec((1, d), fixed)],
        out_specs=[pl.BlockSpec((tm, d), row), pl.BlockSpec((tm, d), row),
                   pl.BlockSpec((tm * PACK_ROWS, LANES), row)],
        out_shape=[jax.ShapeDtypeStruct((m, d), f32), jax.ShapeDtypeStruct((m, d), bf16),
                   jax.ShapeDtypeStruct((m * PACK_ROWS, LANES), jnp.uint32)],
        compiler_params=_params("parallel"),
        name="mm_res_ln",
    )(x, w, b, h, g, beta)


def _cmul_add(xr, xi, ar, ai, sr, si):
    return xr + ar * sr - ai * si, xi + ar * si + ai * sr


def _expand_block_diag(compact_ref, rep_ref, w_scr, row_shift, col_shift):
    width = w_scr.shape[0]
    step = 2 * LANES
    for c0 in range(0, width, step):
        w = jnp.dot(compact_ref[0], rep_ref[:, c0:c0 + step], preferred_element_type=f32)
        row_g = (lax.broadcasted_iota(jnp.int32, w.shape, 0) >> row_shift) & (S5_CHUNK_GROUPS - 1)
        col_g = ((lax.broadcasted_iota(jnp.int32, w.shape, 1) + c0) >> col_shift) & (S5_CHUNK_GROUPS - 1)
        w_scr[:, c0:c0 + step] = jnp.where(row_g == col_g, w, 0.0).astype(bf16)


def _s5_kernel(u_ref, agg_ref, toe_ref, proj_ref, rep_state_ref, rep_chan_ref, dtab_ref, d_ref, z_ref,
               xb_scr, v_scr, wagg_scr, t_scr, wout_scr):
    m_rows = v_scr.shape[0]
    half = v_scr.shape[1] // 2
    ncol = half // LANES
    chan_shift = S5_GROUP.bit_length() - 1
    state_shift = S5_STATE.bit_length() - 1
    _expand_block_diag(agg_ref, rep_state_ref, wagg_scr, chan_shift, state_shift)
    _expand_block_diag(toe_ref, rep_chan_ref, t_scr, chan_shift, chan_shift)
    _expand_block_diag(proj_ref, rep_state_ref, wout_scr, chan_shift, state_shift)
    for i in range(S5_BLOCK):
        xb_scr[:, i * LANES:(i + 1) * LANES] = u_ref[pl.ds(i, m_rows, stride=S5_BLOCK), :].astype(bf16)
    xb = xb_scr[...]
    v_scr[...] = jnp.dot(xb, wagg_scr[...], preferred_element_type=f32)

    tab = dtab_ref[0]
    sub = lax.broadcasted_iota(jnp.int32, (SUBLANES, LANES), 0)

    def col(j, part):
        lo = part * half + j * LANES
        return slice(lo, lo + LANES)

    def bcast(row):
        return jnp.broadcast_to(row, (SUBLANES, LANES))

    def body(r, carry):
        r0 = pl.multiple_of(r * SUBLANES, SUBLANES)
        new = []
        for j in range(ncol):
            cr, ci = carry[2 * j], carry[2 * j + 1]
            xr = v_scr[pl.ds(r0, SUBLANES), col(j, 0)]
            xi = v_scr[pl.ds(r0, SUBLANES), col(j, 1)]
            for shift in (1, 2, 4):
                ar = bcast(tab[shift - 1:shift, col(j, 0)])
                ai = bcast(tab[shift - 1:shift, col(j, 1)])
                sr = jnp.where(sub >= shift, pltpu.roll(xr, shift, 0), 0.0)
                si = jnp.where(sub >= shift, pltpu.roll(xi, shift, 0), 0.0)
                xr, xi = _cmul_add(xr, xi, ar, ai, sr, si)
            xr, xi = _cmul_add(xr, xi, tab[:, col(j, 0)], tab[:, col(j, 1)], cr, ci)
            v_scr[pl.ds(r0, SUBLANES), col(j, 0)] = jnp.where(sub >= 1, pltpu.roll(xr, 1, 0), cr)
            v_scr[pl.ds(r0, SUBLANES), col(j, 1)] = jnp.where(sub >= 1, pltpu.roll(xi, 1, 0), ci)
            new.append(bcast(xr[SUBLANES - 1:SUBLANES, :]))
            new.append(bcast(xi[SUBLANES - 1:SUBLANES, :]))
        return tuple(new)

    zero = jnp.zeros((SUBLANES, LANES), f32)
    lax.fori_loop(0, m_rows // SUBLANES, body, (zero,) * (2 * ncol))

    sp = v_scr[...].astype(bf16)
    for i0 in range(0, S5_BLOCK, 2):
        cs = slice(i0 * LANES, (i0 + 2) * LANES)
        nt = (((1,), (1,)), ((), ()))
        live = (i0 + 2) * LANES
        y = (lax.dot_general(xb[:, :live], t_scr[cs, :live], nt, preferred_element_type=f32)
             + lax.dot_general(sp, wout_scr[cs, :], nt, preferred_element_type=f32))
        for i in (i0, i0 + 1):
            yi = y[:, (i - i0) * LANES:(i - i0 + 1) * LANES] + d_ref[...] * u_ref[pl.ds(i, m_rows, stride=S5_BLOCK), :]
            z_ref[pl.ds(i, m_rows, stride=S5_BLOCK), :] = 0.5 * yi * (1.0 + lax.erf(yi * (1.0 / math.sqrt(2.0))))


def _s5_core(u, agg, toe, proj, dtab, d_skip):
    n, d = u.shape
    nq = d // LANES
    m_rows = n // S5_BLOCK
    width = agg.shape[1]
    col = jnp.arange(width)[None, :]
    lane = jnp.arange(LANES)[:, None]
    half = width // 2
    rep_state = ((col // half == lane // S5_STATE) & (col % S5_STATE == lane % S5_STATE)).astype(bf16)
    rep_chan = ((col // LANES == lane // S5_GROUP) & (col % S5_GROUP == lane % S5_GROUP)).astype(bf16)
    wspec = pl.BlockSpec((1, width, LANES), lambda q: (q, 0, 0))
    rspec = pl.BlockSpec((LANES, width), lambda q: (0, 0))
    return pl.pallas_call(
        _s5_kernel,
        grid=(nq,),
        in_specs=[pl.BlockSpec((n, LANES), lambda q: (0, q)), wspec, wspec, wspec, rspec, rspec,
                  pl.BlockSpec((1, SUBLANES, width), lambda q: (q, 0, 0)),
                  pl.BlockSpec((1, LANES), lambda q: (0, q))],
        out_specs=pl.BlockSpec((n, LANES), lambda q: (0, q)),
        out_shape=jax.ShapeDtypeStruct((n, d), f32),
        scratch_shapes=[pltpu.VMEM((m_rows, width), bf16), pltpu.VMEM((m_rows, width), f32),
                        pltpu.VMEM((width, width), bf16), pltpu.VMEM((width, width), bf16),
                        pltpu.VMEM((width, width), bf16)],
        compiler_params=_params("parallel"),
        name="s5_core",
    )(u, agg, toe, proj, rep_state, rep_chan, dtab, d_skip)


def _s5_derive(b_re, b_im, c_re, c_im, a_re, a_im, log_step):
    highest = lax.Precision.HIGHEST
    g, p, c = b_re.shape
    nq = g // S5_CHUNK_GROUPS
    dt = jnp.exp(log_step)[:, None]
    mag = jnp.exp(a_re * dt)
    lb_re = mag * jnp.cos(a_im * dt)
    lb_im = mag * jnp.sin(a_im * dt)
    den = a_re * a_re + a_im * a_im
    n_re = lb_re - 1.0
    n_im = lb_im
    f_re = (n_re * a_re + n_im * a_im) / den
    f_im = (n_im * a_re - n_re * a_im) / den
    bb_re = f_re[..., None] * b_re - f_im[..., None] * b_im
    bb_im = f_re[..., None] * b_im + f_im[..., None] * b_re

    def powers(br, bi, count):
        rs, is_ = [br], [bi]
        for _ in range(count - 1):
            rs.append(rs[-1] * br - is_[-1] * bi)
            is_.append(rs[-2] * bi + is_[-1] * br)
        return rs, is_

    pr, pi = powers(lb_re, lb_im, S5_BLOCK)
    lam_r = [jnp.ones_like(lb_re)] + pr
    lam_i = [jnp.zeros_like(lb_im)] + pi
    width = S5_BLOCK * LANES

    def compact(w):
        w = w.reshape(S5_BLOCK, nq, S5_CHUNK_GROUPS * c, LANES)
        return w.transpose(1, 0, 2, 3).reshape(nq, width, LANES).astype(bf16)

    bt_re = bb_re.transpose(0, 2, 1)
    bt_im = bb_im.transpose(0, 2, 1)

    ar = jnp.stack([lam_r[S5_BLOCK - 1 - i] for i in range(S5_BLOCK)])[:, :, None, :]
    ai = jnp.stack([lam_i[S5_BLOCK - 1 - i] for i in range(S5_BLOCK)])[:, :, None, :]
    wagg = compact(jnp.concatenate([ar * bt_re - ai * bt_im, ar * bt_im + ai * bt_re], axis=-1))

    ar = jnp.stack(lam_r[1:S5_BLOCK + 1])[:, :, None, :]
    ai = jnp.stack(lam_i[1:S5_BLOCK + 1])[:, :, None, :]
    wout = compact(jnp.concatenate([c_re * ar - c_im * ai, -(c_re * ai + c_im * ar)], axis=-1))

    ar = jnp.stack(lam_r[:S5_BLOCK])[:, :, None, :]
    ai = jnp.stack(lam_i[:S5_BLOCK])[:, :, None, :]
    kj = (jnp.einsum('jgcp,gpd->jgcd', c_re * ar - c_im * ai, bb_re, precision=highest)
          - jnp.einsum('jgcp,gpd->jgcd', c_re * ai + c_im * ar, bb_im, precision=highest))
    none = jnp.zeros_like(kj[0])
    tmat = compact(jnp.stack([jnp.concatenate([kj[i - a] if i >= a else none for a in range(S5_BLOCK)], axis=-1)
                              for i in range(S5_BLOCK)]))

    dr, di = powers(lam_r[S5_BLOCK], lam_i[S5_BLOCK], SUBLANES)
    dtab = jnp.stack([jnp.stack(dr), jnp.stack(di)], axis=1)
    dtab = dtab.reshape(SUBLANES, 2, nq, S5_CHUNK_GROUPS * p).transpose(2, 0, 1, 3).reshape(nq, SUBLANES, width)
    return wagg, tmat, wout, dtab


def _conv_kernel(v_ref, halo_ref, w_ref, b_ref, g_ref, beta_ref, o_ref, ext_scr, rot_scr, acc_scr):
    tt = v_ref.shape[0]
    pad = halo_ref.shape[0]
    first = pl.program_id(0) == 0
    ext_scr[0:pad, :] = jnp.where(first, 0.0, halo_ref[...])
    ext_scr[pad:pad + tt, :] = v_ref[...]
    span = rot_scr.shape[1]
    for sh in range(1, SUBLANES):
        rot_scr[sh - 1] = ext_scr[sh:sh + span, :]
    rows, cols = 64, 128
    for c0 in range(0, v_ref.shape[1], cols):
        for r0 in range(0, tt, rows):
            acc = jnp.broadcast_to(b_ref[:, c0:c0 + cols], (rows, cols))
            for k in range(CONV_WIDTH):
                whole, sh = divmod(pad - (CONV_WIDTH - 1) + k, SUBLANES)
                off = whole * SUBLANES + r0
                if sh == 0:
                    win = ext_scr[off:off + rows, c0:c0 + cols]
                else:
                    win = rot_scr[sh - 1, off:off + rows, c0:c0 + cols]
                acc = acc + w_ref[k:k + 1, c0:c0 + cols] * win
            acc_scr[r0:r0 + rows, c0:c0 + cols] = acc
    y = _layer_norm(acc_scr[...], g_ref[...], beta_ref[...])
    o_ref[...] = (y * jax.nn.sigmoid(y)).astype(o_ref.dtype)


def _conv_module(v, w_dw, b_dw, g, beta, tt=128, pad=32):
    n, d = v.shape
    w_pad = jnp.zeros((pad, d), f32).at[:CONV_WIDTH].set(w_dw)
    ratio = tt // pad
    row = lambda i: (i, 0)
    fixed = lambda i: (0, 0)
    return pl.pallas_call(
        _conv_kernel,
        grid=(n // tt,),
        in_specs=[pl.BlockSpec((tt, d), row),
                  pl.BlockSpec((pad, d), lambda i: (jnp.maximum(i * ratio - 1, 0), 0)),
                  pl.BlockSpec((pad, d), fixed), pl.BlockSpec((1, d), fixed),
                  pl.BlockSpec((1, d), fixed), pl.BlockSpec((1, d), fixed)],
        out_specs=pl.BlockSpec((tt, d), row),
        out_shape=jax.ShapeDtypeStruct((n, d), bf16),
        scratch_shapes=[pltpu.VMEM((tt + pad, d), f32),
                        pltpu.VMEM((SUBLANES - 1, tt + pad - SUBLANES, d), f32),
                        pltpu.VMEM((tt, d), f32)],
        compiler_params=_params("parallel"),
        name="conv_module",
    )(v, v, w_pad, b_dw, g, beta)


def _pool_kernel(h_ref, halo_ref, w_ref, scale_ref, g_ref, beta_ref, of_ref, ob_ref, hp_ref, ext_scr):
    tt = h_ref.shape[0]
    pad = halo_ref.shape[0]
    i = pl.program_id(0)
    ext_scr[0:pad, :] = jnp.where(i == 0, 0.0, halo_ref[...])
    ext_scr[pad:pad + tt, :] = h_ref[...]
    t = i * tt + lax.broadcasted_iota(jnp.int32, (tt, 1), 0)
    ys = []
    for k, win in enumerate(POOL_WINDOWS):
        cs = slice(k * POOL_CH, (k + 1) * POOL_CH)
        x = h_ref[:, cs]
        acc = x
        for j in range(1, win):
            acc = acc + ext_scr[pad - j:pad - j + tt, cs]
        cnt = jnp.minimum(t + 1, win).astype(f32)
        ys.append(_dot(acc / cnt - x, w_ref[k]))
    y = jnp.concatenate(ys, axis=1) * scale_ref[...]
    out = _layer_norm(DN_ALPHA * h_ref[...] + y, g_ref[...], beta_ref[...])
    of_ref[...] = out
    ob_ref[...] = out.astype(bf16)
    _store_packed(out, hp_ref)


def _pool_layer(h, w_grp, scale, g, beta, tt=256, pad=16):
    n, d = h.shape
    ratio = tt // pad
    row = lambda i: (i, 0)
    fixed = lambda i: (0, 0)
    return pl.pallas_call(
        _pool_kernel,
        grid=(n // tt,),
        in_specs=[pl.BlockSpec((tt, d), row),
                  pl.BlockSpec((pad, d), lambda i: (jnp.maximum(i * ratio - 1, 0), 0)),
                  pl.BlockSpec(w_grp.shape, lambda i: (0, 0, 0)),
                  pl.BlockSpec((1, d), fixed), pl.BlockSpec((1, d), fixed), pl.BlockSpec((1, d), fixed)],
        out_specs=[pl.BlockSpec((tt, d), row), pl.BlockSpec((tt, d), row),
                   pl.BlockSpec((tt * PACK_ROWS, LANES), row)],
        out_shape=[jax.ShapeDtypeStruct((n, d), f32), jax.ShapeDtypeStruct((n, d), bf16),
                   jax.ShapeDtypeStruct((n * PACK_ROWS, LANES), jnp.uint32)],
        scratch_shapes=[pltpu.VMEM((tt + pad, d), f32)],
        compiler_params=_params("parallel"),
        name="pool_layer",
    )(h, h, w_grp, scale, g, beta)


R_E1, R_E2, R_G1, R_G2, R_RANK1, R_RANK2 = range(6)


def _router_kernel(h_ref, whi_ref, wlo_ref, b_ref, info_ref, info_t_ref, cnt_ref):
    tm = h_ref.shape[0]

    @pl.when(pl.program_id(0) == 0)
    def _():
        cnt_ref[...] = jnp.zeros_like(cnt_ref)

    h = h_ref[...]
    h_hi = h.astype(bf16)
    h_lo = (h - h_hi.astype(f32)).astype(bf16)
    logits = (jnp.dot(h_hi, whi_ref[...], preferred_element_type=f32)
              + jnp.dot(h_lo, whi_ref[...], preferred_element_type=f32)
              + jnp.dot(h_hi, wlo_ref[...], preferred_element_type=f32)) + b_ref[...]
    lane = lax.broadcasted_iota(jnp.int32, logits.shape, 1)
    real = lane < N_EXPERTS
    e = jnp.exp(logits - jnp.max(logits, axis=-1, keepdims=True))
    probs = e / jnp.sum(e, axis=-1, keepdims=True)

    a = probs
    b = pltpu.roll(probs, 1, 1)
    c = pltpu.roll(probs, 2, 1)
    d = pltpu.roll(probs, 3, 1)
    hi1, lo1 = jnp.maximum(a, b), jnp.minimum(a, b)
    hi2, lo2 = jnp.maximum(c, d), jnp.minimum(c, d)
    score = jnp.maximum(hi1, hi2) + jnp.maximum(jnp.minimum(hi1, hi2), jnp.maximum(lo1, lo2))
    best = None
    g_sel = None
    for grp in range(N_EXPERTS // EXPERTS_PER_GROUP):
        last = grp * EXPERTS_PER_GROUP + EXPERTS_PER_GROUP - 1
        s = jnp.max(jnp.where(lane == last, score, -1.0), axis=-1, keepdims=True)
        if grp == 0:
            best, g_sel = s, jnp.zeros_like(s, dtype=jnp.int32)
        else:
            better = s > best
            best = jnp.where(better, s, best)
            g_sel = jnp.where(better, grp, g_sel)

    in_grp = real & ((lane // EXPERTS_PER_GROUP) == g_sel)
    masked = jnp.where(in_grp, probs, -1.0)
    lane_f = lane.astype(f32)
    p1 = jnp.max(masked, axis=-1, keepdims=True)
    e1 = jnp.min(jnp.where(masked == p1, lane_f, float(LANES)), axis=-1, keepdims=True)
    masked2 = jnp.where(lane_f == e1, -2.0, masked)
    p2 = jnp.max(masked2, axis=-1, keepdims=True)
    e2 = jnp.min(jnp.where(masked2 == p2, lane_f, float(LANES)), axis=-1, keepdims=True)
    tot = p1 + p2

    oh1 = (lane_f == e1).astype(f32)
    oh2 = (lane_f == e2).astype(f32)
    both = oh1 + oh2
    ri = lax.broadcasted_iota(jnp.int32, (tm, tm), 0)
    ci = lax.broadcasted_iota(jnp.int32, (tm, tm), 1)
    tri = (ci < ri).astype(bf16)
    before = cnt_ref[0:1, :] + jnp.dot(tri, both.astype(bf16), preferred_element_type=f32)
    rank1 = jnp.sum(before * oh1, axis=-1, keepdims=True)
    rank2 = jnp.sum(before * oh2, axis=-1, keepdims=True)
    cnt_ref[...] = cnt_ref[...] + jnp.sum(both, axis=0, keepdims=True)

    info = jnp.zeros(logits.shape, f32)
    for slot, val in ((R_E1, e1), (R_E2, e2), (R_G1, p1 / tot), (R_G2, p2 / tot),
                      (R_RANK1, rank1), (R_RANK2, rank2)):
        info = jnp.where(lane == slot, val, info)
    info_ref[...] = info
    info_t_ref[...] = info.T[:SUBLANES, :]


def _router(h, router_w, router_b, tm=512):
    n, d = h.shape
    w = jnp.zeros((d, LANES), f32).at[:, :N_EXPERTS].set(router_w)
    w_hi = w.astype(bf16)
    w_lo = (w - w_hi.astype(f32)).astype(bf16)
    b = jnp.full((1, LANES), -1e30, f32).at[0, :N_EXPERTS].set(router_b)
    return pl.pallas_call(
        _router_kernel,
        grid=(n // tm,),
        in_specs=[pl.BlockSpec((tm, d), lambda i: (i, 0)), pl.BlockSpec((d, LANES), lambda i: (0, 0)),
                  pl.BlockSpec((d, LANES), lambda i: (0, 0)), pl.BlockSpec((1, LANES), lambda i: (0, 0))],
        out_specs=[pl.BlockSpec((tm, LANES), lambda i: (i, 0)), pl.BlockSpec((SUBLANES, tm), lambda i: (0, i)),
                   pl.BlockSpec((SUBLANES, LANES), lambda i: (0, 0))],
        out_shape=[jax.ShapeDtypeStruct((n, LANES), f32), jax.ShapeDtypeStruct((SUBLANES, n), f32),
                   jax.ShapeDtypeStruct((SUBLANES, LANES), f32)],
        compiler_params=_params("arbitrary"),
        name="router",
    )(h, w_hi, w_lo, b)


def _token_copy(src_ref, dst_ref, sem, src_tok, dst_tok, rows):
    src = pl.multiple_of(src_tok * rows, SUBLANES)
    dst = pl.multiple_of(dst_tok * rows, SUBLANES)
    return pltpu.make_async_copy(src_ref.at[pl.ds(src, rows)], dst_ref.at[pl.ds(dst, rows)], sem)


def _expert_kernel(blk_e_ref, blk_valid_ref, blk_first_ref, blk_next_ref, blk_wmask_ref, row_tok_ref,
                   hp_ref, wg_hbm, wu_hbm, wd_hbm, o_ref,
                   xg_scr, x_scr, stage_g, stage_u, stage_d, wg_scr, wu_scr, wd_scr, sems, wsems):
    i = pl.program_id(0)
    tm = x_scr.shape[0]
    nblk = pl.num_programs(0)

    def weight_copies(e):
        return (pltpu.make_async_copy(wg_hbm.at[e], stage_g, wsems.at[0]),
                pltpu.make_async_copy(wu_hbm.at[e], stage_u, wsems.at[1]),
                pltpu.make_async_copy(wd_hbm.at[e], stage_d, wsems.at[2]))

    @pl.when(i == 0)
    def _():
        for cp in weight_copies(blk_e_ref[0]):
            cp.start(priority=1)

    def token_copy(slot, j, tok):
        return _token_copy(hp_ref, xg_scr.at[slot], sems.at[slot], tok, j, PACK_ROWS)

    def wait_gather(slot):
        def body(j0, carry):
            for k in range(GATHER_UNROLL):
                token_copy(slot, 0, 0).wait()
            return carry
        lax.fori_loop(0, tm // GATHER_UNROLL, body, 0)

    @pl.when(i == 0)
    def _():
        def body(j0, carry):
            for k in range(GATHER_UNROLL):
                j = j0 * GATHER_UNROLL + k
                token_copy(0, j, row_tok_ref[j]).start()
            return carry
        lax.fori_loop(0, tm // GATHER_UNROLL, body, 0)

    @pl.when(blk_first_ref[i] != 0)
    def _():
        for cp in weight_copies(blk_e_ref[i]):
            cp.wait()
        chunk = 256
        for stage, dst in ((stage_g, wg_scr), (stage_u, wu_scr), (stage_d, wd_scr)):
            for r in range(0, stage.shape[0], chunk):
                dst[r:r + chunk, :] = stage[r:r + chunk, :].astype(bf16)

    this_slot = i % 2
    next_slot = (i + 1) % 2

    @pl.when(blk_valid_ref[i] != 0)
    def _():
        wait_gather(this_slot)
        _load_packed(xg_scr.at[this_slot], x_scr)
        nxt_blk = jnp.minimum(i + 1, nblk - 1)
        for j in range(tm):
            token_copy(next_slot, j, row_tok_ref[nxt_blk * tm + j]).start()
        x = x_scr[...]
        gate = _dot(x, wg_scr[...])
        hid = gate * jax.nn.sigmoid(gate) * _dot(x, wu_scr[...])
        y = _dot(hid, wd_scr[...])
        _store_packed(y, o_ref)

        mask = blk_wmask_ref[i]
        for k, cp in enumerate(weight_copies(jnp.maximum(blk_next_ref[i], 0))):
            @pl.when(((mask >> k) & 1) != 0)
            def _(cp=cp):
                cp.start(priority=1)

        @pl.when(i == nblk - 1)
        def _():
            wait_gather(next_slot)

    @pl.when(blk_valid_ref[i] == 0)
    def _():
        o_ref[...] = jnp.zeros_like(o_ref)

        @pl.when(blk_valid_ref[jnp.maximum(i - 1, 0)] != 0)
        def _():
            wait_gather(this_slot)


def _experts(hp, row_tok, blk_e, blk_valid, blk_first, blk_next, blk_wmask, w_gate, w_up, w_down):
    cap = row_tok.shape[0]
    _, d, de = w_gate.shape
    row = lambda i, *_: (i, 0)
    any_spec = pl.BlockSpec(memory_space=pl.ANY)
    return pl.pallas_call(
        _expert_kernel,
        grid_spec=pltpu.PrefetchScalarGridSpec(
            num_scalar_prefetch=6, grid=(cap // MOE_ROWS,),
            in_specs=[any_spec, any_spec, any_spec, any_spec],
            out_specs=pl.BlockSpec((MOE_ROWS * PACK_ROWS, LANES), row),
            scratch_shapes=[pltpu.VMEM((2, MOE_ROWS * PACK_ROWS, LANES), jnp.uint32),
                            pltpu.VMEM((MOE_ROWS, d), bf16),
                            pltpu.VMEM((d, de), f32), pltpu.VMEM((d, de), f32), pltpu.VMEM((de, d), f32),
                            pltpu.VMEM((d, de), bf16), pltpu.VMEM((d, de), bf16), pltpu.VMEM((de, d), bf16),
                            pltpu.SemaphoreType.DMA((2,)), pltpu.SemaphoreType.DMA((3,))]),
        out_shape=jax.ShapeDtypeStruct((cap * PACK_ROWS, LANES), jnp.uint32),
        compiler_params=_params("arbitrary"),
        name="experts",
    )(blk_e, blk_valid, blk_first, blk_next, blk_wmask, row_tok, hp, w_gate, w_up, w_down)


def _load_packed_f32(hp_ref, tm):
    words = [hp_ref[pl.ds(c, tm, stride=PACK_ROWS), :] for c in range(PACK_ROWS)]
    lo = [lax.bitcast_convert_type(w << 16, f32) for w in words]
    hi = [lax.bitcast_convert_type(w & jnp.uint32(0xFFFF0000), f32) for w in words]
    return jnp.concatenate(lo + hi, axis=1)


def _moe_ln_kernel(dest_ref, h_ref, y_hbm, info_ref, g_ref, beta_ref, of_ref, ob_ref, y1_scr, y2_scr, sems):
    i = pl.program_id(0)
    ntile = pl.num_programs(0)
    tm = h_ref.shape[0]
    n_tok = ntile * tm

    def token_copy(buf, which, j, row):
        scr = (y1_scr, y2_scr)[which]
        return _token_copy(y_hbm, scr.at[buf], sems.at[buf], row, j, PACK_ROWS)

    def wait_gather(buf):
        def body(j0, carry):
            for _ in range(2 * GATHER_UNROLL):
                token_copy(buf, 0, 0, 0).wait()
            return carry
        lax.fori_loop(0, tm // GATHER_UNROLL, body, 0)

    def start_gather(tile, buf):
        def body(j0, carry):
            for which in range(2):
                for k in range(GATHER_UNROLL):
                    j = j0 * GATHER_UNROLL + k
                    token_copy(buf, which, j, dest_ref[which * n_tok + tile * tm + j]).start(priority=k % 2)
            return carry
        lax.fori_loop(0, tm // GATHER_UNROLL, body, 0)

    this_buf = i % 2

    @pl.when(i == 0)
    def _():
        start_gather(0, 0)

    @pl.when(i + 1 < ntile)
    def _():
        start_gather(i + 1, (i + 1) % 2)

    wait_gather(this_buf)
    info = info_ref[...]
    g1 = info[:, R_G1:R_G1 + 1]
    g2 = info[:, R_G2:R_G2 + 1]
    f = g1 * _load_packed_f32(y1_scr.at[this_buf], tm) + g2 * _load_packed_f32(y2_scr.at[this_buf], tm)
    out = _layer_norm(DN_ALPHA * h_ref[...] + f, g_ref[...], beta_ref[...])
    of_ref[...] = out
    ob_ref[...] = out.astype(bf16)


def _moe_ln(h, y, dest, info, g, beta, tm=256):
    n, d = h.shape
    row = lambda i, *_: (i, 0)
    fixed = lambda i, *_: (0, 0)
    return pl.pallas_call(
        _moe_ln_kernel,
        grid_spec=pltpu.PrefetchScalarGridSpec(
            num_scalar_prefetch=1, grid=(n // tm,),
            in_specs=[pl.BlockSpec((tm, d), row), pl.BlockSpec(memory_space=pl.ANY),
                      pl.BlockSpec((tm, LANES), row), pl.BlockSpec((1, d), fixed), pl.BlockSpec((1, d), fixed)],
            out_specs=[pl.BlockSpec((tm, d), row), pl.BlockSpec((tm, d), row)],
            scratch_shapes=[pltpu.VMEM((2, tm * PACK_ROWS, LANES), jnp.uint32),
                            pltpu.VMEM((2, tm * PACK_ROWS, LANES), jnp.uint32), pltpu.SemaphoreType.DMA((2,))]),
        out_shape=[jax.ShapeDtypeStruct((n, d), f32), jax.ShapeDtypeStruct((n, d), bf16)],
        compiler_params=_params("arbitrary"),
        name="moe_ln",
    )(dest, h, y, info, g, beta)


def _moe_layer(hf, hp, router_w, router_b, w_gate, w_up, w_down, expert_base, g, beta):
    n, d = hf.shape
    info, info_t, cnt = _router(hf, router_w, router_b)
    experts = info_t[R_E1:R_E2 + 1].astype(jnp.int32)
    rank = info_t[R_RANK1:R_RANK2 + 1].astype(jnp.int32)
    counts = cnt[0, :N_EXPERTS].astype(jnp.int32)
    pcounts = ((counts + MOE_ROWS - 1) // MOE_ROWS) * MOE_ROWS
    pends = jnp.cumsum(pcounts)
    pstarts = pends - pcounts
    dest = rank
    for e in range(N_EXPERTS):
        dest = dest + jnp.where(experts == e, pstarts[e], 0)
    cap = 2 * n + N_EXPERTS * MOE_ROWS
    dest = dest.reshape(-1)
    tok = jnp.tile(jnp.arange(n, dtype=jnp.int32), 2)
    row_tok = (jnp.arange(cap, dtype=jnp.int32) % n).at[dest].set(tok, unique_indices=True)
    blk_start = jnp.arange(cap // MOE_ROWS, dtype=jnp.int32) * MOE_ROWS
    blk_e = jnp.sum((blk_start[:, None] >= pends[None, :]).astype(jnp.int32), axis=1)
    blk_e = jnp.minimum(blk_e, N_EXPERTS - 1)
    blk_valid = (blk_start < pends[-1]).astype(jnp.int32)
    last_e = jnp.max(jnp.where(counts > 0, jnp.arange(N_EXPERTS, dtype=jnp.int32), 0))
    blk_e = jnp.where(blk_valid != 0, blk_e, last_e)
    blk_first = blk_valid * jnp.concatenate([jnp.ones((1,), jnp.int32), (blk_e[1:] != blk_e[:-1]).astype(jnp.int32)])
    ids = jnp.arange(N_EXPERTS, dtype=jnp.int32)
    later = (ids[None, :] > ids[:, None]) & (counts[None, :] > 0)
    next_e = jnp.min(jnp.where(later, ids[None, :], N_EXPERTS), axis=1)
    next_e = jnp.where(next_e < N_EXPERTS, next_e + expert_base, -1).astype(jnp.int32)
    mine = blk_e[:, None] == ids[None, :]
    blk_next = jnp.sum(jnp.where(mine, next_e[None, :], 0), axis=1)
    run_pos = blk_start // MOE_ROWS - jnp.sum(jnp.where(mine, pstarts[None, :] // MOE_ROWS, 0), axis=1)
    run_len = jnp.sum(jnp.where(mine, pcounts[None, :] // MOE_ROWS, 0), axis=1)
    mats = jnp.arange(3, dtype=jnp.int32)
    start_at = jnp.minimum(mats[None, :], run_len[:, None] - 1)
    starts = (run_pos[:, None] == start_at) & (blk_next[:, None] >= 0) & (blk_valid[:, None] != 0)
    blk_wmask = jnp.sum(starts.astype(jnp.int32) << mats[None, :], axis=1)

    y = _experts(hp, row_tok, blk_e + expert_base, blk_valid, blk_first, blk_next, blk_wmask,
                 w_gate, w_up, w_down)
    return _moe_ln(hf, y, dest, info, g, beta)


def kernel(x, s5_w_in, s5_b_re, s5_b_im, s5_c_re, s5_c_im, s5_a_re, s5_a_im, s5_log_step, s5_d, s5_w_glu, s5_w_out, cv_w_pw1, cv_b_pw1, cv_w_dw, cv_b_dw, cv_ln_g, cv_ln_b, cv_w_pw2, cv_b_pw2, pl_w, pl_scale, router_w, router_b, moe_w_gate, moe_w_up, moe_w_down, ln_mix_g, ln_mix_b, ln_ffn_g, ln_ffn_b):
    bsz, seq, d = x.shape
    hf = x.reshape(bsz * seq, d)
    hb = hf.astype(bf16)
    row = lambda v: v.reshape(1, -1)
    w_gate = moe_w_gate.reshape((-1,) + moe_w_gate.shape[2:])
    w_up = moe_w_up.reshape((-1,) + moe_w_up.shape[2:])
    w_down = moe_w_down.reshape((-1,) + moe_w_down.shape[2:])

    for i in range(DEPTH):
        mixer, j = i % N_MIXERS, i // N_MIXERS
        g_mix, b_mix = row(ln_mix_g[i]), row(ln_mix_b[i])
        if mixer == 0:
            u = _mm(hb, s5_w_in[j].astype(bf16))
            wagg, tmat, wout, dtab = _s5_derive(s5_b_re[j], s5_b_im[j], s5_c_re[j], s5_c_im[j],
                                                s5_a_re[j], s5_a_im[j], s5_log_step[j])
            z = _s5_core(u, wagg, tmat, wout, dtab, row(s5_d[j]))
            v = _mm_glu(z, s5_w_glu[j].astype(bf16), jnp.zeros((1, 2 * d), f32), bf16)
            hf, hb, hp = _mm_res_ln(v, s5_w_out[j].astype(bf16), jnp.zeros((1, d), f32), hf, g_mix, b_mix)
        elif mixer == 1:
            v = _mm_glu(hb, cv_w_pw1[j].astype(bf16), row(cv_b_pw1[j]), f32)
            cv = _conv_module(v, cv_w_dw[j], row(cv_b_dw[j]), row(cv_ln_g[j]), row(cv_ln_b[j]))
            hf, hb, hp = _mm_res_ln(cv, cv_w_pw2[j].astype(bf16), row(cv_b_pw2[j]), hf, g_mix, b_mix)
        else:
            hf, hb, hp = _pool_layer(hf, pl_w[j].astype(bf16), row(pl_scale[j]), g_mix, b_mix)
        hf, hb = _moe_layer(hf, hp, router_w, router_b, w_gate, w_up, w_down, i * N_EXPERTS,
                            row(ln_ffn_g[i]), row(ln_ffn_b[i]))
    return hf.reshape(bsz, seq, d)
```
